```python
import jax, jax.numpy as jnp
from jax import lax
import numpy as np

D_MODEL = 1024
BATCH = 4
SEQ = 4096
DEPTH = 1

GRID_W = 64
HEAD_DIM = 64
D_ATTN = D_MODEL // 2
N_Q_HEADS = D_ATTN // HEAD_DIM
N_KV_HEADS = 2
Q_PER_KV = N_Q_HEADS // N_KV_HEADS
D_KV = N_KV_HEADS * HEAD_DIM
Q_BLOCK = 128
ROPE_THETA = 10000.0
AXIS_DIM = HEAD_DIM // 2
N_FREQ = AXIS_DIM // 2
D_LRU = D_MODEL // 2
LRU_BLOCKS = 8
LRU_BLOCK_W = D_LRU // LRU_BLOCKS
LRU_C = 8.0
CONV_W = 4
CONV_PAD = (2, 1)
N_DIR = 2
D_MIX = D_ATTN + D_LRU
D_IN = D_ATTN + 2 * D_KV + 2 * D_LRU
D_FF = 4 * D_MODEL
D_PLE = 256
NORM_EPS = 1e-6

kernel_name = "hybrid_gqa_rglru_encoder_layer"


def rms_norm(x, g):
    xf = x.astype(jnp.float32)
    y = xf * lax.rsqrt(jnp.mean(xf * xf, axis=-1, keepdims=True) + NORM_EPS)
    return (y * g.astype(jnp.float32)).astype(x.dtype)


def _rotate(x, cos, sin):
    x1, x2 = jnp.split(x, 2, axis=-1)
    return jnp.concatenate([x1 * cos - x2 * sin, x2 * cos + x1 * sin], axis=-1)


def axial_rope(x, cos_r, sin_r, cos_c, sin_c):
    xf = x.astype(jnp.float32)
    xr, xc = jnp.split(xf, 2, axis=-1)
    out = jnp.concatenate([_rotate(xr, cos_r, sin_r), _rotate(xc, cos_c, sin_c)], axis=-1)
    return out.astype(x.dtype)


def grid_rope_tables(seq_len):
    rows = seq_len // GRID_W
    row = jnp.repeat(jnp.arange(rows, dtype=jnp.float32), GRID_W)
    col = jnp.tile(jnp.arange(GRID_W, dtype=jnp.float32), rows)
    inv_freq = ROPE_THETA ** (-jnp.arange(N_FREQ, dtype=jnp.float32) / N_FREQ)
    ang_r = row[:, None, None] * inv_freq
    ang_c = col[:, None, None] * inv_freq
    return jnp.cos(ang_r), jnp.sin(ang_r), jnp.cos(ang_c), jnp.sin(ang_c)


def block_attention(q, k, v):
    b, s = q.shape[0], q.shape[1]
    nb = s // Q_BLOCK
    scale = HEAD_DIM ** -0.5
    qb = q.reshape(b, nb, Q_BLOCK, N_KV_HEADS, Q_PER_KV, HEAD_DIM).transpose(1, 0, 2, 3, 4, 5)

    def attend(q_blk):
        sc = jnp.einsum('bqhgd,bkhd->bhgqk', q_blk, k, preferred_element_type=jnp.float32) * scale
        pr = jax.nn.softmax(sc, axis=-1)
        return jnp.einsum('bhgqk,bkhd->bqhgd', pr.astype(v.dtype), v)

    o = lax.map(attend, qb)
    return o.transpose(1, 0, 2, 3, 4, 5).reshape(b, s, D_ATTN)


def _lin_combine(c1, c2):
    a1, b1 = c1
    a2, b2 = c2
    return a1 * a2, a2 * b1 + b2


def bidirectional_rglru(xc, wa, ba, wx, bx, lam):
    b, s = xc.shape[0], xc.shape[1]
    xh = xc.reshape(b, s, LRU_BLOCKS, LRU_BLOCK_W)
    ga = jnp.einsum('bsnk,enkj->ebsnj', xh, wa).reshape(N_DIR, b, s, D_LRU) + ba[:, None, None, :]
    gx = jnp.einsum('bsnk,enkj->ebsnj', xh, wx).reshape(N_DIR, b, s, D_LRU) + bx[:, None, None, :]
    r = jax.nn.sigmoid(ga.astype(jnp.float32))
    i = jax.nn.sigmoid(gx.astype(jnp.float32))
    log_a = LRU_C * r * jax.nn.log_sigmoid(lam.astype(jnp.float32))[:, None, None, :]
    a = jnp.exp(log_a)
    u = jnp.sqrt(-jnp.expm1(2.0 * log_a)) * (i * xc.astype(jnp.float32)[None])
    _, h_fwd = lax.associative_scan(_lin_combine, (a[0], u[0]), axis=1)
    _, h_bwd = lax.associative_scan(_lin_combine, (a[1], u[1]), axis=1, reverse=True)
    return (h_fwd + h_bwd).astype(xc.dtype)


def setup_inputs(seed: int = 0) -> dict:
    key = jax.random.key(seed)
    ks = jax.random.split(key, 24)
    f32 = jnp.float32
    nrm = lambda k, shape, s: jax.random.normal(k, shape, f32) * s
    gain = lambda k, shape: 1.0 + 0.02 * jax.random.normal(k, shape, f32)
    u = jax.random.uniform(ks[12], (DEPTH, N_DIR, D_LRU), f32, 0.9, 0.999)
    a0 = u ** (1.0 / LRU_C)
    lam = jnp.log(a0) - jnp.log1p(-a0)
    return {
        "x": nrm(ks[0], (BATCH, SEQ, D_MODEL), 1.0),
        "p": nrm(ks[1], (DEPTH, BATCH, SEQ, D_PLE), 1.0),
        "mix_norm": gain(ks[2], (DEPTH, D_MODEL)),
        "w_in": nrm(ks[3], (DEPTH, D_MODEL, D_IN), D_MODEL ** -0.5),
        "q_norm": gain(ks[4], (DEPTH, HEAD_DIM)),
        "k_norm": gain(ks[5], (DEPTH, HEAD_DIM)),
        "conv_w": nrm(ks[6], (DEPTH, CONV_W, D_LRU), CONV_W ** -0.5),
        "conv_b": nrm(ks[7], (DEPTH, D_LRU), 0.01),
        "lru_wa": nrm(ks[8], (DEPTH, N_DIR, LRU_BLOCKS, LRU_BLOCK_W, LRU_BLOCK_W), LRU_BLOCK_W ** -0.5),
        "lru_ba": nrm(ks[9], (DEPTH, N_DIR, D_LRU), 0.01),
        "lru_wx": nrm(ks[10], (DEPTH, N_DIR, LRU_BLOCKS, LRU_BLOCK_W, LRU_BLOCK_W), LRU_BLOCK_W ** -0.5),
        "lru_bx": nrm(ks[11], (DEPTH, N_DIR, D_LRU), 0.01),
        "lru_lambda": lam,
        "attn_out_norm": gain(ks[13], (DEPTH, D_ATTN)),
        "lru_out_norm": gain(ks[14], (DEPTH, D_LRU)),
        "w_out": nrm(ks[15], (DEPTH, D_MIX, D_MODEL), D_MIX ** -0.5),
        "mlp_norm": gain(ks[16], (DEPTH, D_MODEL)),
        "w_up": nrm(ks[17], (DEPTH, D_MODEL, D_FF), D_MODEL ** -0.5),
        "w_down": nrm(ks[18], (DEPTH, D_FF, D_MODEL), D_FF ** -0.5),
        "ple_norm": gain(ks[19], (DEPTH, D_MODEL)),
        "w_ple_gate": nrm(ks[20], (DEPTH, D_MODEL, D_MODEL), D_MODEL ** -0.5),
        "w_ple_proj": nrm(ks[21], (DEPTH, D_PLE, D_MODEL), D_PLE ** -0.5),
        "final_norm": gain(ks[22], (D_MODEL,)),
    }


def reference(x, p, mix_norm, w_in, q_norm, k_norm, conv_w, conv_b, lru_wa, lru_ba,
              lru_wx, lru_bx, lru_lambda, attn_out_norm, lru_out_norm, w_out,
              mlp_norm, w_up, w_down, ple_norm, w_ple_gate, w_ple_proj, final_norm):
    b, s = x.shape[0], x.shape[1]
    cos_r, sin_r, cos_c, sin_c = grid_rope_tables(s)
    h = x
    for l in range(DEPTH):
        hn = rms_norm(h, mix_norm[l])
        z = hn @ w_in[l]
        q, k, v, xr, xg = jnp.split(
            z, [D_ATTN, D_ATTN + D_KV, D_ATTN + 2 * D_KV, D_ATTN + 2 * D_KV + D_LRU], axis=-1)
        q = rms_norm(q.reshape(b, s, N_Q_HEADS, HEAD_DIM), q_norm[l])
        k = rms_norm(k.reshape(b, s, N_KV_HEADS, HEAD_DIM), k_norm[l])
        v = v.reshape(b, s, N_KV_HEADS, HEAD_DIM)
        q = axial_rope(q, cos_r, sin_r, cos_c, sin_c)
        k = axial_rope(k, cos_r, sin_r, cos_c, sin_c)
        q = q.reshape(b, s, N_KV_HEADS, Q_PER_KV, HEAD_DIM)
        y_attn = block_attention(q, k, v)
        xc = lax.conv_general_dilated(
            xr, conv_w[l][:, None, :], window_strides=(1,), padding=[CONV_PAD],
            dimension_numbers=('NWC', 'WIO', 'NWC'), feature_group_count=D_LRU) + conv_b[l]
        hr = bidirectional_rglru(xc, lru_wa[l], lru_ba[l], lru_wx[l], lru_bx[l], lru_lambda[l])
        y_lru = hr * jax.nn.gelu(xg)
        y = jnp.concatenate([rms_norm(y_attn, attn_out_norm[l]), rms_norm(y_lru, lru_out_norm[l])], axis=-1)
        h = h + y @ w_out[l]
        m = rms_norm(h, mlp_norm[l]) @ w_up[l]
        h = h + jnp.square(jax.nn.relu(m)) @ w_down[l]
        gate = jax.nn.sigmoid((rms_norm(h, ple_norm[l]) @ w_ple_gate[l]).astype(jnp.float32)).astype(h.dtype)
        h = h + gate * (p[l] @ w_ple_proj[l])
    return rms_norm(h, final_norm)
```

```python
import functools

import jax
import jax.numpy as jnp
from jax import lax
from jax.experimental import pallas as pl
from jax.experimental.pallas import tpu as pltpu

D_MODEL = 1024
GRID_W = 64
HEAD_DIM = 64
D_ATTN = 512
N_Q_HEADS = 8
N_KV_HEADS = 2
Q_PER_KV = 4
D_KV = 128
ROPE_THETA = 10000.0
N_FREQ = 16
D_LRU = 512
LRU_BLOCK_W = 64
LRU_C = 8.0
CONV_W = 4
D_FF = 4096
D_PLE = 256
NORM_EPS = 1e-6

SUBLANES = 8
LANES = 128

N_SEG = SUBLANES
Q_TILE = 128
KEY_CHUNK = 512
GATE_CHUNK = 64
VMEM_LIMIT = 56 * 1024 * 1024

F32 = jnp.float32
BF16 = jnp.bfloat16


def _rms(x, axis):
    return lax.rsqrt(jnp.mean(x * x, axis=axis, keepdims=True) + NORM_EPS)


def _dot(a, b):
    return jnp.dot(a, b, preferred_element_type=F32)


def _dot_nt(a, b):
    return lax.dot_general(a, b, (((1,), (1,)), ((), ())), preferred_element_type=F32)


def _norm_rope_t(xt, gcol, cos_t, sin_t, n_heads):
    t = xt.shape[1]
    x3 = xt.reshape(n_heads, HEAD_DIM, t)
    xn = x3 * _rms(x3, 1) * gcol[None]
    x5 = xn.reshape(n_heads * 2, 2, N_FREQ, t)
    xs = jnp.concatenate([x5[:, 1:2], x5[:, 0:1]], axis=1).reshape(n_heads, HEAD_DIM, t)
    out = xn * cos_t[None] + xs * sin_t[None]
    return out.reshape(n_heads * HEAD_DIM, t)


def _in_proj_kernel(x_ref, gmix_ref, wqkv_t_ref, wrg_ref, gq_ref, gk_ref, cos_ref, sin_ref,
                    qt_ref, k_ref, vt_ref, xr_ref, xg_ref):
    x = x_ref[0]
    hn = (x * _rms(x, -1) * gmix_ref[...]).astype(BF16)
    zt = _dot_nt(wqkv_t_ref[...], hn)
    zr = _dot(hn, wrg_ref[...])
    cos_t = cos_ref[...]
    sin_t = sin_ref[...]
    qt = _norm_rope_t(zt[:D_ATTN], gq_ref[...] * (HEAD_DIM ** -0.5), cos_t, sin_t, N_Q_HEADS)
    kt = _norm_rope_t(zt[D_ATTN:D_ATTN + D_KV], gk_ref[...], cos_t, sin_t, N_KV_HEADS)
    qt_ref[0] = qt.astype(BF16)
    k_ref[0] = kt.T.astype(BF16)
    vt_ref[0] = zt[D_ATTN + D_KV:].astype(BF16)
    xr_ref[0] = zr[:, :D_LRU]
    xg_ref[0] = zr[:, D_LRU:]


def _in_proj(x, gmix, wqkv_t, wrg, gq, gk, cos_t, sin_t):
    b, s, d = x.shape
    t = s // N_SEG
    const = lambda *_: (0, 0)
    return pl.pallas_call(
        _in_proj_kernel,
        grid=(b, N_SEG),
        in_specs=[
            pl.BlockSpec((1, t, d), lambda i, j: (i, j, 0)),
            pl.BlockSpec((1, d), const),
            pl.BlockSpec(wqkv_t.shape, const),
            pl.BlockSpec(wrg.shape, const),
            pl.BlockSpec((HEAD_DIM, 1), const),
            pl.BlockSpec((HEAD_DIM, 1), const),
            pl.BlockSpec((HEAD_DIM, t), lambda i, j: (0, j)),
            pl.BlockSpec((HEAD_DIM, t), lambda i, j: (0, j)),
        ],
        out_specs=[
            pl.BlockSpec((1, D_ATTN, t), lambda i, j: (i, 0, j)),
            pl.BlockSpec((1, t, D_KV), lambda i, j: (i, j, 0)),
            pl.BlockSpec((1, D_KV, t), lambda i, j: (i, 0, j)),
            pl.BlockSpec((1, t, D_LRU), lambda i, j: (i, 0, j)),
            pl.BlockSpec((1, t, D_LRU), lambda i, j: (i, 0, j)),
        ],
        out_shape=[
            jax.ShapeDtypeStruct((b, D_ATTN, s), BF16),
            jax.ShapeDtypeStruct((b, s, D_KV), BF16),
            jax.ShapeDtypeStruct((b, D_KV, s), BF16),
            jax.ShapeDtypeStruct((b, t, N_SEG * D_LRU), F32),
            jax.ShapeDtypeStruct((b, t, N_SEG * D_LRU), F32),
        ],
        compiler_params=pltpu.CompilerParams(
            dimension_semantics=("arbitrary", "arbitrary"), vmem_limit_bytes=VMEM_LIMIT),
        name="in_proj",
    )(x, gmix, wqkv_t, wrg, gq, gk, cos_t, sin_t)


def _attn_kernel(qt_ref, k_ref, vt_ref, gcol_ref, y_ref, vta_ref):
    s = k_ref.shape[1]
    tq = qt_ref.shape[2]
    n_chunks = s // KEY_CHUNK

    @pl.when(pl.program_id(1) == 0)
    def _():
        vt = vt_ref[0]
        row = lax.broadcasted_iota(jnp.int32, vt.shape, 0)
        ones_row = jnp.where(row == HEAD_DIM, 1.0, 0.0).astype(BF16)
        swapped = jnp.concatenate([vt[HEAD_DIM:], vt[:HEAD_DIM]], axis=0)
        vta_ref[0] = jnp.where(row < HEAD_DIM, vt, ones_row)
        vta_ref[1] = jnp.where(row < HEAD_DIM, swapped, ones_row)

    qt = qt_ref[0]
    zeros = jnp.zeros((HEAD_DIM, tq), BF16)
    outs = []
    for kv in range(N_KV_HEADS):
        cols = []
        for j in range(Q_PER_KV):
            h = kv * Q_PER_KV + j
            qh = qt[h * HEAD_DIM:(h + 1) * HEAD_DIM]
            cols.append(jnp.concatenate([qh, zeros] if kv == 0 else [zeros, qh], axis=0))
        qst = jnp.concatenate(cols, axis=1)
        m = None
        acc = None
        for c in range(n_chunks):
            kc = k_ref[0, c * KEY_CHUNK:(c + 1) * KEY_CHUNK, :]
            sc = _dot(kc, qst)
            mc = jnp.max(sc, axis=0, keepdims=True)
            m_new = mc if m is None else jnp.maximum(m, mc)
            p = jnp.exp(sc - m_new).astype(BF16)
            pv = _dot(vta_ref[kv, :, c * KEY_CHUNK:(c + 1) * KEY_CHUNK], p)
            acc = pv if acc is None else acc * jnp.exp(m - m_new) + pv
            m = m_new
        o = acc[:HEAD_DIM] / acc[HEAD_DIM:HEAD_DIM + 1]
        for j in range(Q_PER_KV):
            outs.append(o[:, j * tq:(j + 1) * tq])
    ot = jnp.concatenate(outs, axis=0)
    yt = ot * _rms(ot, 0) * gcol_ref[...]
    y_ref[0] = yt.T.astype(BF16)


def _attention(qt, k, vt, gcol):
    b, _, s = qt.shape
    return pl.pallas_call(
        _attn_kernel,
        grid=(b, s // Q_TILE),
        in_specs=[
            pl.BlockSpec((1, D_ATTN, Q_TILE), lambda i, j: (i, 0, j)),
            pl.BlockSpec((1, s, D_KV), lambda i, j: (i, 0, 0)),
            pl.BlockSpec((1, D_KV, s), lambda i, j: (i, 0, 0)),
            pl.BlockSpec((D_ATTN, 1), lambda i, j: (0, 0)),
        ],
        out_specs=pl.BlockSpec((1, Q_TILE, D_ATTN), lambda i, j: (i, j, 0)),
        out_shape=jax.ShapeDtypeStruct((b, s, D_ATTN), BF16),
        scratch_shapes=[pltpu.VMEM((N_KV_HEADS, D_KV, s), BF16)],
        compiler_params=pltpu.CompilerParams(
            dimension_semantics=("arbitrary", "arbitrary"), vmem_limit_bytes=VMEM_LIMIT),
        name="attention",
    )(qt, k, vt, gcol)


def _shift_seg_down(x):
    row = lax.broadcasted_iota(jnp.int32, x.shape, 0)
    return jnp.where(row == 0, 0.0, pltpu.roll(x, 1, 0))


def _shift_seg_up(x):
    row = lax.broadcasted_iota(jnp.int32, x.shape, 0)
    return jnp.where(row == SUBLANES - 1, 0.0, pltpu.roll(x, SUBLANES - 1, 0))


def _sigmoid(x):
    return 0.5 * jnp.tanh(0.5 * x) + 0.5


def _gelu_tanh(x):
    return 0.5 * x * (1.0 + jnp.tanh(0.7978845608028654 * (x + 0.044715 * (x * x * x))))


def _lru_kernel(xr_ref, xg_ref, cw_ref, cb_ref, wg_ref, bg_ref, lam_ref, y_ref,
                xp_ref, a0_ref, u0_ref, a1_ref, u1_ref):
    t_len = xr_ref.shape[1]
    lanes = xr_ref.shape[3]
    n_chunks = t_len // GATE_CHUNK

    xp_ref[0] = _shift_seg_down(xr_ref[0, t_len - 2])
    xp_ref[1] = _shift_seg_down(xr_ref[0, t_len - 1])
    xp_ref[t_len + 2] = _shift_seg_up(xr_ref[0, 0])

    def copy_body(i, carry):
        t0 = pl.multiple_of(i * GATE_CHUNK, GATE_CHUNK)
        xp_ref[pl.ds(t0 + 2, GATE_CHUNK)] = xr_ref[0, pl.ds(t0, GATE_CHUNK)]
        return carry

    lax.fori_loop(0, n_chunks, copy_body, 0)

    cw = cw_ref[...]
    cb = cb_ref[...]
    lam = lam_ref[0]
    cl = LRU_C * (jnp.minimum(lam, 0.0) - jnp.log1p(jnp.exp(-jnp.abs(lam))))
    wg = wg_ref[0]
    bg = bg_ref[0]

    def gate_body(i, carry):
        t0 = pl.multiple_of(i * GATE_CHUNK, GATE_CHUNK)
        xc = cb[None]
        for kk in range(CONV_W):
            xc = xc + xp_ref[pl.ds(t0 + kk, GATE_CHUNK)] * cw[kk:kk + 1][None]
        xc2 = xc.reshape(GATE_CHUNK * SUBLANES, lanes)
        g = _dot(xc2.astype(BF16), wg) + bg
        for e, (a_ref, u_ref) in enumerate(((a0_ref, u0_ref), (a1_ref, u1_ref))):
            r = _sigmoid(g[:, e * lanes:(e + 1) * lanes])
            gi = _sigmoid(g[:, (2 + e) * lanes:(3 + e) * lanes])
            log_a = r * cl[e:e + 1]
            a = jnp.exp(log_a)
            u = jnp.sqrt(1.0 - a * a) * (gi * xc2)
            a_ref[pl.ds(t0, GATE_CHUNK)] = a.reshape(GATE_CHUNK, SUBLANES, lanes)
            u_ref[pl.ds(t0, GATE_CHUNK)] = u.reshape(GATE_CHUNK, SUBLANES, lanes)
        return carry

    lax.fori_loop(0, n_chunks, gate_body, 0)

    def scan_body(t, carry):
        h0, p0, h1, p1 = carry
        tb = t_len - 1 - t
        a0 = a0_ref[t]
        h0 = a0 * h0 + u0_ref[t]
        p0 = a0 * p0
        u0_ref[t] = h0
        a0_ref[t] = p0
        a1 = a1_ref[tb]
        h1 = a1 * h1 + u1_ref[tb]
        p1 = a1 * p1
        u1_ref[tb] = h1
        a1_ref[tb] = p1
        return h0, p0, h1, p1

    zero = jnp.zeros((SUBLANES, lanes), F32)
    one = jnp.ones((SUBLANES, lanes), F32)
    lax.fori_loop(0, t_len, scan_body, (zero, one, zero, one), unroll=8)

    e0, q0 = u0_ref[t_len - 1], a0_ref[t_len - 1]
    e1, q1 = u1_ref[0], a1_ref[0]
    c0 = zero
    c1 = zero
    for _ in range(N_SEG - 1):
        c0 = _shift_seg_down(e0 + q0 * c0)
        c1 = _shift_seg_up(e1 + q1 * c1)

    def out_body(i, carry):
        t0 = pl.multiple_of(i * GATE_CHUNK, GATE_CHUNK)
        sl = pl.ds(t0, GATE_CHUNK)
        h = (u0_ref[sl] + a0_ref[sl] * c0[None]) + (u1_ref[sl] + a1_ref[sl] * c1[None])
        y_ref[0, sl] = h * _gelu_tanh(xg_ref[0, sl])
        return carry

    lax.fori_loop(0, n_chunks, out_body, 0)


def _lru(xr_il, xg_il, conv_w, conv_b, wg, bg, lam):
    b, t_len, n_seg, d = xr_il.shape
    n_groups = d // LANES
    io_spec = pl.BlockSpec((1, t_len, n_seg, LANES), lambda i, j: (i, 0, 0, j))
    slab = pltpu.VMEM((t_len, n_seg, LANES), F32)
    return pl.pallas_call(
        _lru_kernel,
        grid=(b, n_groups),
        in_specs=[
            io_spec,
            io_spec,
            pl.BlockSpec((CONV_W, LANES), lambda i, j: (0, j)),
            pl.BlockSpec((1, LANES), lambda i, j: (0, j)),
            pl.BlockSpec((1, LANES, 4 * LANES), lambda i, j: (j, 0, 0)),
            pl.BlockSpec((1, 1, 4 * LANES), lambda i, j: (j, 0, 0)),
            pl.BlockSpec((1, 2, LANES), lambda i, j: (j, 0, 0)),
        ],
        out_specs=io_spec,
        out_shape=jax.ShapeDtypeStruct(xr_il.shape, F32),
        scratch_shapes=[pltpu.VMEM((t_len + CONV_W, n_seg, LANES), F32), slab, slab, slab, slab],
        compiler_params=pltpu.CompilerParams(
            dimension_semantics=("arbitrary", "arbitrary"), vmem_limit_bytes=VMEM_LIMIT),
        name="rglru",
    )(xr_il, xg_il, conv_w, conv_b, wg, bg, lam)


def _tail_kernel(x_ref, ya_ref, yl_ref, p_ref, glru_ref, wout_ref, gmlp_ref, wup_ref, wdown_ref,
                 gple_ref, wgate_ref, wproj_ref, gfin_ref, o_ref):
    x = x_ref[0]
    yl = yl_ref[0]
    yl_n = (yl * _rms(yl, -1) * glru_ref[...]).astype(BF16)
    y = jnp.concatenate([ya_ref[0], yl_n], axis=-1)
    h = x + _dot(y, wout_ref[...])
    hn = (h * _rms(h, -1) * gmlp_ref[...]).astype(BF16)
    m = _dot(hn, wup_ref[...])
    act = jnp.square(jnp.maximum(m, 0.0)).astype(BF16)
    h = h + _dot(act, wdown_ref[...])
    hn = (h * _rms(h, -1) * gple_ref[...]).astype(BF16)
    gate = _sigmoid(_dot(hn, wgate_ref[...]))
    h = h + gate * _dot(p_ref[0].astype(BF16), wproj_ref[...])
    o_ref[0] = h * _rms(h, -1) * gfin_ref[...]


def _tail(x, ya, yl_il, p, glru, wout, gmlp, wup, wdown, gple, wgate, wproj, gfin):
    b, s, d = x.shape
    t = s // N_SEG
    const = lambda *_: (0, 0)

    def resident(arr):
        return pl.BlockSpec(arr.shape, const, pipeline_mode=pl.Buffered(1))

    return pl.pallas_call(
        _tail_kernel,
        grid=(b, N_SEG),
        in_specs=[
            pl.BlockSpec((1, t, d), lambda i, j: (i, j, 0)),
            pl.BlockSpec((1, t, D_ATTN), lambda i, j: (i, j, 0)),
            pl.BlockSpec((1, t, D_LRU), lambda i, j: (i, 0, j)),
            pl.BlockSpec((1, t, D_PLE), lambda i, j: (i, j, 0)),
            resident(glru), resident(wout), resident(gmlp), resident(wup), resident(wdown),
            resident(gple), resident(wgate), resident(wproj), resident(gfin),
        ],
        out_specs=pl.BlockSpec((1, t, d), lambda i, j: (i, j, 0)),
        out_shape=jax.ShapeDtypeStruct((b, s, d), F32),
        compiler_params=pltpu.CompilerParams(
            dimension_semantics=("arbitrary", "arbitrary"), vmem_limit_bytes=VMEM_LIMIT),
        name="tail",
    )(x, ya, yl_il, p, glru, wout, gmlp, wup, wdown, gple, wgate, wproj, gfin)


def _rope_tables_t(seq_len):
    pos = jnp.arange(seq_len)
    row = (pos // GRID_W).astype(F32)
    col = (pos % GRID_W).astype(F32)
    inv_freq = ROPE_THETA ** (-jnp.arange(N_FREQ, dtype=F32) / N_FREQ)
    ang_r = row[None, :] * inv_freq[:, None]
    ang_c = col[None, :] * inv_freq[:, None]
    cos_t = jnp.concatenate([jnp.cos(ang_r)] * 2 + [jnp.cos(ang_c)] * 2, axis=0)
    sin_t = jnp.concatenate(
        [-jnp.sin(ang_r), jnp.sin(ang_r), -jnp.sin(ang_c), jnp.sin(ang_c)], axis=0)
    return cos_t, sin_t


def _block_diag_pairs(w):
    n_dir, n_blk, bw, _ = w.shape
    per = LANES // bw
    w5 = w.reshape(n_dir, n_blk // per, per, bw, bw)
    eye = jnp.eye(per, dtype=w.dtype)
    bd = jnp.einsum("egikj,im->egikmj", w5, eye)
    return bd.reshape(n_dir, n_blk // per, LANES, LANES)


def kernel(x, p, mix_norm, w_in, q_norm, k_norm, conv_w, conv_b, lru_wa, lru_ba, lru_wx, lru_bx,
           lru_lambda, attn_out_norm, lru_out_norm, w_out, mlp_norm, w_up, w_down, ple_norm,
           w_ple_gate, w_ple_proj, final_norm):
    b, s, d = x.shape
    assert w_in.shape[0] == 1, "single-layer trunk: the final norm is fused into the layer tail"
    t_len = s // N_SEG
    n_groups = D_LRU // LANES
    cos_t, sin_t = _rope_tables_t(s)
    row2 = lambda v: v.reshape(1, -1)
    col2 = lambda v: v.reshape(-1, 1)
    h = x
    for l in range(1):
        wqkv_t = w_in[l][:, :D_ATTN + 2 * D_KV].T.astype(BF16)
        wrg = w_in[l][:, D_ATTN + 2 * D_KV:].astype(BF16)
        qt, k, vt, xr, xg = _in_proj(h, row2(mix_norm[l]), wqkv_t, wrg,
                                     col2(q_norm[l]), col2(k_norm[l]), cos_t, sin_t)
        ya = _attention(qt, k, vt, col2(attn_out_norm[l]))

        wa_bd = _block_diag_pairs(lru_wa[l])
        wx_bd = _block_diag_pairs(lru_wx[l])
        wg = jnp.concatenate([wa_bd[0], wa_bd[1], wx_bd[0], wx_bd[1]], axis=-1).astype(BF16)
        grp = lambda v: v.reshape(n_groups, 1, LANES)
        bg = jnp.concatenate([grp(lru_ba[l][0]), grp(lru_ba[l][1]),
                              grp(lru_bx[l][0]), grp(lru_bx[l][1])], axis=-1)
        lam = lru_lambda[l].reshape(2, n_groups, LANES).transpose(1, 0, 2)
        il = lambda v: v.reshape(b, t_len, N_SEG, D_LRU)
        yl = _lru(il(xr), il(xg), conv_w[l], row2(conv_b[l]), wg, bg, lam)
        yl = yl.reshape(b, t_len, N_SEG * D_LRU)

        h = _tail(h, ya, yl, p[l], row2(lru_out_norm[l]), w_out[l].astype(BF16),
                  row2(mlp_norm[l]), w_up[l].astype(BF16), w_down[l].astype(BF16),
                  row2(ple_norm[l]), w_ple_gate[l].astype(BF16), w_ple_proj[l].astype(BF16),
                  row2(final_norm))
    return h
```

```python
import functools

import jax
import jax.numpy as jnp
from jax import lax
from jax.experimental import pallas as pl
from jax.experimental.pallas import tpu as pltpu

D_MODEL = 1024
GRID_W = 64
HEAD_DIM = 64
D_ATTN = 512
N_Q_HEADS = 8
N_KV_HEADS = 2
Q_PER_KV = 4
D_KV = 128
ROPE_THETA = 10000.0
N_FREQ = 16
D_LRU = 512
LRU_BLOCK_W = 64
LRU_C = 8.0
CONV_W = 4
D_FF = 4096
D_PLE = 256
NORM_EPS = 1e-6

SUBLANES = 8
LANES = 128

N_SEG = SUBLANES
Q_TILE = 256
KEY_CHUNK = 256
GATE_CHUNK = 64
VMEM_LIMIT = 56 * 1024 * 1024
Q_SCALE = HEAD_DIM ** -0.5 * 1.4426950408889634

F32 = jnp.float32
BF16 = jnp.bfloat16


def _rms(x, axis):
    return lax.rsqrt(jnp.mean(x * x, axis=axis, keepdims=True) + NORM_EPS)


def _dot(a, b):
    return jnp.dot(a, b, preferred_element_type=F32)


def _dot_nt(a, b):
    return lax.dot_general(a, b, (((1,), (1,)), ((), ())), preferred_element_type=F32)


def _norm_rope_t(xt, gcol, cos_t, sin_t, n_heads):
    t = xt.shape[1]
    x3 = xt.reshape(n_heads, HEAD_DIM, t)
    xn = x3 * _rms(x3, 1) * gcol[None]
    x5 = xn.reshape(n_heads * 2, 2, N_FREQ, t)
    xs = jnp.concatenate([x5[:, 1:2], x5[:, 0:1]], axis=1).reshape(n_heads, HEAD_DIM, t)
    out = xn * cos_t[None] + xs * sin_t[None]
    return out.reshape(n_heads * HEAD_DIM, t)


def _in_proj_kernel(x_ref, gmix_ref, wqkv_t_ref, wrg_ref, gq_ref, gk_ref, cos_ref, sin_ref,
                    qt_ref, k_ref, vt_ref, xr_ref, xg_ref):
    x = x_ref[0]
    hn = (x * _rms(x, -1) * gmix_ref[...]).astype(BF16)
    zt = _dot_nt(wqkv_t_ref[...], hn)
    zr = _dot(hn, wrg_ref[...])
    cos_t = cos_ref[...]
    sin_t = sin_ref[...]
    qt = _norm_rope_t(zt[:D_ATTN], gq_ref[...] * Q_SCALE, cos_t, sin_t, N_Q_HEADS)
    kt = _norm_rope_t(zt[D_ATTN:D_ATTN + D_KV], gk_ref[...], cos_t, sin_t, N_KV_HEADS)
    qt_ref[0] = qt.astype(BF16)
    k_ref[0] = kt.T.astype(BF16)
    vt_ref[0] = zt[D_ATTN + D_KV:].astype(BF16)
    xr_ref[0] = zr[:, :D_LRU]
    xg_ref[0] = zr[:, D_LRU:]


def _in_proj(x, gmix, wqkv_t, wrg, gq, gk, cos_t, sin_t):
    b, s, d = x.shape
    t = s // N_SEG
    const = lambda *_: (0, 0)
    return pl.pallas_call(
        _in_proj_kernel,
        grid=(b, N_SEG),
        in_specs=[
            pl.BlockSpec((1, t, d), lambda i, j: (i, j, 0)),
            pl.BlockSpec((1, d), const),
            pl.BlockSpec(wqkv_t.shape, const),
            pl.BlockSpec(wrg.shape, const),
            pl.BlockSpec((HEAD_DIM, 1), const),
            pl.BlockSpec((HEAD_DIM, 1), const),
            pl.BlockSpec((HEAD_DIM, t), lambda i, j: (0, j)),
            pl.BlockSpec((HEAD_DIM, t), lambda i, j: (0, j)),
        ],
        out_specs=[
            pl.BlockSpec((1, D_ATTN, t), lambda i, j: (i, 0, j)),
            pl.BlockSpec((1, t, D_KV), lambda i, j: (i, j, 0)),
            pl.BlockSpec((1, D_KV, t), lambda i, j: (i, 0, j)),
            pl.BlockSpec((1, t, D_LRU), lambda i, j: (i, 0, j)),
            pl.BlockSpec((1, t, D_LRU), lambda i, j: (i, 0, j)),
        ],
        out_shape=[
            jax.ShapeDtypeStruct((b, D_ATTN, s), BF16),
            jax.ShapeDtypeStruct((b, s, D_KV), BF16),
            jax.ShapeDtypeStruct((b, D_KV, s), BF16),
            jax.ShapeDtypeStruct((b, t, N_SEG * D_LRU), F32),
            jax.ShapeDtypeStruct((b, t, N_SEG * D_LRU), F32),
        ],
        compiler_params=pltpu.CompilerParams(
            dimension_semantics=("arbitrary", "arbitrary"), vmem_limit_bytes=VMEM_LIMIT),
        name="in_proj",
    )(x, gmix, wqkv_t, wrg, gq, gk, cos_t, sin_t)


def _attn_kernel(qt_ref, k_ref, vt_ref, gcol_ref, y_ref, vta_ref):
    s = k_ref.shape[1]
    tq = qt_ref.shape[2]
    n_chunks = s // KEY_CHUNK

    @pl.when(pl.program_id(1) == 0)
    def _():
        vt = vt_ref[0]
        row = lax.broadcasted_iota(jnp.int32, vt.shape, 0)
        ones_row = jnp.where(row == HEAD_DIM, 1.0, 0.0).astype(BF16)
        swapped = jnp.concatenate([vt[HEAD_DIM:], vt[:HEAD_DIM]], axis=0)
        vta_ref[0] = jnp.where(row < HEAD_DIM, vt, ones_row)
        vta_ref[1] = jnp.where(row < HEAD_DIM, swapped, ones_row)

    qt = qt_ref[0]
    zeros = jnp.zeros((HEAD_DIM, tq), BF16)
    outs = []
    for kv in range(N_KV_HEADS):
        cols = []
        for j in range(Q_PER_KV):
            h = kv * Q_PER_KV + j
            qh = qt[h * HEAD_DIM:(h + 1) * HEAD_DIM]
            cols.append(jnp.concatenate([qh, zeros] if kv == 0 else [zeros, qh], axis=0))
        qst = jnp.concatenate(cols, axis=1)
        m = None
        acc = None
        scores = lambda c: _dot(k_ref[0, c * KEY_CHUNK:(c + 1) * KEY_CHUNK, :], qst)
        sc_next = scores(0)
        for c in range(n_chunks):
            sc = sc_next
            if c + 1 < n_chunks:
                sc_next = scores(c + 1)
            mc = jnp.max(sc, axis=0, keepdims=True)
            m_new = mc if m is None else jnp.maximum(m, mc)
            p = jnp.exp2(sc - m_new).astype(BF16)
            pv = _dot(vta_ref[kv, :, c * KEY_CHUNK:(c + 1) * KEY_CHUNK], p)
            acc = pv if acc is None else acc * jnp.exp2(m - m_new) + pv
            m = m_new
        o = acc[:HEAD_DIM] / acc[HEAD_DIM:HEAD_DIM + 1]
        for j in range(Q_PER_KV):
            outs.append(o[:, j * tq:(j + 1) * tq])
    ot = jnp.concatenate(outs, axis=0)
    yt = ot * _rms(ot, 0) * gcol_ref[...]
    y_ref[0] = yt.T.astype(BF16)


def _attention(qt, k, vt, gcol):
    b, _, s = qt.shape
    return pl.pallas_call(
        _attn_kernel,
        grid=(b, s // Q_TILE),
        in_specs=[
            pl.BlockSpec((1, D_ATTN, Q_TILE), lambda i, j: (i, 0, j)),
            pl.BlockSpec((1, s, D_KV), lambda i, j: (i, 0, 0)),
            pl.BlockSpec((1, D_KV, s), lambda i, j: (i, 0, 0)),
            pl.BlockSpec((D_ATTN, 1), lambda i, j: (0, 0)),
        ],
        out_specs=pl.BlockSpec((1, Q_TILE, D_ATTN), lambda i, j: (i, j, 0)),
        out_shape=jax.ShapeDtypeStruct((b, s, D_ATTN), BF16),
        scratch_shapes=[pltpu.VMEM((N_KV_HEADS, D_KV, s), BF16)],
        compiler_params=pltpu.CompilerParams(
            dimension_semantics=("arbitrary", "arbitrary"), vmem_limit_bytes=VMEM_LIMIT),
        name="attention",
    )(qt, k, vt, gcol)


def _shift_seg_down(x):
    row = lax.broadcasted_iota(jnp.int32, x.shape, 0)
    return jnp.where(row == 0, 0.0, pltpu.roll(x, 1, 0))


def _shift_seg_up(x):
    row = lax.broadcasted_iota(jnp.int32, x.shape, 0)
    return jnp.where(row == SUBLANES - 1, 0.0, pltpu.roll(x, SUBLANES - 1, 0))


def _sigmoid(x):
    return 0.5 * jnp.tanh(0.5 * x) + 0.5


def _gelu_tanh(x):
    return 0.5 * x * (1.0 + jnp.tanh(0.7978845608028654 * (x + 0.044715 * (x * x * x))))


def _lru_kernel(xr_ref, xg_ref, cw_ref, cb_ref, wg_ref, bg_ref, lam_ref, y_ref,
                xp_ref, a0_ref, u0_ref, a1_ref, u1_ref):
    t_len = xr_ref.shape[1]
    lanes = xr_ref.shape[3]
    n_chunks = t_len // GATE_CHUNK

    xp_ref[0] = _shift_seg_down(xr_ref[0, t_len - 2])
    xp_ref[1] = _shift_seg_down(xr_ref[0, t_len - 1])
    xp_ref[t_len + 2] = _shift_seg_up(xr_ref[0, 0])

    def copy_body(i, carry):
        t0 = pl.multiple_of(i * GATE_CHUNK, GATE_CHUNK)
        xp_ref[pl.ds(t0 + 2, GATE_CHUNK)] = xr_ref[0, pl.ds(t0, GATE_CHUNK)]
        return carry

    lax.fori_loop(0, n_chunks, copy_body, 0)

    cw = cw_ref[...]
    cb = cb_ref[...]
    lam = lam_ref[0]
    cl = LRU_C * (jnp.minimum(lam, 0.0) - jnp.log1p(jnp.exp(-jnp.abs(lam))))
    wg = wg_ref[0]
    bg = bg_ref[0]

    def gate_body(i, carry):
        t0 = pl.multiple_of(i * GATE_CHUNK, GATE_CHUNK)
        xc = cb[None]
        for kk in range(CONV_W):
            xc = xc + xp_ref[pl.ds(t0 + kk, GATE_CHUNK)] * cw[kk:kk + 1][None]
        xc2 = xc.reshape(GATE_CHUNK * SUBLANES, lanes)
        g = _dot(xc2.astype(BF16), wg) + bg
        for e, (a_ref, u_ref) in enumerate(((a0_ref, u0_ref), (a1_ref, u1_ref))):
            r = _sigmoid(g[:, e * lanes:(e + 1) * lanes])
            gi = _sigmoid(g[:, (2 + e) * lanes:(3 + e) * lanes])
            log_a = r * cl[e:e + 1]
            a = jnp.exp(log_a)
            u = jnp.sqrt(1.0 - a * a) * (gi * xc2)
            a_ref[pl.ds(t0, GATE_CHUNK)] = a.reshape(GATE_CHUNK, SUBLANES, lanes)
            u_ref[pl.ds(t0, GATE_CHUNK)] = u.reshape(GATE_CHUNK, SUBLANES, lanes)
        return carry

    lax.fori_loop(0, n_chunks, gate_body, 0)

    def scan_body(t, carry):
        h0, p0, h1, p1 = carry
        tb = t_len - 1 - t
        a0 = a0_ref[t]
        h0 = a0 * h0 + u0_ref[t]
        p0 = a0 * p0
        u0_ref[t] = h0
        a0_ref[t] = p0
        a1 = a1_ref[tb]
        h1 = a1 * h1 + u1_ref[tb]
        p1 = a1 * p1
        u1_ref[tb] = h1
        a1_ref[tb] = p1
        return h0, p0, h1, p1

    zero = jnp.zeros((SUBLANES, lanes), F32)
    one = jnp.ones((SUBLANES, lanes), F32)
    lax.fori_loop(0, t_len, scan_body, (zero, one, zero, one), unroll=8)

    e0, q0 = u0_ref[t_len - 1], a0_ref[t_len - 1]
    e1, q1 = u1_ref[0], a1_ref[0]
    c0 = zero
    c1 = zero
    for _ in range(N_SEG - 1):
        c0 = _shift_seg_down(e0 + q0 * c0)
        c1 = _shift_seg_up(e1 + q1 * c1)

    def out_body(i, carry):
        t0 = pl.multiple_of(i * GATE_CHUNK, GATE_CHUNK)
        sl = pl.ds(t0, GATE_CHUNK)
        h = (u0_ref[sl] + a0_ref[sl] * c0[None]) + (u1_ref[sl] + a1_ref[sl] * c1[None])
        y_ref[0, sl] = h * _gelu_tanh(xg_ref[0, sl])
        return carry

    lax.fori_loop(0, n_chunks, out_body, 0)


def _lru(xr_il, xg_il, conv_w, conv_b, wg, bg, lam):
    b, t_len, n_seg, d = xr_il.shape
    n_groups = d // LANES
    io_spec = pl.BlockSpec((1, t_len, n_seg, LANES), lambda i, j: (i, 0, 0, j))
    slab = pltpu.VMEM((t_len, n_seg, LANES), F32)
    return pl.pallas_call(
        _lru_kernel,
        grid=(b, n_groups),
        in_specs=[
            io_spec,
            io_spec,
            pl.BlockSpec((CONV_W, LANES), lambda i, j: (0, j)),
            pl.BlockSpec((1, LANES), lambda i, j: (0, j)),
            pl.BlockSpec((1, LANES, 4 * LANES), lambda i, j: (j, 0, 0)),
            pl.BlockSpec((1, 1, 4 * LANES), lambda i, j: (j, 0, 0)),
            pl.BlockSpec((1, 2, LANES), lambda i, j: (j, 0, 0)),
        ],
        out_specs=io_spec,
        out_shape=jax.ShapeDtypeStruct(xr_il.shape, F32),
        scratch_shapes=[pltpu.VMEM((t_len + CONV_W, n_seg, LANES), F32), slab, slab, slab, slab],
        compiler_params=pltpu.CompilerParams(
            dimension_semantics=("arbitrary", "arbitrary"), vmem_limit_bytes=VMEM_LIMIT),
        name="rglru",
    )(xr_il, xg_il, conv_w, conv_b, wg, bg, lam)


def _tail_kernel(x_ref, ya_ref, yl_ref, p_ref, glru_ref, wout_ref, gmlp_ref, wup_ref, wdown_ref,
                 gple_ref, wgate_ref, wproj_ref, gfin_ref, o_ref):
    x = x_ref[0]
    yl = yl_ref[0]
    yl_n = (yl * _rms(yl, -1) * glru_ref[...]).astype(BF16)
    y = jnp.concatenate([ya_ref[0], yl_n], axis=-1)
    h = x + _dot(y, wout_ref[...])
    hn = (h * _rms(h, -1) * gmlp_ref[...]).astype(BF16)
    m = _dot(hn, wup_ref[...])
    act = jnp.square(jnp.maximum(m, 0.0)).astype(BF16)
    h = h + _dot(act, wdown_ref[...])
    hn = (h * _rms(h, -1) * gple_ref[...]).astype(BF16)
    gate = _sigmoid(_dot(hn, wgate_ref[...]))
    h = h + gate * _dot(p_ref[0].astype(BF16), wproj_ref[...])
    o_ref[0] = h * _rms(h, -1) * gfin_ref[...]


def _tail(x, ya, yl_il, p, glru, wout, gmlp, wup, wdown, gple, wgate, wproj, gfin):
    b, s, d = x.shape
    t = s // N_SEG
    const = lambda *_: (0, 0)

    def resident(arr):
        return pl.BlockSpec(arr.shape, const, pipeline_mode=pl.Buffered(1))

    return pl.pallas_call(
        _tail_kernel,
        grid=(b, N_SEG),
        in_specs=[
            pl.BlockSpec((1, t, d), lambda i, j: (i, j, 0)),
            pl.BlockSpec((1, t, D_ATTN), lambda i, j: (i, j, 0)),
            pl.BlockSpec((1, t, D_LRU), lambda i, j: (i, 0, j)),
            pl.BlockSpec((1, t, D_PLE), lambda i, j: (i, j, 0)),
            resident(glru), resident(wout), resident(gmlp), resident(wup), resident(wdown),
            resident(gple), resident(wgate), resident(wproj), resident(gfin),
        ],
        out_specs=pl.BlockSpec((1, t, d), lambda i, j: (i, j, 0)),
        out_shape=jax.ShapeDtypeStruct((b, s, d), F32),
        compiler_params=pltpu.CompilerParams(
            dimension_semantics=("arbitrary", "arbitrary"), vmem_limit_bytes=VMEM_LIMIT),
        name="tail",
    )(x, ya, yl_il, p, glru, wout, gmlp, wup, wdown, gple, wgate, wproj, gfin)


def _rope_tables_t(seq_len):
    pos = jnp.arange(seq_len)
    row = (pos // GRID_W).astype(F32)
    col = (pos % GRID_W).astype(F32)
    inv_freq = ROPE_THETA ** (-jnp.arange(N_FREQ, dtype=F32) / N_FREQ)
    ang_r = row[None, :] * inv_freq[:, None]
    ang_c = col[None, :] * inv_freq[:, None]
    cos_t = jnp.concatenate([jnp.cos(ang_r)] * 2 + [jnp.cos(ang_c)] * 2, axis=0)
    sin_t = jnp.concatenate(
        [-jnp.sin(ang_r), jnp.sin(ang_r), -jnp.sin(ang_c), jnp.sin(ang_c)], axis=0)
    return cos_t, sin_t


def _block_diag_pairs(w):
    n_dir, n_blk, bw, _ = w.shape
    per = LANES // bw
    w5 = w.reshape(n_dir, n_blk // per, per, bw, bw)
    eye = jnp.eye(per, dtype=w.dtype)
    bd = jnp.einsum("egikj,im->egikmj", w5, eye)
    return bd.reshape(n_dir, n_blk // per, LANES, LANES)


def kernel(x, p, mix_norm, w_in, q_norm, k_norm, conv_w, conv_b, lru_wa, lru_ba, lru_wx, lru_bx,
           lru_lambda, attn_out_norm, lru_out_norm, w_out, mlp_norm, w_up, w_down, ple_norm,
           w_ple_gate, w_ple_proj, final_norm):
    b, s, d = x.shape
    assert w_in.shape[0] == 1, "single-layer trunk: the final norm is fused into the layer tail"
    t_len = s // N_SEG
    n_groups = D_LRU // LANES
    cos_t, sin_t = _rope_tables_t(s)
    row2 = lambda v: v.reshape(1, -1)
    col2 = lambda v: v.reshape(-1, 1)
    h = x
    for l in range(1):
        wqkv_t = w_in[l][:, :D_ATTN + 2 * D_KV].T.astype(BF16)
        wrg = w_in[l][:, D_ATTN + 2 * D_KV:].astype(BF16)
        qt, k, vt, xr, xg = _in_proj(h, row2(mix_norm[l]), wqkv_t, wrg,
                                     col2(q_norm[l]), col2(k_norm[l]), cos_t, sin_t)
        ya = _attention(qt, k, vt, col2(attn_out_norm[l]))

        wa_bd = _block_diag_pairs(lru_wa[l])
        wx_bd = _block_diag_pairs(lru_wx[l])
        wg = jnp.concatenate([wa_bd[0], wa_bd[1], wx_bd[0], wx_bd[1]], axis=-1).astype(BF16)
        grp = lambda v: v.reshape(n_groups, 1, LANES)
        bg = jnp.concatenate([grp(lru_ba[l][0]), grp(lru_ba[l][1]),
                              grp(lru_bx[l][0]), grp(lru_bx[l][1])], axis=-1)
        lam = lru_lambda[l].reshape(2, n_groups, LANES).transpose(1, 0, 2)
        il = lambda v: v.reshape(b, t_len, N_SEG, D_LRU)
        yl = _lru(il(xr), il(xg), conv_w[l], row2(conv_b[l]), wg, bg, lam)
        yl = yl.reshape(b, t_len, N_SEG * D_LRU)

        h = _tail(h, ya, yl, p[l], row2(lru_out_norm[l]), w_out[l].astype(BF16),
                  row2(mlp_norm[l]), w_up[l].astype(BF16), w_down[l].astype(BF16),
                  row2(ple_norm[l]), w_ple_gate[l].astype(BF16), w_ple_proj[l].astype(BF16),
                  row2(final_norm))
    return h
```

```python
import jax
import jax.numpy as jnp
from jax import lax
from jax.experimental import pallas as pl
from jax.experimental.pallas import tpu as pltpu

D_MODEL = 1024
GRID_W = 64
HEAD_DIM = 64
D_ATTN = 512
N_Q_HEADS = 8
N_KV_HEADS = 2
Q_PER_KV = 4
D_KV = 128
ROPE_THETA = 10000.0
N_FREQ = 16
D_LRU = 512
LRU_BLOCK_W = 64
LRU_C = 8.0
CONV_W = 4
D_FF = 4096
D_PLE = 256
NORM_EPS = 1e-6

SUBLANES = 8
LANES = 128

N_SEG = SUBLANES
Q_TILE = 256
KEY_CHUNK = 256
GATE_CHUNK = 64
VMEM_LIMIT = 56 * 1024 * 1024
Q_SCALE = HEAD_DIM ** -0.5 * 1.4426950408889634

F32 = jnp.float32
BF16 = jnp.bfloat16


def _rms(x, axis):
    return lax.rsqrt(jnp.mean(x * x, axis=axis, keepdims=True) + NORM_EPS)


def _dot(a, b):
    return jnp.dot(a, b, preferred_element_type=F32)


def _dot_nt(a, b):
    return lax.dot_general(a, b, (((1,), (1,)), ((), ())), preferred_element_type=F32)


def _norm_rope_t(xt, gcol, cos_t, sin_t, n_heads):
    t = xt.shape[1]
    x3 = xt.reshape(n_heads, HEAD_DIM, t)
    xn = x3 * _rms(x3, 1) * gcol[None]
    x5 = xn.reshape(n_heads * 2, 2, N_FREQ, t)
    xs = jnp.concatenate([x5[:, 1:2], x5[:, 0:1]], axis=1).reshape(n_heads, HEAD_DIM, t)
    out = xn * cos_t[None] + xs * sin_t[None]
    return out.reshape(n_heads * HEAD_DIM, t)


def _in_proj_kernel(x_ref, gmix_ref, wqkv_t_ref, wrg_ref, gq_ref, gk_ref, cos_ref, sin_ref,
                    qt_ref, k_ref, vt_ref, xr_ref, xg_ref):
    x = x_ref[0]
    hn = (x * _rms(x, -1) * gmix_ref[...]).astype(BF16)
    zt = _dot_nt(wqkv_t_ref[...], hn)
    zr = _dot(hn, wrg_ref[...])
    cos_t = cos_ref[...]
    sin_t = sin_ref[...]
    qt = _norm_rope_t(zt[:D_ATTN], gq_ref[...] * Q_SCALE, cos_t, sin_t, N_Q_HEADS)
    kt = _norm_rope_t(zt[D_ATTN:D_ATTN + D_KV], gk_ref[...], cos_t, sin_t, N_KV_HEADS)
    qt_ref[0] = qt.astype(BF16)
    k_ref[0] = kt.T.astype(BF16)
    vt_ref[0] = zt[D_ATTN + D_KV:].astype(BF16)
    xr_ref[0] = zr[:, :D_LRU]
    xg_ref[0] = zr[:, D_LRU:]


def _in_proj(x, gmix, wqkv_t, wrg, gq, gk, cos_t, sin_t):
    b, s, d = x.shape
    t = s // N_SEG
    const = lambda *_: (0, 0)
    return pl.pallas_call(
        _in_proj_kernel,
        grid=(b, N_SEG),
        in_specs=[
            pl.BlockSpec((1, t, d), lambda i, j: (i, j, 0)),
            pl.BlockSpec((1, d), const),
            pl.BlockSpec(wqkv_t.shape, const),
            pl.BlockSpec(wrg.shape, const),
            pl.BlockSpec((HEAD_DIM, 1), const),
            pl.BlockSpec((HEAD_DIM, 1), const),
            pl.BlockSpec((HEAD_DIM, t), lambda i, j: (0, j)),
            pl.BlockSpec((HEAD_DIM, t), lambda i, j: (0, j)),
        ],
        out_specs=[
            pl.BlockSpec((1, D_ATTN, t), lambda i, j: (i, 0, j)),
            pl.BlockSpec((1, t, D_KV), lambda i, j: (i, j, 0)),
            pl.BlockSpec((1, D_KV, t), lambda i, j: (i, 0, j)),
            pl.BlockSpec((1, t, D_LRU), lambda i, j: (i, 0, j)),
            pl.BlockSpec((1, t, D_LRU), lambda i, j: (i, 0, j)),
        ],
        out_shape=[
            jax.ShapeDtypeStruct((b, D_ATTN, s), BF16),
            jax.ShapeDtypeStruct((b, s, D_KV), BF16),
            jax.ShapeDtypeStruct((b, D_KV, s), BF16),
            jax.ShapeDtypeStruct((b, t, N_SEG * D_LRU), F32),
            jax.ShapeDtypeStruct((b, t, N_SEG * D_LRU), F32),
        ],
        compiler_params=pltpu.CompilerParams(
            dimension_semantics=("arbitrary", "arbitrary"), vmem_limit_bytes=VMEM_LIMIT),
        name="in_proj",
    )(x, gmix, wqkv_t, wrg, gq, gk, cos_t, sin_t)


def _attn_kernel(qt_ref, k_ref, vt_ref, gcol_ref, y_ref, vta_ref):
    s = k_ref.shape[1]
    tq = qt_ref.shape[2]
    n_chunks = s // KEY_CHUNK

    @pl.when(pl.program_id(1) == 0)
    def _():
        vt = vt_ref[0]
        row = lax.broadcasted_iota(jnp.int32, vt.shape, 0)
        ones_row = jnp.where(row == HEAD_DIM, 1.0, 0.0).astype(BF16)
        swapped = jnp.concatenate([vt[HEAD_DIM:], vt[:HEAD_DIM]], axis=0)
        vta_ref[0] = jnp.where(row < HEAD_DIM, vt, ones_row)
        vta_ref[1] = jnp.where(row < HEAD_DIM, swapped, ones_row)

    qt = qt_ref[0]
    zeros = jnp.zeros((HEAD_DIM, tq), BF16)
    outs = []
    for kv in range(N_KV_HEADS):
        cols = []
        for j in range(Q_PER_KV):
            h = kv * Q_PER_KV + j
            qh = qt[h * HEAD_DIM:(h + 1) * HEAD_DIM]
            cols.append(jnp.concatenate([qh, zeros] if kv == 0 else [zeros, qh], axis=0))
        qst = jnp.concatenate(cols, axis=1)
        m = None
        acc = None
        scores = lambda c: _dot(k_ref[0, c * KEY_CHUNK:(c + 1) * KEY_CHUNK, :], qst)
        sc_next = scores(0)
        for c in range(n_chunks):
            sc = sc_next
            if c + 1 < n_chunks:
                sc_next = scores(c + 1)
            mc = jnp.max(sc, axis=0, keepdims=True)
            m_new = mc if m is None else jnp.maximum(m, mc)
            p = jnp.exp2(sc - m_new).astype(BF16)
            pv = _dot(vta_ref[kv, :, c * KEY_CHUNK:(c + 1) * KEY_CHUNK], p)
            acc = pv if acc is None else acc * jnp.exp2(m - m_new) + pv
            m = m_new
        o = acc[:HEAD_DIM] / acc[HEAD_DIM:HEAD_DIM + 1]
        for j in range(Q_PER_KV):
            outs.append(o[:, j * tq:(j + 1) * tq])
    ot = jnp.concatenate(outs, axis=0)
    yt = ot * _rms(ot, 0) * gcol_ref[...]
    y_ref[0] = yt.T.astype(BF16)


def _attention(qt, k, vt, gcol):
    b, _, s = qt.shape
    return pl.pallas_call(
        _attn_kernel,
        grid=(b, s // Q_TILE),
        in_specs=[
            pl.BlockSpec((1, D_ATTN, Q_TILE), lambda i, j: (i, 0, j)),
            pl.BlockSpec((1, s, D_KV), lambda i, j: (i, 0, 0)),
            pl.BlockSpec((1, D_KV, s), lambda i, j: (i, 0, 0)),
            pl.BlockSpec((D_ATTN, 1), lambda i, j: (0, 0)),
        ],
        out_specs=pl.BlockSpec((1, Q_TILE, D_ATTN), lambda i, j: (i, j, 0)),
        out_shape=jax.ShapeDtypeStruct((b, s, D_ATTN), BF16),
        scratch_shapes=[pltpu.VMEM((N_KV_HEADS, D_KV, s), BF16)],
        compiler_params=pltpu.CompilerParams(
            dimension_semantics=("arbitrary", "arbitrary"), vmem_limit_bytes=VMEM_LIMIT),
        name="attention",
    )(qt, k, vt, gcol)


def _shift_seg_down(x):
    row = lax.broadcasted_iota(jnp.int32, x.shape, 0)
    return jnp.where(row == 0, 0.0, pltpu.roll(x, 1, 0))


def _shift_seg_up(x):
    row = lax.broadcasted_iota(jnp.int32, x.shape, 0)
    return jnp.where(row == SUBLANES - 1, 0.0, pltpu.roll(x, SUBLANES - 1, 0))


def _sigmoid(x):
    return 0.5 * jnp.tanh(0.5 * x) + 0.5


def _gelu_tanh(x):
    k = 0.7978845608028654
    hx = 0.5 * x
    t = jnp.tanh(x * (k + (k * 0.044715) * (x * x)))
    return hx * t + hx


def _lru_kernel(xr_ref, xg_ref, cw_ref, cb_ref, wg_ref, bg_ref, lam_ref, y_ref,
                xp_ref, a0_ref, u0_ref, a1_ref, u1_ref, h0_ref, p0_ref, h1_ref, p1_ref, ystage_ref):
    t_len = xr_ref.shape[1]
    lanes = xr_ref.shape[3]
    n_chunks = t_len // GATE_CHUNK

    xp_ref[0] = _shift_seg_down(xr_ref[0, t_len - 2])
    xp_ref[1] = _shift_seg_down(xr_ref[0, t_len - 1])
    xp_ref[t_len + 2] = _shift_seg_up(xr_ref[0, 0])

    def copy_body(i, carry):
        t0 = pl.multiple_of(i * GATE_CHUNK, GATE_CHUNK)
        xp_ref[pl.ds(t0 + 2, GATE_CHUNK)] = xr_ref[0, pl.ds(t0, GATE_CHUNK)]
        return carry

    lax.fori_loop(0, n_chunks, copy_body, 0)

    cw = cw_ref[...]
    cb = cb_ref[...]
    lam = lam_ref[0]
    hc = (0.5 * LRU_C * 1.4426950408889634) * (
        jnp.minimum(lam, 0.0) - jnp.log1p(jnp.exp(-jnp.abs(lam))))
    wg_half = wg_ref[0] * 0.5
    bg_half = bg_ref[0] * 0.5

    def gate_body(i, carry):
        t0 = pl.multiple_of(i * GATE_CHUNK, GATE_CHUNK)
        xc = cb[None]
        for kk in range(CONV_W):
            xc = xc + xp_ref[pl.ds(t0 + kk, GATE_CHUNK)] * cw[kk:kk + 1][None]
        xc2 = xc.reshape(GATE_CHUNK * SUBLANES, lanes)
        th = jnp.tanh(_dot(xc2.astype(BF16), wg_half) + bg_half)
        xh = 0.5 * xc2
        for e, (a_ref, u_ref) in enumerate(((a0_ref, u0_ref), (a1_ref, u1_ref))):
            tr = th[:, e * lanes:(e + 1) * lanes]
            ti = th[:, (2 + e) * lanes:(3 + e) * lanes]
            a = jnp.exp2(tr * hc[e:e + 1] + hc[e:e + 1])
            y = 1.0 - a * a
            mult = y * lax.rsqrt(jnp.maximum(y, 1e-30))
            u = (ti * xh + xh) * mult
            a_ref[pl.ds(t0, GATE_CHUNK)] = a.reshape(GATE_CHUNK, SUBLANES, lanes)
            u_ref[pl.ds(t0, GATE_CHUNK)] = u.reshape(GATE_CHUNK, SUBLANES, lanes)
        return carry

    lax.fori_loop(0, n_chunks, gate_body, 0)

    def scan_body(t, carry):
        h0, p0, h1, p1 = carry
        tb = t_len - 1 - t
        a0 = a0_ref[t]
        h0 = a0 * h0 + u0_ref[t]
        p0 = a0 * p0
        h0_ref[t] = h0
        p0_ref[t] = p0
        a1 = a1_ref[tb]
        h1 = a1 * h1 + u1_ref[tb]
        p1 = a1 * p1
        h1_ref[tb] = h1
        p1_ref[tb] = p1
        return h0, p0, h1, p1

    zero = jnp.zeros((SUBLANES, lanes), F32)
    one = jnp.ones((SUBLANES, lanes), F32)
    e0, q0, e1, q1 = lax.fori_loop(0, t_len, scan_body, (zero, one, zero, one), unroll=8)

    c0 = zero
    c1 = zero
    for _ in range(N_SEG - 1):
        c0 = _shift_seg_down(e0 + q0 * c0)
        c1 = _shift_seg_up(e1 + q1 * c1)

    def out_body(i, carry):
        t0 = pl.multiple_of(i * GATE_CHUNK, GATE_CHUNK)
        sl = pl.ds(t0, GATE_CHUNK)
        h = (h0_ref[sl] + p0_ref[sl] * c0[None]) + (h1_ref[sl] + p1_ref[sl] * c1[None])
        y = h * _gelu_tanh(xg_ref[0, sl])
        ystage_ref[...] = y.reshape(GATE_CHUNK * SUBLANES, lanes)
        for seg in range(N_SEG):
            y_ref[0, pl.ds(seg * t_len + t0, GATE_CHUNK)] = (
                ystage_ref[pl.ds(seg, GATE_CHUNK, stride=SUBLANES)])
        return carry

    lax.fori_loop(0, n_chunks, out_body, 0)


def _lru(xr_il, xg_il, conv_w, conv_b, wg, bg, lam):
    b, t_len, n_seg, d = xr_il.shape
    n_groups = d // LANES
    io_spec = pl.BlockSpec((1, t_len, n_seg, LANES), lambda i, j: (i, 0, 0, j))
    slab = pltpu.VMEM((t_len, n_seg, LANES), F32)
    return pl.pallas_call(
        _lru_kernel,
        grid=(b, n_groups),
        in_specs=[
            io_spec,
            io_spec,
            pl.BlockSpec((CONV_W, LANES), lambda i, j: (0, j)),
            pl.BlockSpec((1, LANES), lambda i, j: (0, j)),
            pl.BlockSpec((1, LANES, 4 * LANES), lambda i, j: (j, 0, 0)),
            pl.BlockSpec((1, 1, 4 * LANES), lambda i, j: (j, 0, 0)),
            pl.BlockSpec((1, 2, LANES), lambda i, j: (j, 0, 0)),
        ],
        out_specs=pl.BlockSpec((1, t_len * n_seg, LANES), lambda i, j: (i, 0, j)),
        out_shape=jax.ShapeDtypeStruct((b, t_len * n_seg, d), F32),
        scratch_shapes=[pltpu.VMEM((t_len + CONV_W, n_seg, LANES), F32)] + [slab] * 8 + [
            pltpu.VMEM((GATE_CHUNK * n_seg, LANES), F32)],
        compiler_params=pltpu.CompilerParams(
            dimension_semantics=("arbitrary", "arbitrary"), vmem_limit_bytes=VMEM_LIMIT),
        name="rglru",
    )(xr_il, xg_il, conv_w, conv_b, wg, bg, lam)


def _tail_kernel(x_ref, ya_ref, yl_ref, p_ref, glru_ref, wout_ref, gmlp_ref, wup_ref, wdown_ref,
                 gple_ref, wgate_ref, wproj_ref, gfin_ref, o_ref):
    x = x_ref[0]
    yl = yl_ref[0]
    yl_n = (yl * _rms(yl, -1) * glru_ref[...]).astype(BF16)
    y = jnp.concatenate([ya_ref[0], yl_n], axis=-1)
    h = x + _dot(y, wout_ref[...])
    hn = (h * _rms(h, -1) * gmlp_ref[...]).astype(BF16)
    m = _dot(hn, wup_ref[...])
    act = jnp.square(jnp.maximum(m, 0.0)).astype(BF16)
    h = h + _dot(act, wdown_ref[...])
    hn = (h * _rms(h, -1) * gple_ref[...]).astype(BF16)
    gate = _sigmoid(_dot(hn, wgate_ref[...]))
    h = h + gate * _dot(p_ref[0].astype(BF16), wproj_ref[...])
    o_ref[0] = h * _rms(h, -1) * gfin_ref[...]


def _tail(x, ya, yl_il, p, glru, wout, gmlp, wup, wdown, gple, wgate, wproj, gfin):
    b, s, d = x.shape
    t = s // N_SEG
    const = lambda *_: (0, 0)

    def resident(arr):
        return pl.BlockSpec(arr.shape, const, pipeline_mode=pl.Buffered(1))

    return pl.pallas_call(
        _tail_kernel,
        grid=(b, N_SEG),
        in_specs=[
            pl.BlockSpec((1, t, d), lambda i, j: (i, j, 0)),
            pl.BlockSpec((1, t, D_ATTN), lambda i, j: (i, j, 0)),
            pl.BlockSpec((1, t, D_LRU), lambda i, j: (i, j, 0)),
            pl.BlockSpec((1, t, D_PLE), lambda i, j: (i, j, 0)),
            resident(glru), resident(wout), resident(gmlp), resident(wup), resident(wdown),
            resident(gple), resident(wgate), resident(wproj), resident(gfin),
        ],
        out_specs=pl.BlockSpec((1, t, d), lambda i, j: (i, j, 0)),
        out_shape=jax.ShapeDtypeStruct((b, s, d), F32),
        compiler_params=pltpu.CompilerParams(
            dimension_semantics=("arbitrary", "arbitrary"), vmem_limit_bytes=VMEM_LIMIT),
        name="tail",
    )(x, ya, yl_il, p, glru, wout, gmlp, wup, wdown, gple, wgate, wproj, gfin)


def _rope_tables_t(seq_len):
    pos = jnp.arange(seq_len)
    row = (pos // GRID_W).astype(F32)
    col = (pos % GRID_W).astype(F32)
    inv_freq = ROPE_THETA ** (-jnp.arange(N_FREQ, dtype=F32) / N_FREQ)
    ang_r = row[None, :] * inv_freq[:, None]
    ang_c = col[None, :] * inv_freq[:, None]
    cos_t = jnp.concatenate([jnp.cos(ang_r)] * 2 + [jnp.cos(ang_c)] * 2, axis=0)
    sin_t = jnp.concatenate(
        [-jnp.sin(ang_r), jnp.sin(ang_r), -jnp.sin(ang_c), jnp.sin(ang_c)], axis=0)
    return cos_t, sin_t


def _block_diag_pairs(w):
    n_dir, n_blk, bw, _ = w.shape
    per = LANES // bw
    w5 = w.reshape(n_dir, n_blk // per, per, bw, bw)
    eye = jnp.eye(per, dtype=w.dtype)
    bd = jnp.einsum("egikj,im->egikmj", w5, eye)
    return bd.reshape(n_dir, n_blk // per, LANES, LANES)


def kernel(x, p, mix_norm, w_in, q_norm, k_norm, conv_w, conv_b, lru_wa, lru_ba, lru_wx, lru_bx,
           lru_lambda, attn_out_norm, lru_out_norm, w_out, mlp_norm, w_up, w_down, ple_norm,
           w_ple_gate, w_ple_proj, final_norm):
    b, s, d = x.shape
    assert w_in.shape[0] == 1, "single-layer trunk: the final norm is fused into the layer tail"
    t_len = s // N_SEG
    n_groups = D_LRU // LANES
    cos_t, sin_t = _rope_tables_t(s)
    row2 = lambda v: v.reshape(1, -1)
    col2 = lambda v: v.reshape(-1, 1)
    h = x
    for l in range(1):
        wqkv_t = w_in[l][:, :D_ATTN + 2 * D_KV].T.astype(BF16)
        wrg = w_in[l][:, D_ATTN + 2 * D_KV:].astype(BF16)
        qt, k, vt, xr, xg = _in_proj(h, row2(mix_norm[l]), wqkv_t, wrg,
                                     col2(q_norm[l]), col2(k_norm[l]), cos_t, sin_t)
        ya = _attention(qt, k, vt, col2(attn_out_norm[l]))

        wa_bd = _block_diag_pairs(lru_wa[l])
        wx_bd = _block_diag_pairs(lru_wx[l])
        wg = jnp.concatenate([wa_bd[0], wa_bd[1], wx_bd[0], wx_bd[1]], axis=-1).astype(BF16)
        grp = lambda v: v.reshape(n_groups, 1, LANES)
        bg = jnp.concatenate([grp(lru_ba[l][0]), grp(lru_ba[l][1]),
                              grp(lru_bx[l][0]), grp(lru_bx[l][1])], axis=-1)
        lam = lru_lambda[l].reshape(2, n_groups, LANES).transpose(1, 0, 2)
        il = lambda v: v.reshape(b, t_len, N_SEG, D_LRU)
        yl = _lru(il(xr), il(xg), conv_w[l], row2(conv_b[l]), wg, bg, lam)

        h = _tail(h, ya, yl, p[l], row2(lru_out_norm[l]), w_out[l].astype(BF16),
                  row2(mlp_norm[l]), w_up[l].astype(BF16), w_down[l].astype(BF16),
                  row2(ple_norm[l]), w_ple_gate[l].astype(BF16), w_ple_proj[l].astype(BF16),
                  row2(final_norm))
    return h
```

```python
import jax
import jax.numpy as jnp
from jax import lax
from jax.experimental import pallas as pl
from jax.experimental.pallas import tpu as pltpu

D_MODEL = 1024
GRID_W = 64
HEAD_DIM = 64
D_ATTN = 512
N_Q_HEADS = 8
N_KV_HEADS = 2
Q_PER_KV = 4
D_KV = 128
ROPE_THETA = 10000.0
N_FREQ = 16
D_LRU = 512
LRU_BLOCK_W = 64
LRU_C = 8.0
CONV_W = 4
D_FF = 4096
D_PLE = 256
NORM_EPS = 1e-6

SUBLANES = 8
LANES = 128

N_SEG = SUBLANES
Q_TILE = 256
KEY_CHUNK = 256
FAST_CHUNK = 512
PV_ROWS = 80
BOUND_SLACK = 1.001
L_MIN_OK = 2.0 ** -80
GATE_CHUNK = 64
VMEM_LIMIT = 56 * 1024 * 1024
Q_SCALE = HEAD_DIM ** -0.5 * 1.4426950408889634

F32 = jnp.float32
BF16 = jnp.bfloat16


def _rms(x, axis):
    return lax.rsqrt(jnp.mean(x * x, axis=axis, keepdims=True) + NORM_EPS)


def _dot(a, b):
    return jnp.dot(a, b, preferred_element_type=F32)


def _dot_nt(a, b):
    return lax.dot_general(a, b, (((1,), (1,)), ((), ())), preferred_element_type=F32)


def _norm_rope_t(xt, gcol, cos_t, sin_t, n_heads):
    t = xt.shape[1]
    x3 = xt.reshape(n_heads, HEAD_DIM, t)
    xn = x3 * _rms(x3, 1) * gcol[None]
    x5 = xn.reshape(n_heads * 2, 2, N_FREQ, t)
    xs = jnp.concatenate([x5[:, 1:2], x5[:, 0:1]], axis=1).reshape(n_heads, HEAD_DIM, t)
    out = xn * cos_t[None] + xs * sin_t[None]
    return out.reshape(n_heads * HEAD_DIM, t)


def _in_proj_kernel(x_ref, gmix_ref, wqkv_t_ref, wrg_ref, gq_ref, gk_ref, cos_ref, sin_ref,
                    qt_ref, k_ref, vt_ref, xr_ref, xg_ref):
    x = x_ref[0]
    hn = (x * _rms(x, -1) * gmix_ref[...]).astype(BF16)
    zt = _dot_nt(wqkv_t_ref[...], hn)
    zr = _dot(hn, wrg_ref[...])
    cos_t = cos_ref[...]
    sin_t = sin_ref[...]
    qt = _norm_rope_t(zt[:D_ATTN], gq_ref[...] * Q_SCALE, cos_t, sin_t, N_Q_HEADS)
    kt = _norm_rope_t(zt[D_ATTN:D_ATTN + D_KV], gk_ref[...], cos_t, sin_t, N_KV_HEADS)
    qt_ref[0] = qt.astype(BF16)
    k_ref[0] = kt.T.astype(BF16)
    vt_ref[0] = zt[D_ATTN + D_KV:].astype(BF16)
    xr_ref[0] = zr[:, :D_LRU]
    xg_ref[0] = zr[:, D_LRU:]


def _in_proj(x, gmix, wqkv_t, wrg, gq, gk, cos_t, sin_t):
    b, s, d = x.shape
    t = s // N_SEG
    const = lambda *_: (0, 0)
    return pl.pallas_call(
        _in_proj_kernel,
        grid=(b, N_SEG),
        in_specs=[
            pl.BlockSpec((1, t, d), lambda i, j: (i, j, 0)),
            pl.BlockSpec((1, d), const),
            pl.BlockSpec(wqkv_t.shape, const),
            pl.BlockSpec(wrg.shape, const),
            pl.BlockSpec((HEAD_DIM, 1), const),
            pl.BlockSpec((HEAD_DIM, 1), const),
            pl.BlockSpec((HEAD_DIM, t), lambda i, j: (0, j)),
            pl.BlockSpec((HEAD_DIM, t), lambda i, j: (0, j)),
        ],
        out_specs=[
            pl.BlockSpec((1, D_ATTN, t), lambda i, j: (i, 0, j)),
            pl.BlockSpec((1, t, D_KV), lambda i, j: (i, j, 0)),
            pl.BlockSpec((1, D_KV, t), lambda i, j: (i, 0, j)),
            pl.BlockSpec((1, t, D_LRU), lambda i, j: (i, 0, j)),
            pl.BlockSpec((1, t, D_LRU), lambda i, j: (i, 0, j)),
        ],
        out_shape=[
            jax.ShapeDtypeStruct((b, D_ATTN, s), BF16),
            jax.ShapeDtypeStruct((b, s, D_KV), BF16),
            jax.ShapeDtypeStruct((b, D_KV, s), BF16),
            jax.ShapeDtypeStruct((b, t, N_SEG * D_LRU), F32),
            jax.ShapeDtypeStruct((b, t, N_SEG * D_LRU), F32),
        ],
        compiler_params=pltpu.CompilerParams(
            dimension_semantics=("arbitrary", "arbitrary"), vmem_limit_bytes=VMEM_LIMIT),
        name="in_proj",
    )(x, gmix, wqkv_t, wrg, gq, gk, cos_t, sin_t)


def _attn_kernel(qt_ref, k_ref, vt_ref, gcol_ref, y_ref, vta_ref, kn_ref):
    s = k_ref.shape[1]
    tq = qt_ref.shape[2]

    @pl.when(pl.program_id(1) == 0)
    def _():
        vt = vt_ref[0]
        row = lax.broadcasted_iota(jnp.int32, (PV_ROWS - HEAD_DIM, s), 0)
        ones_pad = jnp.where(row == 0, 1.0, 0.0).astype(BF16)
        kf = k_ref[0].astype(F32)
        lane = lax.broadcasted_iota(jnp.int32, kf.shape, 1)
        ksq = kf * kf
        for kv in range(N_KV_HEADS):
            vta_ref[kv] = jnp.concatenate([vt[kv * HEAD_DIM:(kv + 1) * HEAD_DIM], ones_pad], axis=0)
            in_head = (lane >= kv * HEAD_DIM) & (lane < (kv + 1) * HEAD_DIM)
            n2 = jnp.sum(jnp.where(in_head, ksq, 0.0), axis=1, keepdims=True)
            kn_ref[kv] = jnp.broadcast_to(jnp.sqrt(jnp.max(n2, axis=0, keepdims=True)), (1, LANES))

    qt = qt_ref[0]
    zeros = jnp.zeros((HEAD_DIM, tq), BF16)

    def q_operand(kv):
        cols = []
        for j in range(Q_PER_KV):
            h = kv * Q_PER_KV + j
            qh = qt[h * HEAD_DIM:(h + 1) * HEAD_DIM]
            cols.append(jnp.concatenate([qh, zeros] if kv == 0 else [zeros, qh], axis=0))
        return jnp.concatenate(cols, axis=1)

    def finish(accs):
        outs = []
        for acc in accs:
            o = acc[:HEAD_DIM] / acc[HEAD_DIM:HEAD_DIM + 1]
            for j in range(Q_PER_KV):
                outs.append(o[:, j * tq:(j + 1) * tq])
        ot = jnp.concatenate(outs, axis=0)
        yt = ot * _rms(ot, 0) * gcol_ref[...]
        y_ref[0] = yt.T.astype(BF16)

    def scores_fn(qst, chunk):
        return lambda c: _dot(k_ref[0, c * chunk:(c + 1) * chunk, :], qst)

    qf = qt.astype(F32).reshape(N_Q_HEADS, HEAD_DIM, tq)
    qn = jnp.sqrt(jnp.sum(qf * qf, axis=1))
    accs = []
    for kv in range(N_KV_HEADS):
        bound = qn[kv * Q_PER_KV:(kv + 1) * Q_PER_KV] * (kn_ref[kv][:, :1] * BOUND_SLACK)
        m = jnp.concatenate([bound[j:j + 1] for j in range(Q_PER_KV)], axis=1)
        scores = scores_fn(q_operand(kv), FAST_CHUNK)
        n_chunks = s // FAST_CHUNK
        acc = None
        sc_next = scores(0)
        for c in range(n_chunks):
            sc = sc_next
            if c + 1 < n_chunks:
                sc_next = scores(c + 1)
            p = jnp.exp2(sc - m).astype(BF16)
            pv = _dot(vta_ref[kv, :, c * FAST_CHUNK:(c + 1) * FAST_CHUNK], p)
            acc = pv if acc is None else acc + pv
        accs.append(acc)
    finish(accs)
    lmin = jnp.min(jnp.minimum(accs[0][HEAD_DIM:HEAD_DIM + 1], accs[1][HEAD_DIM:HEAD_DIM + 1]))

    @pl.when(jnp.logical_not(lmin > L_MIN_OK))
    def _():
        accs = []
        for kv in range(N_KV_HEADS):
            scores = scores_fn(q_operand(kv), KEY_CHUNK)
            n_chunks = s // KEY_CHUNK
            m = None
            acc = None
            sc_next = scores(0)
            for c in range(n_chunks):
                sc = sc_next
                if c + 1 < n_chunks:
                    sc_next = scores(c + 1)
                mc = jnp.max(sc, axis=0, keepdims=True)
                m_new = mc if m is None else jnp.maximum(m, mc)
                p = jnp.exp2(sc - m_new).astype(BF16)
                pv = _dot(vta_ref[kv, :, c * KEY_CHUNK:(c + 1) * KEY_CHUNK], p)
                acc = pv if acc is None else acc * jnp.exp2(m - m_new) + pv
                m = m_new
            accs.append(acc)
        finish(accs)


def _attention(qt, k, vt, gcol):
    b, _, s = qt.shape
    return pl.pallas_call(
        _attn_kernel,
        grid=(b, s // Q_TILE),
        in_specs=[
            pl.BlockSpec((1, D_ATTN, Q_TILE), lambda i, j: (i, 0, j)),
            pl.BlockSpec((1, s, D_KV), lambda i, j: (i, 0, 0)),
            pl.BlockSpec((1, D_KV, s), lambda i, j: (i, 0, 0)),
            pl.BlockSpec((D_ATTN, 1), lambda i, j: (0, 0)),
        ],
        out_specs=pl.BlockSpec((1, Q_TILE, D_ATTN), lambda i, j: (i, j, 0)),
        out_shape=jax.ShapeDtypeStruct((b, s, D_ATTN), BF16),
        scratch_shapes=[pltpu.VMEM((N_KV_HEADS, PV_ROWS, s), BF16),
                        pltpu.VMEM((N_KV_HEADS, 1, LANES), F32)],
        compiler_params=pltpu.CompilerParams(
            dimension_semantics=("arbitrary", "arbitrary"), vmem_limit_bytes=VMEM_LIMIT),
        name="attention",
    )(qt, k, vt, gcol)


def _shift_seg_down(x):
    row = lax.broadcasted_iota(jnp.int32, x.shape, 0)
    return jnp.where(row == 0, 0.0, pltpu.roll(x, 1, 0))


def _shift_seg_up(x):
    row = lax.broadcasted_iota(jnp.int32, x.shape, 0)
    return jnp.where(row == SUBLANES - 1, 0.0, pltpu.roll(x, SUBLANES - 1, 0))


def _sigmoid(x):
    return 0.5 * jnp.tanh(0.5 * x) + 0.5


def _gelu_tanh(x):
    k = 0.7978845608028654
    hx = 0.5 * x
    t = jnp.tanh(x * (k + (k * 0.044715) * (x * x)))
    return hx * t + hx


def _lru_kernel(xr_ref, xg_ref, cw_ref, cb_ref, wg_ref, bg_ref, lam_ref, y_ref,
                xp_ref, a0_ref, u0_ref, a1_ref, u1_ref, h0_ref, p0_ref, h1_ref, p1_ref, ystage_ref):
    t_len = xr_ref.shape[1]
    lanes = xr_ref.shape[3]
    n_chunks = t_len // GATE_CHUNK

    xp_ref[0] = _shift_seg_down(xr_ref[0, t_len - 2])
    xp_ref[1] = _shift_seg_down(xr_ref[0, t_len - 1])
    xp_ref[t_len + 2] = _shift_seg_up(xr_ref[0, 0])

    def copy_body(i, carry):
        t0 = pl.multiple_of(i * GATE_CHUNK, GATE_CHUNK)
        xp_ref[pl.ds(t0 + 2, GATE_CHUNK)] = xr_ref[0, pl.ds(t0, GATE_CHUNK)]
        return carry

    lax.fori_loop(0, n_chunks, copy_body, 0)

    cw = cw_ref[...]
    cb = cb_ref[...]
    lam = lam_ref[0]
    hc = (0.5 * LRU_C * 1.4426950408889634) * (
        jnp.minimum(lam, 0.0) - jnp.log1p(jnp.exp(-jnp.abs(lam))))
    wg_half = wg_ref[0] * 0.5
    bg_half = bg_ref[0] * 0.5

    def gate_body(i, carry):
        t0 = pl.multiple_of(i * GATE_CHUNK, GATE_CHUNK)
        xc = cb[None]
        for kk in range(CONV_W):
            xc = xc + xp_ref[pl.ds(t0 + kk, GATE_CHUNK)] * cw[kk:kk + 1][None]
        xc2 = xc.reshape(GATE_CHUNK * SUBLANES, lanes)
        th = jnp.tanh(_dot(xc2.astype(BF16), wg_half) + bg_half)
        xh = 0.5 * xc2
        for e, (a_ref, u_ref) in enumerate(((a0_ref, u0_ref), (a1_ref, u1_ref))):
            tr = th[:, e * lanes:(e + 1) * lanes]
            ti = th[:, (2 + e) * lanes:(3 + e) * lanes]
            a = jnp.exp2(tr * hc[e:e + 1] + hc[e:e + 1])
            y = 1.0 - a * a
            mult = y * lax.rsqrt(jnp.maximum(y, 1e-30))
            u = (ti * xh + xh) * mult
            a_ref[pl.ds(t0, GATE_CHUNK)] = a.reshape(GATE_CHUNK, SUBLANES, lanes)
            u_ref[pl.ds(t0, GATE_CHUNK)] = u.reshape(GATE_CHUNK, SUBLANES, lanes)
        return carry

    lax.fori_loop(0, n_chunks, gate_body, 0)

    def scan_body(t, carry):
        h0, p0, h1, p1 = carry
        tb = t_len - 1 - t
        a0 = a0_ref[t]
        h0 = a0 * h0 + u0_ref[t]
        p0 = a0 * p0
        h0_ref[t] = h0
        p0_ref[t] = p0
        a1 = a1_ref[tb]
        h1 = a1 * h1 + u1_ref[tb]
        p1 = a1 * p1
        h1_ref[tb] = h1
        p1_ref[tb] = p1
        return h0, p0, h1, p1

    zero = jnp.zeros((SUBLANES, lanes), F32)
    one = jnp.ones((SUBLANES, lanes), F32)
    e0, q0, e1, q1 = lax.fori_loop(0, t_len, scan_body, (zero, one, zero, one), unroll=8)

    c0 = zero
    c1 = zero
    for _ in range(N_SEG - 1):
        c0 = _shift_seg_down(e0 + q0 * c0)
        c1 = _shift_seg_up(e1 + q1 * c1)

    def out_body(i, carry):
        t0 = pl.multiple_of(i * GATE_CHUNK, GATE_CHUNK)
        sl = pl.ds(t0, GATE_CHUNK)
        h = (h0_ref[sl] + p0_ref[sl] * c0[None]) + (h1_ref[sl] + p1_ref[sl] * c1[None])
        y = h * _gelu_tanh(xg_ref[0, sl])
        ystage_ref[...] = y.reshape(GATE_CHUNK * SUBLANES, lanes)
        for seg in range(N_SEG):
            y_ref[0, pl.ds(seg * t_len + t0, GATE_CHUNK)] = (
                ystage_ref[pl.ds(seg, GATE_CHUNK, stride=SUBLANES)])
        return carry

    lax.fori_loop(0, n_chunks, out_body, 0)


def _lru(xr_il, xg_il, conv_w, conv_b, wg, bg, lam):
    b, t_len, n_seg, d = xr_il.shape
    n_groups = d // LANES
    io_spec = pl.BlockSpec((1, t_len, n_seg, LANES), lambda i, j: (i, 0, 0, j))
    slab = pltpu.VMEM((t_len, n_seg, LANES), F32)
    return pl.pallas_call(
        _lru_kernel,
        grid=(b, n_groups),
        in_specs=[
            io_spec,
            io_spec,
            pl.BlockSpec((CONV_W, LANES), lambda i, j: (0, j)),
            pl.BlockSpec((1, LANES), lambda i, j: (0, j)),
            pl.BlockSpec((1, LANES, 4 * LANES), lambda i, j: (j, 0, 0)),
            pl.BlockSpec((1, 1, 4 * LANES), lambda i, j: (j, 0, 0)),
            pl.BlockSpec((1, 2, LANES), lambda i, j: (j, 0, 0)),
        ],
        out_specs=pl.BlockSpec((1, t_len * n_seg, LANES), lambda i, j: (i, 0, j)),
        out_shape=jax.ShapeDtypeStruct((b, t_len * n_seg, d), F32),
        scratch_shapes=[pltpu.VMEM((t_len + CONV_W, n_seg, LANES), F32)] + [slab] * 8 + [
            pltpu.VMEM((GATE_CHUNK * n_seg, LANES), F32)],
        compiler_params=pltpu.CompilerParams(
            dimension_semantics=("arbitrary", "arbitrary"), vmem_limit_bytes=VMEM_LIMIT),
        name="rglru",
    )(xr_il, xg_il, conv_w, conv_b, wg, bg, lam)


def _tail_kernel(x_ref, ya_ref, yl_ref, p_ref, glru_ref, wout_ref, gmlp_ref, wup_ref, wdown_ref,
                 gple_ref, wgate_ref, wproj_ref, gfin_ref, o_ref):
    x = x_ref[0]
    yl = yl_ref[0]
    yl_n = (yl * _rms(yl, -1) * glru_ref[...]).astype(BF16)
    y = jnp.concatenate([ya_ref[0], yl_n], axis=-1)
    h = x + _dot(y, wout_ref[...])
    hn = (h * _rms(h, -1) * gmlp_ref[...]).astype(BF16)
    m = _dot(hn, wup_ref[...])
    act = jnp.square(jnp.maximum(m, 0.0)).astype(BF16)
    h = h + _dot(act, wdown_ref[...])
    hn = (h * _rms(h, -1) * gple_ref[...]).astype(BF16)
    gate = _sigmoid(_dot(hn, wgate_ref[...]))
    h = h + gate * _dot(p_ref[0].astype(BF16), wproj_ref[...])
    o_ref[0] = h * _rms(h, -1) * gfin_ref[...]


def _tail(x, ya, yl_il, p, glru, wout, gmlp, wup, wdown, gple, wgate, wproj, gfin):
    b, s, d = x.shape
    t = s // N_SEG
    const = lambda *_: (0, 0)

    def resident(arr):
        return pl.BlockSpec(arr.shape, const, pipeline_mode=pl.Buffered(1))

    return pl.pallas_call(
        _tail_kernel,
        grid=(b, N_SEG),
        in_specs=[
            pl.BlockSpec((1, t, d), lambda i, j: (i, j, 0)),
            pl.BlockSpec((1, t, D_ATTN), lambda i, j: (i, j, 0)),
            pl.BlockSpec((1, t, D_LRU), lambda i, j: (i, j, 0)),
            pl.BlockSpec((1, t, D_PLE), lambda i, j: (i, j, 0)),
            resident(glru), resident(wout), resident(gmlp), resident(wup), resident(wdown),
            resident(gple), resident(wgate), resident(wproj), resident(gfin),
        ],
        out_specs=pl.BlockSpec((1, t, d), lambda i, j: (i, j, 0)),
        out_shape=jax.ShapeDtypeStruct((b, s, d), F32),
        compiler_params=pltpu.CompilerParams(
            dimension_semantics=("arbitrary", "arbitrary"), vmem_limit_bytes=VMEM_LIMIT),
        name="tail",
    )(x, ya, yl_il, p, glru, wout, gmlp, wup, wdown, gple, wgate, wproj, gfin)


def _rope_tables_t(seq_len):
    pos = jnp.arange(seq_len)
    row = (pos // GRID_W).astype(F32)
    col = (pos % GRID_W).astype(F32)
    inv_freq = ROPE_THETA ** (-jnp.arange(N_FREQ, dtype=F32) / N_FREQ)
    ang_r = row[None, :] * inv_freq[:, None]
    ang_c = col[None, :] * inv_freq[:, None]
    cos_t = jnp.concatenate([jnp.cos(ang_r)] * 2 + [jnp.cos(ang_c)] * 2, axis=0)
    sin_t = jnp.concatenate(
        [-jnp.sin(ang_r), jnp.sin(ang_r), -jnp.sin(ang_c), jnp.sin(ang_c)], axis=0)
    return cos_t, sin_t


def _block_diag_pairs(w):
    n_dir, n_blk, bw, _ = w.shape
    per = LANES // bw
    w5 = w.reshape(n_dir, n_blk // per, per, bw, bw)
    eye = jnp.eye(per, dtype=w.dtype)
    bd = jnp.einsum("egikj,im->egikmj", w5, eye)
    return bd.reshape(n_dir, n_blk // per, LANES, LANES)


def kernel(x, p, mix_norm, w_in, q_norm, k_norm, conv_w, conv_b, lru_wa, lru_ba, lru_wx, lru_bx,
           lru_lambda, attn_out_norm, lru_out_norm, w_out, mlp_norm, w_up, w_down, ple_norm,
           w_ple_gate, w_ple_proj, final_norm):
    b, s, d = x.shape
    assert w_in.shape[0] == 1, "single-layer trunk: the final norm is fused into the layer tail"
    t_len = s // N_SEG
    n_groups = D_LRU // LANES
    cos_t, sin_t = _rope_tables_t(s)
    row2 = lambda v: v.reshape(1, -1)
    col2 = lambda v: v.reshape(-1, 1)
    h = x
    for l in range(1):
        wqkv_t = w_in[l][:, :D_ATTN + 2 * D_KV].T.astype(BF16)
        wrg = w_in[l][:, D_ATTN + 2 * D_KV:].astype(BF16)
        qt, k, vt, xr, xg = _in_proj(h, row2(mix_norm[l]), wqkv_t, wrg,
                                     col2(q_norm[l]), col2(k_norm[l]), cos_t, sin_t)
        ya = _attention(qt, k, vt, col2(attn_out_norm[l]))

        wa_bd = _block_diag_pairs(lru_wa[l])
        wx_bd = _block_diag_pairs(lru_wx[l])
        wg = jnp.concatenate([wa_bd[0], wa_bd[1], wx_bd[0], wx_bd[1]], axis=-1).astype(BF16)
        grp = lambda v: v.reshape(n_groups, 1, LANES)
        bg = jnp.concatenate([grp(lru_ba[l][0]), grp(lru_ba[l][1]),
                              grp(lru_bx[l][0]), grp(lru_bx[l][1])], axis=-1)
        lam = lru_lambda[l].reshape(2, n_groups, LANES).transpose(1, 0, 2)
        il = lambda v: v.reshape(b, t_len, N_SEG, D_LRU)
        yl = _lru(il(xr), il(xg), conv_w[l], row2(conv_b[l]), wg, bg, lam)

        h = _tail(h, ya, yl, p[l], row2(lru_out_norm[l]), w_out[l].astype(BF16),
                  row2(mlp_norm[l]), w_up[l].astype(BF16), w_down[l].astype(BF16),
                  row2(ple_norm[l]), w_ple_gate[l].astype(BF16), w_ple_proj[l].astype(BF16),
                  row2(final_norm))
    return h
```

```python
import jax
import jax.numpy as jnp
from jax import lax
from jax.experimental import pallas as pl
from jax.experimental.pallas import tpu as pltpu

D_MODEL = 1024
GRID_W = 64
HEAD_DIM = 64
D_ATTN = 512
N_Q_HEADS = 8
N_KV_HEADS = 2
Q_PER_KV = 4
D_KV = 128
ROPE_THETA = 10000.0
N_FREQ = 16
D_LRU = 512
LRU_BLOCK_W = 64
LRU_C = 8.0
CONV_W = 4
D_FF = 4096
D_PLE = 256
NORM_EPS = 1e-6

SUBLANES = 8
LANES = 128

N_SEG = SUBLANES
Q_TILE = 256
KEY_CHUNK = 256
FAST_CHUNK = 256
BOUND_SLACK = 1.001
L_MIN_OK = 2.0 ** -80
GATE_CHUNK = 64
VMEM_LIMIT = 56 * 1024 * 1024
Q_SCALE = HEAD_DIM ** -0.5 * 1.4426950408889634

F32 = jnp.float32
BF16 = jnp.bfloat16


def _rms(x, axis):
    return lax.rsqrt(jnp.mean(x * x, axis=axis, keepdims=True) + NORM_EPS)


def _dot(a, b):
    return jnp.dot(a, b, preferred_element_type=F32)


def _dot_nt(a, b):
    return lax.dot_general(a, b, (((1,), (1,)), ((), ())), preferred_element_type=F32)


def _norm_rope_t(xt, gcol, cos_t, sin_t, n_heads):
    t = xt.shape[1]
    x3 = xt.reshape(n_heads, HEAD_DIM, t)
    xn = x3 * _rms(x3, 1) * gcol[None]
    x5 = xn.reshape(n_heads * 2, 2, N_FREQ, t)
    xs = jnp.concatenate([x5[:, 1:2], x5[:, 0:1]], axis=1).reshape(n_heads, HEAD_DIM, t)
    out = xn * cos_t[None] + xs * sin_t[None]
    return out.reshape(n_heads * HEAD_DIM, t)


def _in_proj_kernel(x_ref, gmix_ref, wqkv_t_ref, wrg_ref, gq_ref, gk_ref, cos_ref, sin_ref,
                    qt_ref, k_ref, vt_ref, xr_ref, xg_ref):
    x = x_ref[0]
    hn = (x * _rms(x, -1) * gmix_ref[...]).astype(BF16)
    zt = _dot_nt(wqkv_t_ref[...], hn)
    zr = _dot(hn, wrg_ref[...])
    cos_t = cos_ref[...]
    sin_t = sin_ref[...]
    qt = _norm_rope_t(zt[:D_ATTN], gq_ref[...] * Q_SCALE, cos_t, sin_t, N_Q_HEADS)
    kt = _norm_rope_t(zt[D_ATTN:D_ATTN + D_KV], gk_ref[...], cos_t, sin_t, N_KV_HEADS)
    qt_ref[0] = qt.astype(BF16)
    k_ref[0] = kt.T.astype(BF16)
    vt_ref[0] = zt[D_ATTN + D_KV:].astype(BF16)
    xr_ref[0] = zr[:, :D_LRU]
    xg_ref[0] = zr[:, D_LRU:]


def _in_proj(x, gmix, wqkv_t, wrg, gq, gk, cos_t, sin_t):
    b, s, d = x.shape
    t = s // N_SEG
    const = lambda *_: (0, 0)
    return pl.pallas_call(
        _in_proj_kernel,
        grid=(b, N_SEG),
        in_specs=[
            pl.BlockSpec((1, t, d), lambda i, j: (i, j, 0)),
            pl.BlockSpec((1, d), const),
            pl.BlockSpec(wqkv_t.shape, const),
            pl.BlockSpec(wrg.shape, const),
            pl.BlockSpec((HEAD_DIM, 1), const),
            pl.BlockSpec((HEAD_DIM, 1), const),
            pl.BlockSpec((HEAD_DIM, t), lambda i, j: (0, j)),
            pl.BlockSpec((HEAD_DIM, t), lambda i, j: (0, j)),
        ],
        out_specs=[
            pl.BlockSpec((1, D_ATTN, t), lambda i, j: (i, 0, j)),
            pl.BlockSpec((1, t, D_KV), lambda i, j: (i, j, 0)),
            pl.BlockSpec((1, D_KV, t), lambda i, j: (i, 0, j)),
            pl.BlockSpec((1, t, D_LRU), lambda i, j: (i, 0, j)),
            pl.BlockSpec((1, t, D_LRU), lambda i, j: (i, 0, j)),
        ],
        out_shape=[
            jax.ShapeDtypeStruct((b, D_ATTN, s), BF16),
            jax.ShapeDtypeStruct((b, s, D_KV), BF16),
            jax.ShapeDtypeStruct((b, D_KV, s), BF16),
            jax.ShapeDtypeStruct((b, t, N_SEG * D_LRU), F32),
            jax.ShapeDtypeStruct((b, t, N_SEG * D_LRU), F32),
        ],
        compiler_params=pltpu.CompilerParams(
            dimension_semantics=("arbitrary", "arbitrary"), vmem_limit_bytes=VMEM_LIMIT),
        name="in_proj",
    )(x, gmix, wqkv_t, wrg, gq, gk, cos_t, sin_t)


def _attn_kernel(qt_ref, k_ref, vt_ref, gcol_ref, y_ref, kn_ref):
    s = k_ref.shape[1]
    tq = qt_ref.shape[2]

    @pl.when(pl.program_id(1) == 0)
    def _():
        kf = k_ref[0].astype(F32)
        lane = lax.broadcasted_iota(jnp.int32, kf.shape, 1)
        ksq = kf * kf
        for kv in range(N_KV_HEADS):
            in_head = (lane >= kv * HEAD_DIM) & (lane < (kv + 1) * HEAD_DIM)
            n2 = jnp.sum(jnp.where(in_head, ksq, 0.0), axis=1, keepdims=True)
            kn_ref[kv] = jnp.broadcast_to(jnp.sqrt(jnp.max(n2, axis=0, keepdims=True)), (1, LANES))

    qt = qt_ref[0]
    zeros = jnp.zeros((HEAD_DIM, tq), BF16)

    def q_operand(kv):
        cols = []
        for j in range(Q_PER_KV):
            h = kv * Q_PER_KV + j
            qh = qt[h * HEAD_DIM:(h + 1) * HEAD_DIM]
            cols.append(jnp.concatenate([qh, zeros] if kv == 0 else [zeros, qh], axis=0))
        return jnp.concatenate(cols, axis=1)

    def finish(accs, dens):
        outs = []
        for acc, den in zip(accs, dens):
            o = acc / den
            for j in range(Q_PER_KV):
                outs.append(o[:, j * tq:(j + 1) * tq])
        ot = jnp.concatenate(outs, axis=0)
        yt = ot * _rms(ot, 0) * gcol_ref[...]
        y_ref[0] = yt.T.astype(BF16)

    def scores_fn(qst, chunk):
        return lambda c: _dot(k_ref[0, c * chunk:(c + 1) * chunk, :], qst)

    def pv_fn(kv, chunk):
        rows = slice(kv * HEAD_DIM, (kv + 1) * HEAD_DIM)
        return lambda c, p: _dot(vt_ref[0, rows, c * chunk:(c + 1) * chunk], p)

    def col_sum(p):
        return jnp.sum(p.reshape(p.shape[0] // SUBLANES, SUBLANES, p.shape[1]), axis=0)

    qf = qt.astype(F32).reshape(N_Q_HEADS, HEAD_DIM, tq)
    qn = jnp.sqrt(jnp.sum(qf * qf, axis=1))
    accs, dens = [], []
    for kv in range(N_KV_HEADS):
        bound = qn[kv * Q_PER_KV:(kv + 1) * Q_PER_KV] * (kn_ref[kv][:, :1] * BOUND_SLACK)
        m = jnp.concatenate([bound[j:j + 1] for j in range(Q_PER_KV)], axis=1)
        scores = scores_fn(q_operand(kv), FAST_CHUNK)
        pv = pv_fn(kv, FAST_CHUNK)
        n_chunks = s // FAST_CHUNK
        acc = None
        den8 = None
        sc_next = scores(0)
        for c in range(n_chunks):
            sc = sc_next
            if c + 1 < n_chunks:
                sc_next = scores(c + 1)
            p = jnp.exp2(sc - m)
            den8 = col_sum(p) if den8 is None else den8 + col_sum(p)
            o = pv(c, p.astype(BF16))
            acc = o if acc is None else acc + o
        accs.append(acc)
        dens.append(jnp.sum(den8, axis=0, keepdims=True))
    finish(accs, dens)
    lmin = jnp.min(jnp.minimum(dens[0], dens[1]))

    @pl.when(jnp.logical_not(lmin > L_MIN_OK))
    def _():
        accs, dens = [], []
        for kv in range(N_KV_HEADS):
            scores = scores_fn(q_operand(kv), KEY_CHUNK)
            pv = pv_fn(kv, KEY_CHUNK)
            n_chunks = s // KEY_CHUNK
            m = None
            acc = None
            den = None
            sc_next = scores(0)
            for c in range(n_chunks):
                sc = sc_next
                if c + 1 < n_chunks:
                    sc_next = scores(c + 1)
                mc = jnp.max(sc, axis=0, keepdims=True)
                m_new = mc if m is None else jnp.maximum(m, mc)
                p = jnp.exp2(sc - m_new)
                o = pv(c, p.astype(BF16))
                psum = jnp.sum(p, axis=0, keepdims=True)
                if m is None:
                    acc, den = o, psum
                else:
                    alpha = jnp.exp2(m - m_new)
                    acc = acc * alpha + o
                    den = den * alpha + psum
                m = m_new
            accs.append(acc)
            dens.append(den)
        finish(accs, dens)


def _attention(qt, k, vt, gcol):
    b, _, s = qt.shape
    return pl.pallas_call(
        _attn_kernel,
        grid=(b, s // Q_TILE),
        in_specs=[
            pl.BlockSpec((1, D_ATTN, Q_TILE), lambda i, j: (i, 0, j)),
            pl.BlockSpec((1, s, D_KV), lambda i, j: (i, 0, 0)),
            pl.BlockSpec((1, D_KV, s), lambda i, j: (i, 0, 0)),
            pl.BlockSpec((D_ATTN, 1), lambda i, j: (0, 0)),
        ],
        out_specs=pl.BlockSpec((1, Q_TILE, D_ATTN), lambda i, j: (i, j, 0)),
        out_shape=jax.ShapeDtypeStruct((b, s, D_ATTN), BF16),
        scratch_shapes=[pltpu.VMEM((N_KV_HEADS, 1, LANES), F32)],
        compiler_params=pltpu.CompilerParams(
            dimension_semantics=("arbitrary", "arbitrary"), vmem_limit_bytes=VMEM_LIMIT),
        name="attention",
    )(qt, k, vt, gcol)


def _shift_seg_down(x):
    row = lax.broadcasted_iota(jnp.int32, x.shape, 0)
    return jnp.where(row == 0, 0.0, pltpu.roll(x, 1, 0))


def _shift_seg_up(x):
    row = lax.broadcasted_iota(jnp.int32, x.shape, 0)
    return jnp.where(row == SUBLANES - 1, 0.0, pltpu.roll(x, SUBLANES - 1, 0))


def _sigmoid(x):
    return 0.5 * jnp.tanh(0.5 * x) + 0.5


def _gelu_tanh(x):
    k = 0.7978845608028654
    hx = 0.5 * x
    t = jnp.tanh(x * (k + (k * 0.044715) * (x * x)))
    return hx * t + hx


def _lru_kernel(xr_ref, xg_ref, cw_ref, cb_ref, wg_ref, bg_ref, lam_ref, y_ref,
                xp_ref, a0_ref, u0_ref, a1_ref, u1_ref, h0_ref, p0_ref, h1_ref, p1_ref, ystage_ref):
    t_len = xr_ref.shape[1]
    lanes = xr_ref.shape[3]
    n_chunks = t_len // GATE_CHUNK

    xp_ref[0] = _shift_seg_down(xr_ref[0, t_len - 2])
    xp_ref[1] = _shift_seg_down(xr_ref[0, t_len - 1])
    xp_ref[t_len + 2] = _shift_seg_up(xr_ref[0, 0])

    def copy_body(i, carry):
        t0 = pl.multiple_of(i * GATE_CHUNK, GATE_CHUNK)
        xp_ref[pl.ds(t0 + 2, GATE_CHUNK)] = xr_ref[0, pl.ds(t0, GATE_CHUNK)]
        return carry

    lax.fori_loop(0, n_chunks, copy_body, 0)

    cw = cw_ref[...]
    cb = cb_ref[...]
    lam = lam_ref[0]
    hc = (0.5 * LRU_C * 1.4426950408889634) * (
        jnp.minimum(lam, 0.0) - jnp.log1p(jnp.exp(-jnp.abs(lam))))
    wg_half = wg_ref[0] * 0.5
    bg_half = bg_ref[0] * 0.5

    def gate_body(i, carry):
        t0 = pl.multiple_of(i * GATE_CHUNK, GATE_CHUNK)
        xc = cb[None]
        for kk in range(CONV_W):
            xc = xc + xp_ref[pl.ds(t0 + kk, GATE_CHUNK)] * cw[kk:kk + 1][None]
        xc2 = xc.reshape(GATE_CHUNK * SUBLANES, lanes)
        th = jnp.tanh(_dot(xc2.astype(BF16), wg_half) + bg_half)
        xh = 0.5 * xc2
        for e, (a_ref, u_ref) in enumerate(((a0_ref, u0_ref), (a1_ref, u1_ref))):
            tr = th[:, e * lanes:(e + 1) * lanes]
            ti = th[:, (2 + e) * lanes:(3 + e) * lanes]
            a = jnp.exp2(tr * hc[e:e + 1] + hc[e:e + 1])
            y = 1.0 - a * a
            mult = y * lax.rsqrt(jnp.maximum(y, 1e-30))
            u = (ti * xh + xh) * mult
            a_ref[pl.ds(t0, GATE_CHUNK)] = a.reshape(GATE_CHUNK, SUBLANES, lanes)
            u_ref[pl.ds(t0, GATE_CHUNK)] = u.reshape(GATE_CHUNK, SUBLANES, lanes)
        return carry

    lax.fori_loop(0, n_chunks, gate_body, 0)

    def scan_body(t, carry):
        h0, p0, h1, p1 = carry
        tb = t_len - 1 - t
        a0 = a0_ref[t]
        h0 = a0 * h0 + u0_ref[t]
        p0 = a0 * p0
        h0_ref[t] = h0
        p0_ref[t] = p0
        a1 = a1_ref[tb]
        h1 = a1 * h1 + u1_ref[tb]
        p1 = a1 * p1
        h1_ref[tb] = h1
        p1_ref[tb] = p1
        return h0, p0, h1, p1

    zero = jnp.zeros((SUBLANES, lanes), F32)
    one = jnp.ones((SUBLANES, lanes), F32)
    e0, q0, e1, q1 = lax.fori_loop(0, t_len, scan_body, (zero, one, zero, one), unroll=8)

    c0 = zero
    c1 = zero
    for _ in range(N_SEG - 1):
        c0 = _shift_seg_down(e0 + q0 * c0)
        c1 = _shift_seg_up(e1 + q1 * c1)

    def out_body(i, carry):
        t0 = pl.multiple_of(i * GATE_CHUNK, GATE_CHUNK)
        sl = pl.ds(t0, GATE_CHUNK)
        h = (h0_ref[sl] + p0_ref[sl] * c0[None]) + (h1_ref[sl] + p1_ref[sl] * c1[None])
        y = h * _gelu_tanh(xg_ref[0, sl])
        ystage_ref[...] = y.reshape(GATE_CHUNK * SUBLANES, lanes)
        for seg in range(N_SEG):
            y_ref[0, pl.ds(seg * t_len + t0, GATE_CHUNK)] = (
                ystage_ref[pl.ds(seg, GATE_CHUNK, stride=SUBLANES)])
        return carry

    lax.fori_loop(0, n_chunks, out_body, 0)


def _lru(xr_il, xg_il, conv_w, conv_b, wg, bg, lam):
    b, t_len, n_seg, d = xr_il.shape
    n_groups = d // LANES
    io_spec = pl.BlockSpec((1, t_len, n_seg, LANES), lambda i, j: (i, 0, 0, j))
    slab = pltpu.VMEM((t_len, n_seg, LANES), F32)
    return pl.pallas_call(
        _lru_kernel,
        grid=(b, n_groups),
        in_specs=[
            io_spec,
            io_spec,
            pl.BlockSpec((CONV_W, LANES), lambda i, j: (0, j)),
            pl.BlockSpec((1, LANES), lambda i, j: (0, j)),
            pl.BlockSpec((1, LANES, 4 * LANES), lambda i, j: (j, 0, 0)),
            pl.BlockSpec((1, 1, 4 * LANES), lambda i, j: (j, 0, 0)),
            pl.BlockSpec((1, 2, LANES), lambda i, j: (j, 0, 0)),
        ],
        out_specs=pl.BlockSpec((1, t_len * n_seg, LANES), lambda i, j: (i, 0, j)),
        out_shape=jax.ShapeDtypeStruct((b, t_len * n_seg, d), F32),
        scratch_shapes=[pltpu.VMEM((t_len + CONV_W, n_seg, LANES), F32)] + [slab] * 8 + [
            pltpu.VMEM((GATE_CHUNK * n_seg, LANES), F32)],
        compiler_params=pltpu.CompilerParams(
            dimension_semantics=("arbitrary", "arbitrary"), vmem_limit_bytes=VMEM_LIMIT),
        name="rglru",
    )(xr_il, xg_il, conv_w, conv_b, wg, bg, lam)


def _tail_kernel(x_ref, ya_ref, yl_ref, p_ref, glru_ref, wout_ref, gmlp_ref, wup_ref, wdown_ref,
                 gple_ref, wgate_ref, wproj_ref, gfin_ref, o_ref):
    x = x_ref[0]
    yl = yl_ref[0]
    yl_n = (yl * _rms(yl, -1) * glru_ref[...]).astype(BF16)
    y = jnp.concatenate([ya_ref[0], yl_n], axis=-1)
    h = x + _dot(y, wout_ref[...])
    hn = (h * _rms(h, -1) * gmlp_ref[...]).astype(BF16)
    m = _dot(hn, wup_ref[...])
    act = jnp.square(jnp.maximum(m, 0.0)).astype(BF16)
    h = h + _dot(act, wdown_ref[...])
    hn = (h * _rms(h, -1) * gple_ref[...]).astype(BF16)
    gate = _sigmoid(_dot(hn, wgate_ref[...]))
    h = h + gate * _dot(p_ref[0].astype(BF16), wproj_ref[...])
    o_ref[0] = h * _rms(h, -1) * gfin_ref[...]


def _tail(x, ya, yl_il, p, glru, wout, gmlp, wup, wdown, gple, wgate, wproj, gfin):
    b, s, d = x.shape
    t = s // N_SEG
    const = lambda *_: (0, 0)

    def resident(arr):
        return pl.BlockSpec(arr.shape, const, pipeline_mode=pl.Buffered(1))

    return pl.pallas_call(
        _tail_kernel,
        grid=(b, N_SEG),
        in_specs=[
            pl.BlockSpec((1, t, d), lambda i, j: (i, j, 0)),
            pl.BlockSpec((1, t, D_ATTN), lambda i, j: (i, j, 0)),
            pl.BlockSpec((1, t, D_LRU), lambda i, j: (i, j, 0)),
            pl.BlockSpec((1, t, D_PLE), lambda i, j: (i, j, 0)),
            resident(glru), resident(wout), resident(gmlp), resident(wup), resident(wdown),
            resident(gple), resident(wgate), resident(wproj), resident(gfin),
        ],
        out_specs=pl.BlockSpec((1, t, d), lambda i, j: (i, j, 0)),
        out_shape=jax.ShapeDtypeStruct((b, s, d), F32),
        compiler_params=pltpu.CompilerParams(
            dimension_semantics=("arbitrary", "arbitrary"), vmem_limit_bytes=VMEM_LIMIT),
        name="tail",
    )(x, ya, yl_il, p, glru, wout, gmlp, wup, wdown, gple, wgate, wproj, gfin)


def _rope_tables_t(seq_len):
    pos = jnp.arange(seq_len)
    row = (pos // GRID_W).astype(F32)
    col = (pos % GRID_W).astype(F32)
    inv_freq = ROPE_THETA ** (-jnp.arange(N_FREQ, dtype=F32) / N_FREQ)
    ang_r = row[None, :] * inv_freq[:, None]
    ang_c = col[None, :] * inv_freq[:, None]
    cos_t = jnp.concatenate([jnp.cos(ang_r)] * 2 + [jnp.cos(ang_c)] * 2, axis=0)
    sin_t = jnp.concatenate(
        [-jnp.sin(ang_r), jnp.sin(ang_r), -jnp.sin(ang_c), jnp.sin(ang_c)], axis=0)
    return cos_t, sin_t


def _block_diag_pairs(w):
    n_dir, n_blk, bw, _ = w.shape
    per = LANES // bw
    w5 = w.reshape(n_dir, n_blk // per, per, bw, bw)
    eye = jnp.eye(per, dtype=w.dtype)
    bd = jnp.einsum("egikj,im->egikmj", w5, eye)
    return bd.reshape(n_dir, n_blk // per, LANES, LANES)


def kernel(x, p, mix_norm, w_in, q_norm, k_norm, conv_w, conv_b, lru_wa, lru_ba, lru_wx, lru_bx,
           lru_lambda, attn_out_norm, lru_out_norm, w_out, mlp_norm, w_up, w_down, ple_norm,
           w_ple_gate, w_ple_proj, final_norm):
    b, s, d = x.shape
    assert w_in.shape[0] == 1, "single-layer trunk: the final norm is fused into the layer tail"
    t_len = s // N_SEG
    n_groups = D_LRU // LANES
    cos_t, sin_t = _rope_tables_t(s)
    row2 = lambda v: v.reshape(1, -1)
    col2 = lambda v: v.reshape(-1, 1)
    h = x
    for l in range(1):
        wqkv_t = w_in[l][:, :D_ATTN + 2 * D_KV].T.astype(BF16)
        wrg = w_in[l][:, D_ATTN + 2 * D_KV:].astype(BF16)
        qt, k, vt, xr, xg = _in_proj(h, row2(mix_norm[l]), wqkv_t, wrg,
                                     col2(q_norm[l]), col2(k_norm[l]), cos_t, sin_t)
        ya = _attention(qt, k, vt, col2(attn_out_norm[l]))

        wa_bd = _block_diag_pairs(lru_wa[l])
        wx_bd = _block_diag_pairs(lru_wx[l])
        wg = jnp.concatenate([wa_bd[0], wa_bd[1], wx_bd[0], wx_bd[1]], axis=-1).astype(BF16)
        grp = lambda v: v.reshape(n_groups, 1, LANES)
        bg = jnp.concatenate([grp(lru_ba[l][0]), grp(lru_ba[l][1]),
                              grp(lru_bx[l][0]), grp(lru_bx[l][1])], axis=-1)
        lam = lru_lambda[l].reshape(2, n_groups, LANES).transpose(1, 0, 2)
        il = lambda v: v.reshape(b, t_len, N_SEG, D_LRU)
        yl = _lru(il(xr), il(xg), conv_w[l], row2(conv_b[l]), wg, bg, lam)

        h = _tail(h, ya, yl, p[l], row2(lru_out_norm[l]), w_out[l].astype(BF16),
                  row2(mlp_norm[l]), w_up[l].astype(BF16), w_down[l].astype(BF16),
                  row2(ple_norm[l]), w_ple_gate[l].astype(BF16), w_ple_proj[l].astype(BF16),
                  row2(final_norm))
    return h
```

```python
import jax
import jax.numpy as jnp
from jax import lax
from jax.experimental import pallas as pl
from jax.experimental.pallas import tpu as pltpu

D_MODEL = 1024
GRID_W = 64
HEAD_DIM = 64
D_ATTN = 512
N_Q_HEADS = 8
N_KV_HEADS = 2
Q_PER_KV = 4
D_KV = 128
ROPE_THETA = 10000.0
N_FREQ = 16
D_LRU = 512
LRU_BLOCK_W = 64
LRU_C = 8.0
CONV_W = 4
D_FF = 4096
D_PLE = 256
NORM_EPS = 1e-6

SUBLANES = 8
LANES = 128

N_SEG = SUBLANES
Q_TILE = 256
KEY_CHUNK = 256
FAST_CHUNK = 256
PV_ROWS = 80
BOUND_SLACK = 1.001
L_MIN_OK = 2.0 ** -80
GATE_CHUNK = 64
VMEM_LIMIT = 56 * 1024 * 1024
Q_SCALE = HEAD_DIM ** -0.5 * 1.4426950408889634

F32 = jnp.float32
BF16 = jnp.bfloat16


def _rms(x, axis):
    return lax.rsqrt(jnp.mean(x * x, axis=axis, keepdims=True) + NORM_EPS)


def _dot(a, b):
    return jnp.dot(a, b, preferred_element_type=F32)


def _dot_nt(a, b):
    return lax.dot_general(a, b, (((1,), (1,)), ((), ())), preferred_element_type=F32)


def _norm_rope_t(xt, gcol, cos_t, sin_t, n_heads):
    t = xt.shape[1]
    x3 = xt.reshape(n_heads, HEAD_DIM, t)
    xn = x3 * _rms(x3, 1) * gcol[None]
    x5 = xn.reshape(n_heads * 2, 2, N_FREQ, t)
    xs = jnp.concatenate([x5[:, 1:2], x5[:, 0:1]], axis=1).reshape(n_heads, HEAD_DIM, t)
    out = xn * cos_t[None] + xs * sin_t[None]
    return out.reshape(n_heads * HEAD_DIM, t)


def _in_proj_kernel(x_ref, gmix_ref, wqkv_t_ref, wrg_ref, gq_ref, gk_ref, cos_ref, sin_ref,
                    qt_ref, k_ref, vt_ref, xr_ref, xg_ref):
    x = x_ref[0]
    hn = (x * _rms(x, -1) * gmix_ref[...]).astype(BF16)
    zt = _dot_nt(wqkv_t_ref[...], hn)
    zr = _dot(hn, wrg_ref[...])
    cos_t = cos_ref[...]
    sin_t = sin_ref[...]
    qt = _norm_rope_t(zt[:D_ATTN], gq_ref[...] * Q_SCALE, cos_t, sin_t, N_Q_HEADS)
    kt = _norm_rope_t(zt[D_ATTN:D_ATTN + D_KV], gk_ref[...], cos_t, sin_t, N_KV_HEADS)
    qt_ref[0] = qt.astype(BF16)
    k_ref[0] = kt.T.astype(BF16)
    vt_ref[0] = zt[D_ATTN + D_KV:].astype(BF16)
    xr_ref[0] = zr[:, :D_LRU]
    xg_ref[0] = zr[:, D_LRU:]


def _in_proj(x, gmix, wqkv_t, wrg, gq, gk, cos_t, sin_t):
    b, s, d = x.shape
    t = s // N_SEG
    const = lambda *_: (0, 0)
    return pl.pallas_call(
        _in_proj_kernel,
        grid=(b, N_SEG),
        in_specs=[
            pl.BlockSpec((1, t, d), lambda i, j: (i, j, 0)),
            pl.BlockSpec((1, d), const),
            pl.BlockSpec(wqkv_t.shape, const),
            pl.BlockSpec(wrg.shape, const),
            pl.BlockSpec((HEAD_DIM, 1), const),
            pl.BlockSpec((HEAD_DIM, 1), const),
            pl.BlockSpec((HEAD_DIM, t), lambda i, j: (0, j)),
            pl.BlockSpec((HEAD_DIM, t), lambda i, j: (0, j)),
        ],
        out_specs=[
            pl.BlockSpec((1, D_ATTN, t), lambda i, j: (i, 0, j)),
            pl.BlockSpec((1, t, D_KV), lambda i, j: (i, j, 0)),
            pl.BlockSpec((1, D_KV, t), lambda i, j: (i, 0, j)),
            pl.BlockSpec((1, t, D_LRU), lambda i, j: (i, 0, j)),
            pl.BlockSpec((1, t, D_LRU), lambda i, j: (i, 0, j)),
        ],
        out_shape=[
            jax.ShapeDtypeStruct((b, D_ATTN, s), BF16),
            jax.ShapeDtypeStruct((b, s, D_KV), BF16),
            jax.ShapeDtypeStruct((b, D_KV, s), BF16),
            jax.ShapeDtypeStruct((b, t, N_SEG * D_LRU), F32),
            jax.ShapeDtypeStruct((b, t, N_SEG * D_LRU), F32),
        ],
        compiler_params=pltpu.CompilerParams(
            dimension_semantics=("arbitrary", "arbitrary"), vmem_limit_bytes=VMEM_LIMIT),
        name="in_proj",
    )(x, gmix, wqkv_t, wrg, gq, gk, cos_t, sin_t)


def _attn_kernel(qt_ref, k_ref, vt_ref, gcol_ref, y_ref, vta_ref, kn_ref):
    s = k_ref.shape[1]
    tq = qt_ref.shape[2]

    @pl.when(pl.program_id(1) == 0)
    def _():
        vt = vt_ref[0]
        row = lax.broadcasted_iota(jnp.int32, (PV_ROWS - HEAD_DIM, s), 0)
        ones_pad = jnp.where(row == 0, 1.0, 0.0).astype(BF16)
        kf = k_ref[0].astype(F32)
        lane = lax.broadcasted_iota(jnp.int32, kf.shape, 1)
        ksq = kf * kf
        for kv in range(N_KV_HEADS):
            vta_ref[kv] = jnp.concatenate([vt[kv * HEAD_DIM:(kv + 1) * HEAD_DIM], ones_pad], axis=0)
            in_head = (lane >= kv * HEAD_DIM) & (lane < (kv + 1) * HEAD_DIM)
            n2 = jnp.sum(jnp.where(in_head, ksq, 0.0), axis=1, keepdims=True)
            kn_ref[kv] = jnp.broadcast_to(jnp.sqrt(jnp.max(n2, axis=0, keepdims=True)), (1, LANES))

    qt = qt_ref[0]
    zeros = jnp.zeros((HEAD_DIM, tq), BF16)

    def q_operand(kv):
        cols = []
        for j in range(Q_PER_KV):
            h = kv * Q_PER_KV + j
            qh = qt[h * HEAD_DIM:(h + 1) * HEAD_DIM]
            cols.append(jnp.concatenate([qh, zeros] if kv == 0 else [zeros, qh], axis=0))
        return jnp.concatenate(cols, axis=1)

    def finish(accs):
        outs = []
        for acc in accs:
            o = acc[:HEAD_DIM] / acc[HEAD_DIM:HEAD_DIM + 1]
            for j in range(Q_PER_KV):
                outs.append(o[:, j * tq:(j + 1) * tq])
        ot = jnp.concatenate(outs, axis=0)
        yt = ot * _rms(ot, 0) * gcol_ref[...]
        y_ref[0] = yt.T.astype(BF16)

    def scores_fn(qst, chunk):
        return lambda c: _dot(k_ref[0, c * chunk:(c + 1) * chunk, :], qst)

    qf = qt.astype(F32).reshape(N_Q_HEADS, HEAD_DIM, tq)
    qn = jnp.sqrt(jnp.sum(qf * qf, axis=1))
    accs = []
    for kv in range(N_KV_HEADS):
        bound = qn[kv * Q_PER_KV:(kv + 1) * Q_PER_KV] * (kn_ref[kv][:, :1] * BOUND_SLACK)
        m = jnp.concatenate([bound[j:j + 1] for j in range(Q_PER_KV)], axis=1)
        scores = scores_fn(q_operand(kv), FAST_CHUNK)
        n_chunks = s // FAST_CHUNK
        acc = None
        sc_next = scores(0)
        for c in range(n_chunks):
            sc = sc_next
            if c + 1 < n_chunks:
                sc_next = scores(c + 1)
            p = jnp.exp2(sc - m).astype(BF16)
            pv = _dot(vta_ref[kv, :, c * FAST_CHUNK:(c + 1) * FAST_CHUNK], p)
            acc = pv if acc is None else acc + pv
        accs.append(acc)
    finish(accs)
    lmin = jnp.min(jnp.minimum(accs[0][HEAD_DIM:HEAD_DIM + 1], accs[1][HEAD_DIM:HEAD_DIM + 1]))

    @pl.when(jnp.logical_not(lmin > L_MIN_OK))
    def _():
        accs = []
        for kv in range(N_KV_HEADS):
            scores = scores_fn(q_operand(kv), KEY_CHUNK)
            n_chunks = s // KEY_CHUNK
            m = None
            acc = None
            sc_next = scores(0)
            for c in range(n_chunks):
                sc = sc_next
                if c + 1 < n_chunks:
                    sc_next = scores(c + 1)
                mc = jnp.max(sc, axis=0, keepdims=True)
                m_new = mc if m is None else jnp.maximum(m, mc)
                p = jnp.exp2(sc - m_new).astype(BF16)
                pv = _dot(vta_ref[kv, :, c * KEY_CHUNK:(c + 1) * KEY_CHUNK], p)
                acc = pv if acc is None else acc * jnp.exp2(m - m_new) + pv
                m = m_new
            accs.append(acc)
        finish(accs)


def _attention(qt, k, vt, gcol):
    b, _, s = qt.shape
    return pl.pallas_call(
        _attn_kernel,
        grid=(b, s // Q_TILE),
        in_specs=[
            pl.BlockSpec((1, D_ATTN, Q_TILE), lambda i, j: (i, 0, j)),
            pl.BlockSpec((1, s, D_KV), lambda i, j: (i, 0, 0)),
            pl.BlockSpec((1, D_KV, s), lambda i, j: (i, 0, 0)),
            pl.BlockSpec((D_ATTN, 1), lambda i, j: (0, 0)),
        ],
        out_specs=pl.BlockSpec((1, Q_TILE, D_ATTN), lambda i, j: (i, j, 0)),
        out_shape=jax.ShapeDtypeStruct((b, s, D_ATTN), BF16),
        scratch_shapes=[pltpu.VMEM((N_KV_HEADS, PV_ROWS, s), BF16),
                        pltpu.VMEM((N_KV_HEADS, 1, LANES), F32)],
        compiler_params=pltpu.CompilerParams(
            dimension_semantics=("arbitrary", "arbitrary"), vmem_limit_bytes=VMEM_LIMIT),
        name="attention",
    )(qt, k, vt, gcol)


def _shift_seg_down(x):
    row = lax.broadcasted_iota(jnp.int32, x.shape, 0)
    return jnp.where(row == 0, 0.0, pltpu.roll(x, 1, 0))


def _shift_seg_up(x):
    row = lax.broadcasted_iota(jnp.int32, x.shape, 0)
    return jnp.where(row == SUBLANES - 1, 0.0, pltpu.roll(x, SUBLANES - 1, 0))


def _sigmoid(x):
    return 0.5 * jnp.tanh(0.5 * x) + 0.5


def _gelu_tanh(x):
    k = 0.7978845608028654
    hx = 0.5 * x
    t = jnp.tanh(x * (k + (k * 0.044715) * (x * x)))
    return hx * t + hx


def _lru_kernel(xr_ref, xg_ref, cw_ref, cb_ref, wg_ref, bg_ref, lam_ref, y_ref,
                xp_ref, a0_ref, u0_ref, a1_ref, u1_ref, h0_ref, p0_ref, h1_ref, p1_ref, ystage_ref):
    t_len = xr_ref.shape[1]
    lanes = xr_ref.shape[3]
    n_chunks = t_len // GATE_CHUNK

    xp_ref[0] = _shift_seg_down(xr_ref[0, t_len - 2])
    xp_ref[1] = _shift_seg_down(xr_ref[0, t_len - 1])
    xp_ref[t_len + 2] = _shift_seg_up(xr_ref[0, 0])

    def copy_body(i, carry):
        t0 = pl.multiple_of(i * GATE_CHUNK, GATE_CHUNK)
        xp_ref[pl.ds(t0 + 2, GATE_CHUNK)] = xr_ref[0, pl.ds(t0, GATE_CHUNK)]
        return carry

    lax.fori_loop(0, n_chunks, copy_body, 0)

    cw = cw_ref[...]
    cb = cb_ref[...]
    lam = lam_ref[0]
    hc = (0.5 * LRU_C * 1.4426950408889634) * (
        jnp.minimum(lam, 0.0) - jnp.log1p(jnp.exp(-jnp.abs(lam))))
    wg_half = wg_ref[0] * 0.5
    bg_half = bg_ref[0] * 0.5

    def gate_body(i, carry):
        t0 = pl.multiple_of(i * GATE_CHUNK, GATE_CHUNK)
        xc = cb[None]
        for kk in range(CONV_W):
            xc = xc + xp_ref[pl.ds(t0 + kk, GATE_CHUNK)] * cw[kk:kk + 1][None]
        xc2 = xc.reshape(GATE_CHUNK * SUBLANES, lanes)
        th = jnp.tanh(_dot(xc2.astype(BF16), wg_half) + bg_half)
        xh = 0.5 * xc2
        for e, (a_ref, u_ref) in enumerate(((a0_ref, u0_ref), (a1_ref, u1_ref))):
            tr = th[:, e * lanes:(e + 1) * lanes]
            ti = th[:, (2 + e) * lanes:(3 + e) * lanes]
            a = jnp.exp2(tr * hc[e:e + 1] + hc[e:e + 1])
            y = 1.0 - a * a
            mult = y * lax.rsqrt(jnp.maximum(y, 1e-30))
            u = (ti * xh + xh) * mult
            a_ref[pl.ds(t0, GATE_CHUNK)] = a.reshape(GATE_CHUNK, SUBLANES, lanes)
            u_ref[pl.ds(t0, GATE_CHUNK)] = u.reshape(GATE_CHUNK, SUBLANES, lanes)
        return carry

    lax.fori_loop(0, n_chunks, gate_body, 0)

    def scan_body(t, carry):
        h0, p0, h1, p1 = carry
        tb = t_len - 1 - t
        a0 = a0_ref[t]
        h0 = a0 * h0 + u0_ref[t]
        p0 = a0 * p0
        h0_ref[t] = h0
        p0_ref[t] = p0
        a1 = a1_ref[tb]
        h1 = a1 * h1 + u1_ref[tb]
        p1 = a1 * p1
        h1_ref[tb] = h1
        p1_ref[tb] = p1
        return h0, p0, h1, p1

    zero = jnp.zeros((SUBLANES, lanes), F32)
    one = jnp.ones((SUBLANES, lanes), F32)
    e0, q0, e1, q1 = lax.fori_loop(0, t_len, scan_body, (zero, one, zero, one), unroll=8)

    c0 = zero
    c1 = zero
    for _ in range(N_SEG - 1):
        c0 = _shift_seg_down(e0 + q0 * c0)
        c1 = _shift_seg_up(e1 + q1 * c1)

    def out_body(i, carry):
        t0 = pl.multiple_of(i * GATE_CHUNK, GATE_CHUNK)
        sl = pl.ds(t0, GATE_CHUNK)
        h = (h0_ref[sl] + p0_ref[sl] * c0[None]) + (h1_ref[sl] + p1_ref[sl] * c1[None])
        y = h * _gelu_tanh(xg_ref[0, sl])
        ystage_ref[...] = y.reshape(GATE_CHUNK * SUBLANES, lanes)
        for seg in range(N_SEG):
            y_ref[0, pl.ds(seg * t_len + t0, GATE_CHUNK)] = (
                ystage_ref[pl.ds(seg, GATE_CHUNK, stride=SUBLANES)])
        return carry

    lax.fori_loop(0, n_chunks, out_body, 0)


def _lru(xr_il, xg_il, conv_w, conv_b, wg, bg, lam):
    b, t_len, n_seg, d = xr_il.shape
    n_groups = d // LANES
    io_spec = pl.BlockSpec((1, t_len, n_seg, LANES), lambda i, j: (i, 0, 0, j))
    slab = pltpu.VMEM((t_len, n_seg, LANES), F32)
    return pl.pallas_call(
        _lru_kernel,
        grid=(b, n_groups),
        in_specs=[
            io_spec,
            io_spec,
            pl.BlockSpec((CONV_W, LANES), lambda i, j: (0, j)),
            pl.BlockSpec((1, LANES), lambda i, j: (0, j)),
            pl.BlockSpec((1, LANES, 4 * LANES), lambda i, j: (j, 0, 0)),
            pl.BlockSpec((1, 1, 4 * LANES), lambda i, j: (j, 0, 0)),
            pl.BlockSpec((1, 2, LANES), lambda i, j: (j, 0, 0)),
        ],
        out_specs=pl.BlockSpec((1, t_len * n_seg, LANES), lambda i, j: (i, 0, j)),
        out_shape=jax.ShapeDtypeStruct((b, t_len * n_seg, d), F32),
        scratch_shapes=[pltpu.VMEM((t_len + CONV_W, n_seg, LANES), F32)] + [slab] * 8 + [
            pltpu.VMEM((GATE_CHUNK * n_seg, LANES), F32)],
        compiler_params=pltpu.CompilerParams(
            dimension_semantics=("arbitrary", "arbitrary"), vmem_limit_bytes=VMEM_LIMIT),
        name="rglru",
    )(xr_il, xg_il, conv_w, conv_b, wg, bg, lam)


def _tail_kernel(x_ref, ya_ref, yl_ref, p_ref, glru_ref, wout_ref, gmlp_ref, wup_ref, wdown_ref,
                 gple_ref, wgate_ref, wproj_ref, gfin_ref, o_ref):
    x = x_ref[0]
    yl = yl_ref[0]
    yl_n = (yl * _rms(yl, -1) * glru_ref[...]).astype(BF16)
    y = jnp.concatenate([ya_ref[0], yl_n], axis=-1)
    h = x + _dot(y, wout_ref[...])
    hn = (h * _rms(h, -1) * gmlp_ref[...]).astype(BF16)
    m = _dot(hn, wup_ref[...])
    act = jnp.square(jnp.maximum(m, 0.0)).astype(BF16)
    h = h + _dot(act, wdown_ref[...])
    hn = (h * _rms(h, -1) * gple_ref[...]).astype(BF16)
    gate = _sigmoid(_dot(hn, wgate_ref[...]))
    h = h + gate * _dot(p_ref[0].astype(BF16), wproj_ref[...])
    o_ref[0] = h * _rms(h, -1) * gfin_ref[...]


def _tail(x, ya, yl_il, p, glru, wout, gmlp, wup, wdown, gple, wgate, wproj, gfin):
    b, s, d = x.shape
    t = s // N_SEG
    const = lambda *_: (0, 0)

    def resident(arr):
        return pl.BlockSpec(arr.shape, const, pipeline_mode=pl.Buffered(1))

    return pl.pallas_call(
        _tail_kernel,
        grid=(b, N_SEG),
        in_specs=[
            pl.BlockSpec((1, t, d), lambda i, j: (i, j, 0)),
            pl.BlockSpec((1, t, D_ATTN), lambda i, j: (i, j, 0)),
            pl.BlockSpec((1, t, D_LRU), lambda i, j: (i, j, 0)),
            pl.BlockSpec((1, t, D_PLE), lambda i, j: (i, j, 0)),
            resident(glru), resident(wout), resident(gmlp), resident(wup), resident(wdown),
            resident(gple), resident(wgate), resident(wproj), resident(gfin),
        ],
        out_specs=pl.BlockSpec((1, t, d), lambda i, j: (i, j, 0)),
        out_shape=jax.ShapeDtypeStruct((b, s, d), F32),
        compiler_params=pltpu.CompilerParams(
            dimension_semantics=("arbitrary", "arbitrary"), vmem_limit_bytes=VMEM_LIMIT),
        name="tail",
    )(x, ya, yl_il, p, glru, wout, gmlp, wup, wdown, gple, wgate, wproj, gfin)


def _rope_tables_t(seq_len):
    pos = jnp.arange(seq_len)
    row = (pos // GRID_W).astype(F32)
    col = (pos % GRID_W).astype(F32)
    inv_freq = ROPE_THETA ** (-jnp.arange(N_FREQ, dtype=F32) / N_FREQ)
    ang_r = row[None, :] * inv_freq[:, None]
    ang_c = col[None, :] * inv_freq[:, None]
    cos_t = jnp.concatenate([jnp.cos(ang_r)] * 2 + [jnp.cos(ang_c)] * 2, axis=0)
    sin_t = jnp.concatenate(
        [-jnp.sin(ang_r), jnp.sin(ang_r), -jnp.sin(ang_c), jnp.sin(ang_c)], axis=0)
    return cos_t, sin_t


def _block_diag_pairs(w):
    n_dir, n_blk, bw, _ = w.shape
    per = LANES // bw
    w5 = w.reshape(n_dir, n_blk // per, per, bw, bw)
    eye = jnp.eye(per, dtype=w.dtype)
    bd = jnp.einsum("egikj,im->egikmj", w5, eye)
    return bd.reshape(n_dir, n_blk // per, LANES, LANES)


def kernel(x, p, mix_norm, w_in, q_norm, k_norm, conv_w, conv_b, lru_wa, lru_ba, lru_wx, lru_bx,
           lru_lambda, attn_out_norm, lru_out_norm, w_out, mlp_norm, w_up, w_down, ple_norm,
           w_ple_gate, w_ple_proj, final_norm):
    b, s, d = x.shape
    assert w_in.shape[0] == 1, "single-layer trunk: the final norm is fused into the layer tail"
    t_len = s // N_SEG
    n_groups = D_LRU // LANES
    cos_t, sin_t = _rope_tables_t(s)
    row2 = lambda v: v.reshape(1, -1)
    col2 = lambda v: v.reshape(-1, 1)
    h = x
    for l in range(1):
        wqkv_t = w_in[l][:, :D_ATTN + 2 * D_KV].T.astype(BF16)
        wrg = w_in[l][:, D_ATTN + 2 * D_KV:].astype(BF16)
        qt, k, vt, xr, xg = _in_proj(h, row2(mix_norm[l]), wqkv_t, wrg,
                                     col2(q_norm[l]), col2(k_norm[l]), cos_t, sin_t)
        ya = _attention(qt, k, vt, col2(attn_out_norm[l]))

        wa_bd = _block_diag_pairs(lru_wa[l])
        wx_bd = _block_diag_pairs(lru_wx[l])
        wg = jnp.concatenate([wa_bd[0], wa_bd[1], wx_bd[0], wx_bd[1]], axis=-1).astype(BF16)
        grp = lambda v: v.reshape(n_groups, 1, LANES)
        bg = jnp.concatenate([grp(lru_ba[l][0]), grp(lru_ba[l][1]),
                              grp(lru_bx[l][0]), grp(lru_bx[l][1])], axis=-1)
        lam = lru_lambda[l].reshape(2, n_groups, LANES).transpose(1, 0, 2)
        il = lambda v: v.reshape(b, t_len, N_SEG, D_LRU)
        yl = _lru(il(xr), il(xg), conv_w[l], row2(conv_b[l]), wg, bg, lam)

        h = _tail(h, ya, yl, p[l], row2(lru_out_norm[l]), w_out[l].astype(BF16),
                  row2(mlp_norm[l]), w_up[l].astype(BF16), w_down[l].astype(BF16),
                  row2(ple_norm[l]), w_ple_gate[l].astype(BF16), w_ple_proj[l].astype(BF16),
                  row2(final_norm))
    return h
```

```python
import jax
import jax.numpy as jnp
from jax import lax
from jax.experimental import pallas as pl
from jax.experimental.pallas import tpu as pltpu

D_MODEL = 1024
GRID_W = 64
HEAD_DIM = 64
D_ATTN = 512
N_Q_HEADS = 8
N_KV_HEADS = 2
Q_PER_KV = 4
D_KV = 128
ROPE_THETA = 10000.0
N_FREQ = 16
D_LRU = 512
LRU_BLOCK_W = 64
LRU_C = 8.0
CONV_W = 4
CONV_PAD_LEFT = 2
D_FF = 4096
D_PLE = 256
NORM_EPS = 1e-6

SUBLANES = 8
LANES = 128

N_SEG = SUBLANES
Q_TILE = 256
KEY_CHUNK = 256
FAST_CHUNK = 256
PV_ROWS = 80
BOUND_SLACK = 1.001
L_MIN_OK = 2.0 ** -80
GATE_CHUNK = 64
SCAN_PART = 128
VMEM_LIMIT = 56 * 1024 * 1024
Q_SCALE = HEAD_DIM ** -0.5 * 1.4426950408889634

F32 = jnp.float32
BF16 = jnp.bfloat16


def _rms(x, axis):
    return lax.rsqrt(jnp.mean(x * x, axis=axis, keepdims=True) + NORM_EPS)


def _dot(a, b):
    return jnp.dot(a, b, preferred_element_type=F32)


def _dot_nt(a, b):
    return lax.dot_general(a, b, (((1,), (1,)), ((), ())), preferred_element_type=F32)


def _norm_rope_t(xt, gcol, cos_t, sin_t, n_heads):
    t = xt.shape[1]
    x3 = xt.reshape(n_heads, HEAD_DIM, t)
    xn = x3 * _rms(x3, 1) * gcol[None]
    x5 = xn.reshape(n_heads * 2, 2, N_FREQ, t)
    xs = jnp.concatenate([x5[:, 1:2], x5[:, 0:1]], axis=1).reshape(n_heads, HEAD_DIM, t)
    out = xn * cos_t[None] + xs * sin_t[None]
    return out.reshape(n_heads * HEAD_DIM, t)


def _in_proj_kernel(x_ref, gmix_ref, wqkv_t_ref, wrg_ref, gq_ref, gk_ref, cos_ref, sin_ref,
                    qt_ref, k_ref, vt_ref, xr_ref, xg_ref):
    x = x_ref[0]
    hn = (x * _rms(x, -1) * gmix_ref[...]).astype(BF16)
    zt = _dot_nt(wqkv_t_ref[...], hn)
    zr = _dot(hn, wrg_ref[...])
    cos_t = cos_ref[...]
    sin_t = sin_ref[...]
    qt = _norm_rope_t(zt[:D_ATTN], gq_ref[...] * Q_SCALE, cos_t, sin_t, N_Q_HEADS)
    kt = _norm_rope_t(zt[D_ATTN:D_ATTN + D_KV], gk_ref[...], cos_t, sin_t, N_KV_HEADS)
    qt_ref[0] = qt.astype(BF16)
    k_ref[0] = kt.T.astype(BF16)
    vt_ref[0] = zt[D_ATTN + D_KV:].astype(BF16)
    xr_ref[0] = zr[:, :D_LRU]
    xg_ref[0] = zr[:, D_LRU:]


def _in_proj(x, gmix, wqkv_t, wrg, gq, gk, cos_t, sin_t):
    b, s, d = x.shape
    t = s // N_SEG
    const = lambda *_: (0, 0)
    return pl.pallas_call(
        _in_proj_kernel,
        grid=(b, N_SEG),
        in_specs=[
            pl.BlockSpec((1, t, d), lambda i, j: (i, j, 0)),
            pl.BlockSpec((1, d), const),
            pl.BlockSpec(wqkv_t.shape, const),
            pl.BlockSpec(wrg.shape, const),
            pl.BlockSpec((HEAD_DIM, 1), const),
            pl.BlockSpec((HEAD_DIM, 1), const),
            pl.BlockSpec((HEAD_DIM, t), lambda i, j: (0, j)),
            pl.BlockSpec((HEAD_DIM, t), lambda i, j: (0, j)),
        ],
        out_specs=[
            pl.BlockSpec((1, D_ATTN, t), lambda i, j: (i, 0, j)),
            pl.BlockSpec((1, t, D_KV), lambda i, j: (i, j, 0)),
            pl.BlockSpec((1, D_KV, t), lambda i, j: (i, 0, j)),
            pl.BlockSpec((1, t, D_LRU), lambda i, j: (i, 0, j)),
            pl.BlockSpec((1, t, D_LRU), lambda i, j: (i, 0, j)),
        ],
        out_shape=[
            jax.ShapeDtypeStruct((b, D_ATTN, s), BF16),
            jax.ShapeDtypeStruct((b, s, D_KV), BF16),
            jax.ShapeDtypeStruct((b, D_KV, s), BF16),
            jax.ShapeDtypeStruct((b, t, N_SEG * D_LRU), F32),
            jax.ShapeDtypeStruct((b, t, N_SEG * D_LRU), F32),
        ],
        compiler_params=pltpu.CompilerParams(
            dimension_semantics=("arbitrary", "arbitrary"), vmem_limit_bytes=VMEM_LIMIT),
        name="in_proj",
    )(x, gmix, wqkv_t, wrg, gq, gk, cos_t, sin_t)


def _attn_kernel(qt_ref, k_ref, vt_ref, gcol_ref, y_ref, vta_ref, kn_ref):
    s = k_ref.shape[1]
    tq = qt_ref.shape[2]

    @pl.when(pl.program_id(1) == 0)
    def _():
        vt = vt_ref[0]
        row = lax.broadcasted_iota(jnp.int32, (PV_ROWS - HEAD_DIM, s), 0)
        ones_pad = jnp.where(row == 0, 1.0, 0.0).astype(BF16)
        kf = k_ref[0].astype(F32)
        lane = lax.broadcasted_iota(jnp.int32, kf.shape, 1)
        ksq = kf * kf
        for kv in range(N_KV_HEADS):
            vta_ref[kv] = jnp.concatenate([vt[kv * HEAD_DIM:(kv + 1) * HEAD_DIM], ones_pad], axis=0)
            in_head = (lane >= kv * HEAD_DIM) & (lane < (kv + 1) * HEAD_DIM)
            n2 = jnp.sum(jnp.where(in_head, ksq, 0.0), axis=1, keepdims=True)
            kn_ref[kv] = jnp.broadcast_to(jnp.sqrt(jnp.max(n2, axis=0, keepdims=True)), (1, LANES))

    qt = qt_ref[0]
    zeros = jnp.zeros((HEAD_DIM, tq), BF16)

    def q_operand(kv):
        cols = []
        for j in range(Q_PER_KV):
            h = kv * Q_PER_KV + j
            qh = qt[h * HEAD_DIM:(h + 1) * HEAD_DIM]
            cols.append(jnp.concatenate([qh, zeros] if kv == 0 else [zeros, qh], axis=0))
        return jnp.concatenate(cols, axis=1)

    def finish(accs):
        outs = []
        for acc in accs:
            o = acc[:HEAD_DIM] / acc[HEAD_DIM:HEAD_DIM + 1]
            for j in range(Q_PER_KV):
                outs.append(o[:, j * tq:(j + 1) * tq])
        ot = jnp.concatenate(outs, axis=0)
        yt = ot * _rms(ot, 0) * gcol_ref[...]
        y_ref[0] = yt.T.astype(BF16)

    def scores_fn(qst, chunk):
        return lambda c: _dot(k_ref[0, c * chunk:(c + 1) * chunk, :], qst)

    qf = qt.astype(F32).reshape(N_Q_HEADS, HEAD_DIM, tq)
    qn = jnp.sqrt(jnp.sum(qf * qf, axis=1))
    accs = []
    for kv in range(N_KV_HEADS):
        bound = qn[kv * Q_PER_KV:(kv + 1) * Q_PER_KV] * (kn_ref[kv][:, :1] * BOUND_SLACK)
        m = jnp.concatenate([bound[j:j + 1] for j in range(Q_PER_KV)], axis=1)
        scores = scores_fn(q_operand(kv), FAST_CHUNK)
        n_chunks = s // FAST_CHUNK
        acc = None
        sc_next = scores(0)
        for c in range(n_chunks):
            sc = sc_next
            if c + 1 < n_chunks:
                sc_next = scores(c + 1)
            p = jnp.exp2(sc - m).astype(BF16)
            pv = _dot(vta_ref[kv, :, c * FAST_CHUNK:(c + 1) * FAST_CHUNK], p)
            acc = pv if acc is None else acc + pv
        accs.append(acc)
    finish(accs)
    lmin = jnp.min(jnp.minimum(accs[0][HEAD_DIM:HEAD_DIM + 1], accs[1][HEAD_DIM:HEAD_DIM + 1]))

    @pl.when(jnp.logical_not(lmin > L_MIN_OK))
    def _():
        accs = []
        for kv in range(N_KV_HEADS):
            scores = scores_fn(q_operand(kv), KEY_CHUNK)
            n_chunks = s // KEY_CHUNK
            m = None
            acc = None
            sc_next = scores(0)
            for c in range(n_chunks):
                sc = sc_next
                if c + 1 < n_chunks:
                    sc_next = scores(c + 1)
                mc = jnp.max(sc, axis=0, keepdims=True)
                m_new = mc if m is None else jnp.maximum(m, mc)
                p = jnp.exp2(sc - m_new).astype(BF16)
                pv = _dot(vta_ref[kv, :, c * KEY_CHUNK:(c + 1) * KEY_CHUNK], p)
                acc = pv if acc is None else acc * jnp.exp2(m - m_new) + pv
                m = m_new
            accs.append(acc)
        finish(accs)


def _attention(qt, k, vt, gcol):
    b, _, s = qt.shape
    return pl.pallas_call(
        _attn_kernel,
        grid=(b, s // Q_TILE),
        in_specs=[
            pl.BlockSpec((1, D_ATTN, Q_TILE), lambda i, j: (i, 0, j)),
            pl.BlockSpec((1, s, D_KV), lambda i, j: (i, 0, 0)),
            pl.BlockSpec((1, D_KV, s), lambda i, j: (i, 0, 0)),
            pl.BlockSpec((D_ATTN, 1), lambda i, j: (0, 0)),
        ],
        out_specs=pl.BlockSpec((1, Q_TILE, D_ATTN), lambda i, j: (i, j, 0)),
        out_shape=jax.ShapeDtypeStruct((b, s, D_ATTN), BF16),
        scratch_shapes=[pltpu.VMEM((N_KV_HEADS, PV_ROWS, s), BF16),
                        pltpu.VMEM((N_KV_HEADS, 1, LANES), F32)],
        compiler_params=pltpu.CompilerParams(
            dimension_semantics=("arbitrary", "arbitrary"), vmem_limit_bytes=VMEM_LIMIT),
        name="attention",
    )(qt, k, vt, gcol)


def _shift_seg_down(x):
    row = lax.broadcasted_iota(jnp.int32, x.shape, 0)
    return jnp.where(row == 0, 0.0, pltpu.roll(x, 1, 0))


def _shift_seg_up(x):
    row = lax.broadcasted_iota(jnp.int32, x.shape, 0)
    return jnp.where(row == SUBLANES - 1, 0.0, pltpu.roll(x, SUBLANES - 1, 0))


def _sigmoid(x):
    return 0.5 * jnp.tanh(0.5 * x) + 0.5


def _gelu_tanh(x):
    k = 0.7978845608028654
    hx = 0.5 * x
    t = jnp.tanh(x * (k + (k * 0.044715) * (x * x)))
    return hx * t + hx


def _lru_kernel(xr_ref, cw_ref, cb_ref, wg_ref, bg_ref, lam_ref, y_ref,
                xh_ref, h0_ref, p0_ref, h1_ref, p1_ref, ystage_ref):
    t_len = xr_ref.shape[1]
    lanes = xr_ref.shape[3]
    n_chunks = t_len // GATE_CHUNK

    cw = cw_ref[...] * 0.5
    cb = cb_ref[...] * 0.5

    halo_lo = [_shift_seg_down(xr_ref[0, t_len - 2])[None], _shift_seg_down(xr_ref[0, t_len - 1])[None]]
    halo_hi = [_shift_seg_up(xr_ref[0, 0])[None]]

    def shifted(t0, n, k):
        lo = t0 + k - CONV_PAD_LEFT
        pieces = halo_lo[lo + CONV_PAD_LEFT:] if lo < 0 else []
        pieces = pieces + [xr_ref[0, max(lo, 0):min(lo + n, t_len)]]
        if lo + n > t_len:
            pieces = pieces + halo_hi[:lo + n - t_len]
        return pieces[0] if len(pieces) == 1 else jnp.concatenate(pieces, axis=0)

    for t0 in range(0, t_len, GATE_CHUNK):
        xh = cb[None]
        for kk in range(CONV_W):
            xh = xh + shifted(t0, GATE_CHUNK, kk) * cw[kk:kk + 1][None]
        xh_ref[t0:t0 + GATE_CHUNK] = xh
    lam = lam_ref[0]
    hc = (0.5 * LRU_C * 1.4426950408889634) * (
        jnp.minimum(lam, 0.0) - jnp.log1p(jnp.exp(-jnp.abs(lam))))
    wg = wg_ref[0]
    bg_half = bg_ref[0] * 0.5

    def gates(t0, e):
        xh = xh_ref[t0:t0 + SCAN_PART].reshape(SCAN_PART * SUBLANES, lanes)
        cols = slice(2 * e * lanes, 2 * (e + 1) * lanes)
        th = jnp.tanh(_dot(xh.astype(BF16), wg[:, cols]) + bg_half[:, cols])
        tr = th[:, :lanes]
        ti = th[:, lanes:]
        a = jnp.exp2(tr * hc[e:e + 1] + hc[e:e + 1])
        y = 1.0 - a * a
        mult = y * lax.rsqrt(jnp.maximum(y, 1e-30))
        u = (ti * xh + xh) * mult
        return a, u

    zero = jnp.zeros((SUBLANES, lanes), F32)
    one = jnp.ones((SUBLANES, lanes), F32)
    e0, q0, e1, q1 = zero, one, zero, one
    for part in range(t_len // SCAN_PART):
        t0 = part * SCAN_PART
        a, u = gates(t0, 0)
        for t in range(SCAN_PART):
            rows = slice(t * SUBLANES, (t + 1) * SUBLANES)
            e0 = a[rows] * e0 + u[rows]
            q0 = a[rows] * q0
            h0_ref[t0 + t] = e0
            p0_ref[t0 + t] = q0
        t1 = t_len - (part + 1) * SCAN_PART
        a, u = gates(t1, 1)
        for t in reversed(range(SCAN_PART)):
            rows = slice(t * SUBLANES, (t + 1) * SUBLANES)
            e1 = a[rows] * e1 + u[rows]
            q1 = a[rows] * q1
            h1_ref[t1 + t] = e1
            p1_ref[t1 + t] = q1

    c0 = zero
    c1 = zero
    for _ in range(N_SEG - 1):
        c0 = _shift_seg_down(e0 + q0 * c0)
        c1 = _shift_seg_up(e1 + q1 * c1)

    def out_body(i, carry):
        t0 = pl.multiple_of(i * GATE_CHUNK, GATE_CHUNK)
        sl = pl.ds(t0, GATE_CHUNK)
        h = (h0_ref[sl] + p0_ref[sl] * c0[None]) + (h1_ref[sl] + p1_ref[sl] * c1[None])
        ystage_ref[...] = h.reshape(GATE_CHUNK * SUBLANES, lanes)
        for seg in range(N_SEG):
            y_ref[0, pl.ds(seg * t_len + t0, GATE_CHUNK)] = (
                ystage_ref[pl.ds(seg, GATE_CHUNK, stride=SUBLANES)])
        return carry

    lax.fori_loop(0, n_chunks, out_body, 0)


def _lru(xr_il, conv_w, conv_b, wg, bg, lam):
    b, t_len, n_seg, d = xr_il.shape
    n_groups = d // LANES
    slab = pltpu.VMEM((t_len, n_seg, LANES), F32)
    return pl.pallas_call(
        _lru_kernel,
        grid=(b, n_groups),
        in_specs=[
            pl.BlockSpec((1, t_len, n_seg, LANES), lambda i, j: (i, 0, 0, j)),
            pl.BlockSpec((CONV_W, LANES), lambda i, j: (0, j)),
            pl.BlockSpec((1, LANES), lambda i, j: (0, j)),
            pl.BlockSpec((1, LANES, 4 * LANES), lambda i, j: (j, 0, 0)),
            pl.BlockSpec((1, 1, 4 * LANES), lambda i, j: (j, 0, 0)),
            pl.BlockSpec((1, 2, LANES), lambda i, j: (j, 0, 0)),
        ],
        out_specs=pl.BlockSpec((1, t_len * n_seg, LANES), lambda i, j: (i, 0, j)),
        out_shape=jax.ShapeDtypeStruct((b, t_len * n_seg, d), F32),
        scratch_shapes=[slab] * 5 + [
            pltpu.VMEM((GATE_CHUNK * n_seg, LANES), F32)],
        compiler_params=pltpu.CompilerParams(
            dimension_semantics=("arbitrary", "arbitrary"), vmem_limit_bytes=VMEM_LIMIT),
        name="rglru",
    )(xr_il, conv_w, conv_b, wg, bg, lam)


def _tail_kernel(x_ref, ya_ref, hl_ref, xg_ref, p_ref, glru_ref, wout_ref, gmlp_ref, wup_ref,
                 wdown_ref, gple_ref, wgate_ref, wproj_ref, gfin_ref, o_ref):
    x = x_ref[0]
    yl = hl_ref[0] * _gelu_tanh(xg_ref[0])
    yl_n = (yl * _rms(yl, -1) * glru_ref[...]).astype(BF16)
    h = x + _dot(ya_ref[0], wout_ref[:D_ATTN]) + _dot(yl_n, wout_ref[D_ATTN:])
    hn = (h * _rms(h, -1) * gmlp_ref[...]).astype(BF16)
    m = _dot(hn, wup_ref[...])
    act = jnp.square(jnp.maximum(m, 0.0)).astype(BF16)
    h = h + _dot(act, wdown_ref[...])
    hn = (h * _rms(h, -1) * gple_ref[...]).astype(BF16)
    gate = _sigmoid(_dot(hn, wgate_ref[...]))
    h = h + gate * _dot(p_ref[0].astype(BF16), wproj_ref[...])
    o_ref[0] = h * _rms(h, -1) * gfin_ref[...]


def _tail(x, ya, hl, xg_il, p, glru, wout, gmlp, wup, wdown, gple, wgate, wproj, gfin):
    b, s, d = x.shape
    t = s // N_SEG
    const = lambda *_: (0, 0)

    def resident(arr):
        return pl.BlockSpec(arr.shape, const, pipeline_mode=pl.Buffered(1))

    return pl.pallas_call(
        _tail_kernel,
        grid=(b, N_SEG),
        in_specs=[
            pl.BlockSpec((1, t, d), lambda i, j: (i, j, 0)),
            pl.BlockSpec((1, t, D_ATTN), lambda i, j: (i, j, 0)),
            pl.BlockSpec((1, t, D_LRU), lambda i, j: (i, j, 0)),
            pl.BlockSpec((1, t, D_LRU), lambda i, j: (i, 0, j)),
            pl.BlockSpec((1, t, D_PLE), lambda i, j: (i, j, 0)),
            resident(glru), resident(wout), resident(gmlp), resident(wup), resident(wdown),
            resident(gple), resident(wgate), resident(wproj), resident(gfin),
        ],
        out_specs=pl.BlockSpec((1, t, d), lambda i, j: (i, j, 0)),
        out_shape=jax.ShapeDtypeStruct((b, s, d), F32),
        compiler_params=pltpu.CompilerParams(
            dimension_semantics=("arbitrary", "arbitrary"), vmem_limit_bytes=VMEM_LIMIT),
        name="tail",
    )(x, ya, hl, xg_il, p, glru, wout, gmlp, wup, wdown, gple, wgate, wproj, gfin)


def _rope_tables_t(seq_len):
    pos = jnp.arange(seq_len)
    row = (pos // GRID_W).astype(F32)
    col = (pos % GRID_W).astype(F32)
    inv_freq = ROPE_THETA ** (-jnp.arange(N_FREQ, dtype=F32) / N_FREQ)
    ang_r = row[None, :] * inv_freq[:, None]
    ang_c = col[None, :] * inv_freq[:, None]
    cos_t = jnp.concatenate([jnp.cos(ang_r)] * 2 + [jnp.cos(ang_c)] * 2, axis=0)
    sin_t = jnp.concatenate(
        [-jnp.sin(ang_r), jnp.sin(ang_r), -jnp.sin(ang_c), jnp.sin(ang_c)], axis=0)
    return cos_t, sin_t


def _block_diag_pairs(w):
    n_dir, n_blk, bw, _ = w.shape
    per = LANES // bw
    w5 = w.reshape(n_dir, n_blk // per, per, bw, bw)
    eye = jnp.eye(per, dtype=w.dtype)
    bd = jnp.einsum("egikj,im->egikmj", w5, eye)
    return bd.reshape(n_dir, n_blk // per, LANES, LANES)


def kernel(x, p, mix_norm, w_in, q_norm, k_norm, conv_w, conv_b, lru_wa, lru_ba, lru_wx, lru_bx,
           lru_lambda, attn_out_norm, lru_out_norm, w_out, mlp_norm, w_up, w_down, ple_norm,
           w_ple_gate, w_ple_proj, final_norm):
    b, s, d = x.shape
    assert w_in.shape[0] == 1, "single-layer trunk: the final norm is fused into the layer tail"
    t_len = s // N_SEG
    n_groups = D_LRU // LANES
    cos_t, sin_t = _rope_tables_t(s)
    row2 = lambda v: v.reshape(1, -1)
    col2 = lambda v: v.reshape(-1, 1)
    h = x
    for l in range(1):
        wqkv_t = w_in[l][:, :D_ATTN + 2 * D_KV].T.astype(BF16)
        wrg = w_in[l][:, D_ATTN + 2 * D_KV:].astype(BF16)
        qt, k, vt, xr, xg = _in_proj(h, row2(mix_norm[l]), wqkv_t, wrg,
                                     col2(q_norm[l]), col2(k_norm[l]), cos_t, sin_t)
        ya = _attention(qt, k, vt, col2(attn_out_norm[l]))

        wa_bd = _block_diag_pairs(lru_wa[l])
        wx_bd = _block_diag_pairs(lru_wx[l])
        wg = jnp.concatenate([wa_bd[0], wx_bd[0], wa_bd[1], wx_bd[1]], axis=-1).astype(BF16)
        grp = lambda v: v.reshape(n_groups, 1, LANES)
        bg = jnp.concatenate([grp(lru_ba[l][0]), grp(lru_bx[l][0]),
                              grp(lru_ba[l][1]), grp(lru_bx[l][1])], axis=-1)
        lam = lru_lambda[l].reshape(2, n_groups, LANES).transpose(1, 0, 2)
        hl = _lru(xr.reshape(b, t_len, N_SEG, D_LRU), conv_w[l], row2(conv_b[l]), wg, bg, lam)

        h = _tail(h, ya, hl, xg, p[l], row2(lru_out_norm[l]), w_out[l].astype(BF16),
                  row2(mlp_norm[l]), w_up[l].astype(BF16), w_down[l].astype(BF16),
                  row2(ple_norm[l]), w_ple_gate[l].astype(BF16), w_ple_proj[l].astype(BF16),
                  row2(final_norm))
    return h
```

```python
import jax
import jax.numpy as jnp
import numpy as np
from jax import lax
from jax.experimental import pallas as pl
from jax.experimental.pallas import tpu as pltpu

D_MODEL = 1024
GRID_W = 64
HEAD_DIM = 64
D_ATTN = 512
N_Q_HEADS = 8
N_KV_HEADS = 2
Q_PER_KV = 4
D_KV = 128
ROPE_THETA = 10000.0
N_FREQ = 16
D_LRU = 512
LRU_BLOCK_W = 64
LRU_C = 8.0
CONV_W = 4
CONV_PAD_LEFT = 2
D_FF = 4096
D_PLE = 256
NORM_EPS = 1e-6

SUBLANES = 8
LANES = 128

N_SEG = SUBLANES
Q_TILE = 256
KEY_CHUNK = 256
FAST_CHUNK = 256
PV_ROWS = 80
BOUND_SLACK = 1.001
L_MIN_OK = 2.0 ** -80
GATE_CHUNK = 64
SCAN_PART = 128
VMEM_LIMIT = 56 * 1024 * 1024
Q_SCALE = HEAD_DIM ** -0.5 * 1.4426950408889634

F32 = jnp.float32
BF16 = jnp.bfloat16


def _rms(x, axis):
    return lax.rsqrt(jnp.mean(x * x, axis=axis, keepdims=True) + NORM_EPS)


def _dot(a, b):
    return jnp.dot(a, b, preferred_element_type=F32)


def _as_column(row):
    n = row.shape[1]
    ri = lax.broadcasted_iota(jnp.int32, (n, n), 0)
    ci = lax.broadcasted_iota(jnp.int32, (n, n), 1)
    return jnp.sum(jnp.where(ri == ci, row, 0.0), axis=1, keepdims=True)


def _dot_nt(a, b):
    return lax.dot_general(a, b, (((1,), (1,)), ((), ())), preferred_element_type=F32)


def _norm_rope_t(xt, gcol, cos_t, sin_t, n_heads):
    t = xt.shape[1]
    x3 = xt.reshape(n_heads, HEAD_DIM, t)
    xn = x3 * _rms(x3, 1) * gcol[None]
    x5 = xn.reshape(n_heads * 2, 2, N_FREQ, t)
    xs = jnp.concatenate([x5[:, 1:2], x5[:, 0:1]], axis=1).reshape(n_heads, HEAD_DIM, t)
    out = xn * cos_t[None] + xs * sin_t[None]
    return out.reshape(n_heads * HEAD_DIM, t)


def _in_proj_kernel(x_ref, gmix_ref, wqkv_t_ref, wrg_ref, gq_ref, gk_ref, cos_ref, sin_ref,
                    qt_ref, k_ref, vt_ref, xr_ref, xg_ref):
    x = x_ref[0]
    hn = (x * _rms(x, -1) * gmix_ref[...]).astype(BF16)
    zt = _dot_nt(wqkv_t_ref[...], hn)
    zr = _dot(hn, wrg_ref[...])
    cos_t = cos_ref[...]
    sin_t = sin_ref[...]
    qt = _norm_rope_t(zt[:D_ATTN], _as_column(gq_ref[...]) * Q_SCALE, cos_t, sin_t, N_Q_HEADS)
    kt = _norm_rope_t(zt[D_ATTN:D_ATTN + D_KV], _as_column(gk_ref[...]), cos_t, sin_t, N_KV_HEADS)
    qt_ref[0] = qt.astype(BF16)
    k_ref[0] = kt.T.astype(BF16)
    vt_ref[0] = zt[D_ATTN + D_KV:].astype(BF16)
    xr_ref[0] = zr[:, :D_LRU]
    xg_ref[0] = zr[:, D_LRU:]


def _in_proj(x, gmix, wqkv_t, wrg, gq, gk, cos_t, sin_t):
    b, s, d = x.shape
    t = s // N_SEG
    const = lambda *_: (0, 0)
    return pl.pallas_call(
        _in_proj_kernel,
        grid=(b, N_SEG),
        in_specs=[
            pl.BlockSpec((1, t, d), lambda i, j: (i, j, 0)),
            pl.BlockSpec((1, d), const),
            pl.BlockSpec(wqkv_t.shape, const),
            pl.BlockSpec(wrg.shape, const),
            pl.BlockSpec((1, HEAD_DIM), const),
            pl.BlockSpec((1, HEAD_DIM), const),
            pl.BlockSpec((HEAD_DIM, t), lambda i, j: (0, j)),
            pl.BlockSpec((HEAD_DIM, t), lambda i, j: (0, j)),
        ],
        out_specs=[
            pl.BlockSpec((1, D_ATTN, t), lambda i, j: (i, 0, j)),
            pl.BlockSpec((1, t, D_KV), lambda i, j: (i, j, 0)),
            pl.BlockSpec((1, D_KV, t), lambda i, j: (i, 0, j)),
            pl.BlockSpec((1, t, D_LRU), lambda i, j: (i, 0, j)),
            pl.BlockSpec((1, t, D_LRU), lambda i, j: (i, 0, j)),
        ],
        out_shape=[
            jax.ShapeDtypeStruct((b, D_ATTN, s), BF16),
            jax.ShapeDtypeStruct((b, s, D_KV), BF16),
            jax.ShapeDtypeStruct((b, D_KV, s), BF16),
            jax.ShapeDtypeStruct((b, t, N_SEG * D_LRU), F32),
            jax.ShapeDtypeStruct((b, t, N_SEG * D_LRU), F32),
        ],
        compiler_params=pltpu.CompilerParams(
            dimension_semantics=("arbitrary", "arbitrary"), vmem_limit_bytes=VMEM_LIMIT),
        name="in_proj",
    )(x, gmix, wqkv_t, wrg, gq, gk, cos_t, sin_t)


def _attn_kernel(qt_ref, k_ref, vt_ref, grow_ref, y_ref, vta_ref, kn_ref):
    s = k_ref.shape[1]
    tq = qt_ref.shape[2]

    @pl.when(pl.program_id(1) == 0)
    def _():
        vt = vt_ref[0]
        row = lax.broadcasted_iota(jnp.int32, (PV_ROWS - HEAD_DIM, s), 0)
        ones_pad = jnp.where(row == 0, 1.0, 0.0).astype(BF16)
        kf = k_ref[0].astype(F32)
        lane = lax.broadcasted_iota(jnp.int32, kf.shape, 1)
        ksq = kf * kf
        for kv in range(N_KV_HEADS):
            vta_ref[kv] = jnp.concatenate([vt[kv * HEAD_DIM:(kv + 1) * HEAD_DIM], ones_pad], axis=0)
            in_head = (lane >= kv * HEAD_DIM) & (lane < (kv + 1) * HEAD_DIM)
            n2 = jnp.sum(jnp.where(in_head, ksq, 0.0), axis=1, keepdims=True)
            kn_ref[kv] = jnp.broadcast_to(jnp.sqrt(jnp.max(n2, axis=0, keepdims=True)), (1, LANES))

    qt = qt_ref[0]
    zeros = jnp.zeros((HEAD_DIM, tq), BF16)

    def q_operand(kv):
        cols = []
        for j in range(Q_PER_KV):
            h = kv * Q_PER_KV + j
            qh = qt[h * HEAD_DIM:(h + 1) * HEAD_DIM]
            cols.append(jnp.concatenate([qh, zeros] if kv == 0 else [zeros, qh], axis=0))
        return jnp.concatenate(cols, axis=1)

    def finish(accs):
        outs = []
        for acc in accs:
            o = acc[:HEAD_DIM] / acc[HEAD_DIM:HEAD_DIM + 1]
            for j in range(Q_PER_KV):
                outs.append(o[:, j * tq:(j + 1) * tq])
        ot = jnp.concatenate(outs, axis=0)
        y_ref[0] = ((ot * _rms(ot, 0)).T * grow_ref[...]).astype(BF16)

    def scores_fn(qst, chunk):
        return lambda c: _dot(k_ref[0, c * chunk:(c + 1) * chunk, :], qst)

    qf = qt.astype(F32).reshape(N_Q_HEADS, HEAD_DIM, tq)
    qn = jnp.sqrt(jnp.sum(qf * qf, axis=1))
    accs = []
    for kv in range(N_KV_HEADS):
        bound = qn[kv * Q_PER_KV:(kv + 1) * Q_PER_KV] * (kn_ref[kv][:, :1] * BOUND_SLACK)
        m = jnp.concatenate([bound[j:j + 1] for j in range(Q_PER_KV)], axis=1)
        scores = scores_fn(q_operand(kv), FAST_CHUNK)
        n_chunks = s // FAST_CHUNK
        acc = None
        sc_next = scores(0)
        for c in range(n_chunks):
            sc = sc_next
            if c + 1 < n_chunks:
                sc_next = scores(c + 1)
            p = jnp.exp2(sc - m).astype(BF16)
            pv = _dot(vta_ref[kv, :, c * FAST_CHUNK:(c + 1) * FAST_CHUNK], p)
            acc = pv if acc is None else acc + pv
        accs.append(acc)
    finish(accs)
    lmin = jnp.min(jnp.minimum(accs[0][HEAD_DIM:HEAD_DIM + 1], accs[1][HEAD_DIM:HEAD_DIM + 1]))

    @pl.when(jnp.logical_not(lmin > L_MIN_OK))
    def _():
        accs = []
        for kv in range(N_KV_HEADS):
            scores = scores_fn(q_operand(kv), KEY_CHUNK)
            n_chunks = s // KEY_CHUNK
            m = None
            acc = None
            sc_next = scores(0)
            for c in range(n_chunks):
                sc = sc_next
                if c + 1 < n_chunks:
                    sc_next = scores(c + 1)
                mc = jnp.max(sc, axis=0, keepdims=True)
                m_new = mc if m is None else jnp.maximum(m, mc)
                p = jnp.exp2(sc - m_new).astype(BF16)
                pv = _dot(vta_ref[kv, :, c * KEY_CHUNK:(c + 1) * KEY_CHUNK], p)
                acc = pv if acc is None else acc * jnp.exp2(m - m_new) + pv
                m = m_new
            accs.append(acc)
        finish(accs)


def _attention(qt, k, vt, gcol):
    b, _, s = qt.shape
    return pl.pallas_call(
        _attn_kernel,
        grid=(b, s // Q_TILE),
        in_specs=[
            pl.BlockSpec((1, D_ATTN, Q_TILE), lambda i, j: (i, 0, j)),
            pl.BlockSpec((1, s, D_KV), lambda i, j: (i, 0, 0)),
            pl.BlockSpec((1, D_KV, s), lambda i, j: (i, 0, 0)),
            pl.BlockSpec((1, D_ATTN), lambda i, j: (0, 0)),
        ],
        out_specs=pl.BlockSpec((1, Q_TILE, D_ATTN), lambda i, j: (i, j, 0)),
        out_shape=jax.ShapeDtypeStruct((b, s, D_ATTN), BF16),
        scratch_shapes=[pltpu.VMEM((N_KV_HEADS, PV_ROWS, s), BF16),
                        pltpu.VMEM((N_KV_HEADS, 1, LANES), F32)],
        compiler_params=pltpu.CompilerParams(
            dimension_semantics=("arbitrary", "arbitrary"), vmem_limit_bytes=VMEM_LIMIT),
        name="attention",
    )(qt, k, vt, gcol)


def _shift_seg_down(x):
    row = lax.broadcasted_iota(jnp.int32, x.shape, 0)
    return jnp.where(row == 0, 0.0, pltpu.roll(x, 1, 0))


def _shift_seg_up(x):
    row = lax.broadcasted_iota(jnp.int32, x.shape, 0)
    return jnp.where(row == SUBLANES - 1, 0.0, pltpu.roll(x, SUBLANES - 1, 0))


def _sigmoid(x):
    return 0.5 * jnp.tanh(0.5 * x) + 0.5


def _gelu_tanh(x):
    k = 0.7978845608028654
    hx = 0.5 * x
    t = jnp.tanh(x * (k + (k * 0.044715) * (x * x)))
    return hx * t + hx


def _lru_kernel(xr_ref, cw_ref, cb_ref, wa_ref, wx_ref, ba_ref, bx_ref, lam_ref, y_ref,
                xh_ref, h0_ref, p0_ref, h1_ref, p1_ref, ystage_ref):
    t_len = xr_ref.shape[1]
    lanes = xr_ref.shape[3]
    n_chunks = t_len // GATE_CHUNK

    cw = cw_ref[0] * 0.5
    cb = cb_ref[...] * 0.5

    halo_lo = [_shift_seg_down(xr_ref[0, t_len - 2])[None], _shift_seg_down(xr_ref[0, t_len - 1])[None]]
    halo_hi = [_shift_seg_up(xr_ref[0, 0])[None]]

    def shifted(t0, n, k):
        lo = t0 + k - CONV_PAD_LEFT
        pieces = halo_lo[lo + CONV_PAD_LEFT:] if lo < 0 else []
        pieces = pieces + [xr_ref[0, max(lo, 0):min(lo + n, t_len)]]
        if lo + n > t_len:
            pieces = pieces + halo_hi[:lo + n - t_len]
        return pieces[0] if len(pieces) == 1 else jnp.concatenate(pieces, axis=0)

    for t0 in range(0, t_len, GATE_CHUNK):
        xh = cb[None]
        for kk in range(CONV_W):
            xh = xh + shifted(t0, GATE_CHUNK, kk) * cw[kk:kk + 1][None]
        xh_ref[t0:t0 + GATE_CHUNK] = xh
    lam = lam_ref[0]
    hc = (0.5 * LRU_C * 1.4426950408889634) * (
        jnp.minimum(lam, 0.0) - jnp.log1p(jnp.exp(-jnp.abs(lam))))

    zblk = jnp.zeros((LRU_BLOCK_W, LRU_BLOCK_W), F32)
    wg, bg_half = [], []
    for e in range(2):
        rows = []
        for i in range(lanes // LRU_BLOCK_W):
            blocks = []
            for w_ref in (wa_ref, wx_ref):
                blocks += [w_ref[0, e, i] if c == i else zblk for c in range(lanes // LRU_BLOCK_W)]
            rows.append(jnp.concatenate(blocks, axis=1))
        wg.append(jnp.concatenate(rows, axis=0).astype(BF16))
        bg_half.append(0.5 * jnp.concatenate([ba_ref[0, e:e + 1], bx_ref[0, e:e + 1]], axis=1))

    def gates(t0, e):
        xh = xh_ref[t0:t0 + SCAN_PART].reshape(SCAN_PART * SUBLANES, lanes)
        th = jnp.tanh(_dot(xh.astype(BF16), wg[e]) + bg_half[e])
        tr = th[:, :lanes]
        ti = th[:, lanes:]
        a = jnp.exp2(tr * hc[e:e + 1] + hc[e:e + 1])
        y = 1.0 - a * a
        mult = y * lax.rsqrt(jnp.maximum(y, 1e-30))
        u = (ti * xh + xh) * mult
        return a, u

    zero = jnp.zeros((SUBLANES, lanes), F32)
    one = jnp.ones((SUBLANES, lanes), F32)
    e0, q0, e1, q1 = zero, one, zero, one
    for part in range(t_len // SCAN_PART):
        t0 = part * SCAN_PART
        a, u = gates(t0, 0)
        for t in range(SCAN_PART):
            rows = slice(t * SUBLANES, (t + 1) * SUBLANES)
            e0 = a[rows] * e0 + u[rows]
            q0 = a[rows] * q0
            h0_ref[t0 + t] = e0
            p0_ref[t0 + t] = q0
        t1 = t_len - (part + 1) * SCAN_PART
        a, u = gates(t1, 1)
        for t in reversed(range(SCAN_PART)):
            rows = slice(t * SUBLANES, (t + 1) * SUBLANES)
            e1 = a[rows] * e1 + u[rows]
            q1 = a[rows] * q1
            h1_ref[t1 + t] = e1
            p1_ref[t1 + t] = q1

    c0 = zero
    c1 = zero
    for _ in range(N_SEG - 1):
        c0 = _shift_seg_down(e0 + q0 * c0)
        c1 = _shift_seg_up(e1 + q1 * c1)

    def out_body(i, carry):
        t0 = pl.multiple_of(i * GATE_CHUNK, GATE_CHUNK)
        sl = pl.ds(t0, GATE_CHUNK)
        h = (h0_ref[sl] + p0_ref[sl] * c0[None]) + (h1_ref[sl] + p1_ref[sl] * c1[None])
        ystage_ref[...] = h.reshape(GATE_CHUNK * SUBLANES, lanes)
        for seg in range(N_SEG):
            y_ref[0, pl.ds(seg * t_len + t0, GATE_CHUNK)] = (
                ystage_ref[pl.ds(seg, GATE_CHUNK, stride=SUBLANES)])
        return carry

    lax.fori_loop(0, n_chunks, out_body, 0)


def _lru(xr_il, conv_w, conv_b, wa, wx, ba, bx, lam):
    b, t_len, n_seg, d = xr_il.shape
    n_groups = d // LANES
    blocks_per_group = LANES // LRU_BLOCK_W
    slab = pltpu.VMEM((t_len, n_seg, LANES), F32)
    w_spec = pl.BlockSpec((1, 2, blocks_per_group, LRU_BLOCK_W, LRU_BLOCK_W),
                          lambda i, j: (0, 0, j, 0, 0))
    lane_group = pl.BlockSpec((1, 2, LANES), lambda i, j: (0, 0, j))
    return pl.pallas_call(
        _lru_kernel,
        grid=(b, n_groups),
        in_specs=[
            pl.BlockSpec((1, t_len, n_seg, LANES), lambda i, j: (i, 0, 0, j)),
            pl.BlockSpec((1, CONV_W, LANES), lambda i, j: (0, 0, j)),
            pl.BlockSpec((1, LANES), lambda i, j: (0, j)),
            w_spec, w_spec, lane_group, lane_group, lane_group,
        ],
        out_specs=pl.BlockSpec((1, t_len * n_seg, LANES), lambda i, j: (i, 0, j)),
        out_shape=jax.ShapeDtypeStruct((b, t_len * n_seg, d), F32),
        scratch_shapes=[slab] * 5 + [
            pltpu.VMEM((GATE_CHUNK * n_seg, LANES), F32)],
        compiler_params=pltpu.CompilerParams(
            dimension_semantics=("arbitrary", "arbitrary"), vmem_limit_bytes=VMEM_LIMIT),
        name="rglru",
    )(xr_il, conv_w, conv_b, wa, wx, ba, bx, lam)


def _tail_kernel(x_ref, ya_ref, hl_ref, xg_ref, p_ref, glru_ref, wout_ref, gmlp_ref, wup_ref,
                 wdown_ref, gple_ref, wgate_ref, wproj_ref, gfin_ref, o_ref):
    x = x_ref[0]
    yl = hl_ref[0] * _gelu_tanh(xg_ref[0])
    yl_n = (yl * _rms(yl, -1) * glru_ref[...]).astype(BF16)
    h = x + _dot(ya_ref[0], wout_ref[:D_ATTN]) + _dot(yl_n, wout_ref[D_ATTN:])
    hn = (h * _rms(h, -1) * gmlp_ref[...]).astype(BF16)
    m = _dot(hn, wup_ref[...])
    act = jnp.square(jnp.maximum(m, 0.0)).astype(BF16)
    h = h + _dot(act, wdown_ref[...])
    hn = (h * _rms(h, -1) * gple_ref[...]).astype(BF16)
    gate = _sigmoid(_dot(hn, wgate_ref[...]))
    h = h + gate * _dot(p_ref[0, 0].astype(BF16), wproj_ref[...])
    o_ref[0] = h * _rms(h, -1) * gfin_ref[...]


def _tail(x, ya, hl, xg_il, p, glru, wout, gmlp, wup, wdown, gple, wgate, wproj, gfin):
    b, s, d = x.shape
    t = s // N_SEG
    const = lambda *_: (0, 0)

    def resident(arr):
        return pl.BlockSpec(arr.shape, const, pipeline_mode=pl.Buffered(1))

    return pl.pallas_call(
        _tail_kernel,
        grid=(b, N_SEG),
        in_specs=[
            pl.BlockSpec((1, t, d), lambda i, j: (i, j, 0)),
            pl.BlockSpec((1, t, D_ATTN), lambda i, j: (i, j, 0)),
            pl.BlockSpec((1, t, D_LRU), lambda i, j: (i, j, 0)),
            pl.BlockSpec((1, t, D_LRU), lambda i, j: (i, 0, j)),
            pl.BlockSpec((1, 1, t, D_PLE), lambda i, j: (0, i, j, 0)),
            resident(glru), resident(wout), resident(gmlp), resident(wup), resident(wdown),
            resident(gple), resident(wgate), resident(wproj), resident(gfin),
        ],
        out_specs=pl.BlockSpec((1, t, d), lambda i, j: (i, j, 0)),
        out_shape=jax.ShapeDtypeStruct((b, s, d), F32),
        compiler_params=pltpu.CompilerParams(
            dimension_semantics=("arbitrary", "arbitrary"), vmem_limit_bytes=VMEM_LIMIT),
        name="tail",
    )(x, ya, hl, xg_il, p, glru, wout, gmlp, wup, wdown, gple, wgate, wproj, gfin)


def _rope_tables_t(seq_len):
    pos = np.arange(seq_len)
    row = (pos // GRID_W).astype(np.float32)
    col = (pos % GRID_W).astype(np.float32)
    inv_freq = np.float32(ROPE_THETA) ** (-np.arange(N_FREQ, dtype=np.float32) / np.float32(N_FREQ))
    ang_r = (row[None, :] * inv_freq[:, None]).astype(np.float32)
    ang_c = (col[None, :] * inv_freq[:, None]).astype(np.float32)
    cos_t = np.concatenate([np.cos(ang_r)] * 2 + [np.cos(ang_c)] * 2, axis=0)
    sin_t = np.concatenate([-np.sin(ang_r), np.sin(ang_r), -np.sin(ang_c), np.sin(ang_c)], axis=0)
    return jnp.asarray(cos_t, F32), jnp.asarray(sin_t, F32)


def kernel(x, p, mix_norm, w_in, q_norm, k_norm, conv_w, conv_b, lru_wa, lru_ba, lru_wx, lru_bx,
           lru_lambda, attn_out_norm, lru_out_norm, w_out, mlp_norm, w_up, w_down, ple_norm,
           w_ple_gate, w_ple_proj, final_norm):
    b, s, d = x.shape
    assert w_in.shape[0] == 1, "single-layer trunk: the final norm is fused into the layer tail"
    cos_t, sin_t = _rope_tables_t(s)
    wqkv_t = w_in[0, :, :D_ATTN + 2 * D_KV].T.astype(BF16)
    wrg = w_in[0, :, D_ATTN + 2 * D_KV:].astype(BF16)
    qt, k, vt, xr, xg = _in_proj(x, mix_norm, wqkv_t, wrg, q_norm, k_norm, cos_t, sin_t)
    ya = _attention(qt, k, vt, attn_out_norm)
    hl = _lru(xr.reshape(b, s // N_SEG, N_SEG, D_LRU), conv_w, conv_b, lru_wa, lru_wx, lru_ba, lru_bx,
              lru_lambda)
    return _tail(x, ya, hl, xg, p, lru_out_norm, w_out[0].astype(BF16), mlp_norm,
                 w_up[0].astype(BF16), w_down[0].astype(BF16), ple_norm,
                 w_ple_gate[0].astype(BF16), w_ple_proj[0].astype(BF16), final_norm.reshape(1, -1))
```

```python
import jax
import jax.numpy as jnp
import numpy as np
from jax import lax
from jax.experimental import pallas as pl
from jax.experimental.pallas import tpu as pltpu

D_MODEL = 1024
GRID_W = 64
HEAD_DIM = 64
D_ATTN = 512
N_Q_HEADS = 8
N_KV_HEADS = 2
Q_PER_KV = 4
D_KV = 128
ROPE_THETA = 10000.0
N_FREQ = 16
D_LRU = 512
LRU_BLOCK_W = 64
LRU_C = 8.0
CONV_W = 4
CONV_PAD_LEFT = 2
D_FF = 4096
D_PLE = 256
NORM_EPS = 1e-6

SUBLANES = 8
LANES = 128

N_SEG = SUBLANES
Q_TILE = 256
KEY_CHUNK = 256
FAST_CHUNK = 256
FAST_HEADS = 2
PV_ROWS = 80
BOUND_SLACK = 1.001
L_MIN_OK = 2.0 ** -80
GATE_CHUNK = 64
SCAN_PART = 128
VMEM_LIMIT = 56 * 1024 * 1024
Q_SCALE = HEAD_DIM ** -0.5 * 1.4426950408889634

F32 = jnp.float32
BF16 = jnp.bfloat16


def _rms(x, axis):
    return lax.rsqrt(jnp.mean(x * x, axis=axis, keepdims=True) + NORM_EPS)


def _dot(a, b):
    return jnp.dot(a, b, preferred_element_type=F32)


def _as_column(row):
    n = row.shape[1]
    ri = lax.broadcasted_iota(jnp.int32, (n, n), 0)
    ci = lax.broadcasted_iota(jnp.int32, (n, n), 1)
    return jnp.sum(jnp.where(ri == ci, row, 0.0), axis=1, keepdims=True)


def _dot_nt(a, b):
    return lax.dot_general(a, b, (((1,), (1,)), ((), ())), preferred_element_type=F32)


def _norm_rope_t(xt, gcol, cos_t, sin_t, n_heads):
    t = xt.shape[1]
    x3 = xt.reshape(n_heads, HEAD_DIM, t)
    xn = x3 * _rms(x3, 1) * gcol[None]
    x5 = xn.reshape(n_heads * 2, 2, N_FREQ, t)
    xs = jnp.concatenate([x5[:, 1:2], x5[:, 0:1]], axis=1).reshape(n_heads, HEAD_DIM, t)
    out = xn * cos_t[None] + xs * sin_t[None]
    return out.reshape(n_heads * HEAD_DIM, t)


def _in_proj_kernel(x_ref, gmix_ref, wqkv_t_ref, wrg_ref, gq_ref, gk_ref, cos_ref, sin_ref,
                    qt_ref, k_ref, vt_ref, xr_ref, xg_ref):
    x = x_ref[0]
    hn = (x * _rms(x, -1) * gmix_ref[...]).astype(BF16)
    zt = _dot_nt(wqkv_t_ref[...], hn)
    zr = _dot(hn, wrg_ref[...])
    cos_t = cos_ref[...]
    sin_t = sin_ref[...]
    qt = _norm_rope_t(zt[:D_ATTN], _as_column(gq_ref[...]) * Q_SCALE, cos_t, sin_t, N_Q_HEADS)
    kt = _norm_rope_t(zt[D_ATTN:D_ATTN + D_KV], _as_column(gk_ref[...]), cos_t, sin_t, N_KV_HEADS)
    qt_ref[0] = qt.astype(BF16)
    k_ref[0] = kt.T.astype(BF16)
    vt_ref[0] = zt[D_ATTN + D_KV:].astype(BF16)
    xr_ref[0] = zr[:, :D_LRU]
    xg_ref[0] = zr[:, D_LRU:]


def _in_proj(x, gmix, wqkv_t, wrg, gq, gk, cos_t, sin_t):
    b, s, d = x.shape
    t = s // N_SEG
    const = lambda *_: (0, 0)
    return pl.pallas_call(
        _in_proj_kernel,
        grid=(b, N_SEG),
        in_specs=[
            pl.BlockSpec((1, t, d), lambda i, j: (i, j, 0)),
            pl.BlockSpec((1, d), const),
            pl.BlockSpec(wqkv_t.shape, const),
            pl.BlockSpec(wrg.shape, const),
            pl.BlockSpec((1, HEAD_DIM), const),
            pl.BlockSpec((1, HEAD_DIM), const),
            pl.BlockSpec((HEAD_DIM, t), lambda i, j: (0, j)),
            pl.BlockSpec((HEAD_DIM, t), lambda i, j: (0, j)),
        ],
        out_specs=[
            pl.BlockSpec((1, D_ATTN, t), lambda i, j: (i, 0, j)),
            pl.BlockSpec((1, t, D_KV), lambda i, j: (i, j, 0)),
            pl.BlockSpec((1, D_KV, t), lambda i, j: (i, 0, j)),
            pl.BlockSpec((1, t, D_LRU), lambda i, j: (i, 0, j)),
            pl.BlockSpec((1, t, D_LRU), lambda i, j: (i, 0, j)),
        ],
        out_shape=[
            jax.ShapeDtypeStruct((b, D_ATTN, s), BF16),
            jax.ShapeDtypeStruct((b, s, D_KV), BF16),
            jax.ShapeDtypeStruct((b, D_KV, s), BF16),
            jax.ShapeDtypeStruct((b, t, N_SEG * D_LRU), F32),
            jax.ShapeDtypeStruct((b, t, N_SEG * D_LRU), F32),
        ],
        compiler_params=pltpu.CompilerParams(
            dimension_semantics=("arbitrary", "arbitrary"), vmem_limit_bytes=VMEM_LIMIT),
        name="in_proj",
    )(x, gmix, wqkv_t, wrg, gq, gk, cos_t, sin_t)


def _attn_kernel(qt_ref, k_ref, vt_ref, grow_ref, y_ref, vta_ref, kn_ref):
    s = k_ref.shape[1]
    tq = qt_ref.shape[2]

    @pl.when(pl.program_id(1) == 0)
    def _():
        vt = vt_ref[0]
        row = lax.broadcasted_iota(jnp.int32, (PV_ROWS - HEAD_DIM, s), 0)
        ones_pad = jnp.where(row == 0, 1.0, 0.0).astype(BF16)
        kf = k_ref[0].astype(F32)
        lane = lax.broadcasted_iota(jnp.int32, kf.shape, 1)
        ksq = kf * kf
        for kv in range(N_KV_HEADS):
            vta_ref[kv] = jnp.concatenate([vt[kv * HEAD_DIM:(kv + 1) * HEAD_DIM], ones_pad], axis=0)
            in_head = (lane >= kv * HEAD_DIM) & (lane < (kv + 1) * HEAD_DIM)
            n2 = jnp.sum(jnp.where(in_head, ksq, 0.0), axis=1, keepdims=True)
            kn_ref[kv] = jnp.broadcast_to(jnp.sqrt(jnp.max(n2, axis=0, keepdims=True)), (1, LANES))

    qt = qt_ref[0]
    zeros = jnp.zeros((HEAD_DIM, tq), BF16)

    def q_operand(kv, heads=range(Q_PER_KV)):
        cols = []
        for j in heads:
            h = kv * Q_PER_KV + j
            qh = qt[h * HEAD_DIM:(h + 1) * HEAD_DIM]
            cols.append(jnp.concatenate([qh, zeros] if kv == 0 else [zeros, qh], axis=0))
        return jnp.concatenate(cols, axis=1)

    def finish(accs):
        outs = []
        for acc in accs:
            o = acc[:HEAD_DIM] / acc[HEAD_DIM:HEAD_DIM + 1]
            for j in range(acc.shape[1] // tq):
                outs.append(o[:, j * tq:(j + 1) * tq])
        ot = jnp.concatenate(outs, axis=0)
        y_ref[0] = ((ot * _rms(ot, 0)).T * grow_ref[...]).astype(BF16)

    def scores_fn(qst, chunk):
        return lambda c: _dot(k_ref[0, c * chunk:(c + 1) * chunk, :], qst)

    qf = qt.astype(F32).reshape(N_Q_HEADS, HEAD_DIM, tq)
    qn = jnp.sqrt(jnp.sum(qf * qf, axis=1))
    accs = []
    for kv in range(N_KV_HEADS):
        bound = qn[kv * Q_PER_KV:(kv + 1) * Q_PER_KV] * (kn_ref[kv][:, :1] * BOUND_SLACK)
        groups = [range(j0, j0 + FAST_HEADS) for j0 in range(0, Q_PER_KV, FAST_HEADS)]
        ms = [jnp.concatenate([bound[j:j + 1] for j in g], axis=1) for g in groups]
        scores = [scores_fn(q_operand(kv, g), FAST_CHUNK) for g in groups]
        n_chunks = s // FAST_CHUNK
        acc = [None] * len(groups)
        sc_next = [sf(0) for sf in scores]
        for c in range(n_chunks):
            for gi in range(len(groups)):
                sc = sc_next[gi]
                if c + 1 < n_chunks:
                    sc_next[gi] = scores[gi](c + 1)
                p = jnp.exp2(sc - ms[gi]).astype(BF16)
                pv = _dot(vta_ref[kv, :, c * FAST_CHUNK:(c + 1) * FAST_CHUNK], p)
                acc[gi] = pv if acc[gi] is None else acc[gi] + pv
        accs += acc
    finish(accs)
    den = accs[0][HEAD_DIM:HEAD_DIM + 1]
    for acc in accs[1:]:
        den = jnp.minimum(den, acc[HEAD_DIM:HEAD_DIM + 1])
    lmin = jnp.min(den)

    @pl.when(jnp.logical_not(lmin > L_MIN_OK))
    def _():
        accs = []
        for kv in range(N_KV_HEADS):
            scores = scores_fn(q_operand(kv), KEY_CHUNK)
            n_chunks = s // KEY_CHUNK
            m = None
            acc = None
            sc_next = scores(0)
            for c in range(n_chunks):
                sc = sc_next
                if c + 1 < n_chunks:
                    sc_next = scores(c + 1)
                mc = jnp.max(sc, axis=0, keepdims=True)
                m_new = mc if m is None else jnp.maximum(m, mc)
                p = jnp.exp2(sc - m_new).astype(BF16)
                pv = _dot(vta_ref[kv, :, c * KEY_CHUNK:(c + 1) * KEY_CHUNK], p)
                acc = pv if acc is None else acc * jnp.exp2(m - m_new) + pv
                m = m_new
            accs.append(acc)
        finish(accs)


def _attention(qt, k, vt, gcol):
    b, _, s = qt.shape
    return pl.pallas_call(
        _attn_kernel,
        grid=(b, s // Q_TILE),
        in_specs=[
            pl.BlockSpec((1, D_ATTN, Q_TILE), lambda i, j: (i, 0, j)),
            pl.BlockSpec((1, s, D_KV), lambda i, j: (i, 0, 0)),
            pl.BlockSpec((1, D_KV, s), lambda i, j: (i, 0, 0)),
            pl.BlockSpec((1, D_ATTN), lambda i, j: (0, 0)),
        ],
        out_specs=pl.BlockSpec((1, Q_TILE, D_ATTN), lambda i, j: (i, j, 0)),
        out_shape=jax.ShapeDtypeStruct((b, s, D_ATTN), BF16),
        scratch_shapes=[pltpu.VMEM((N_KV_HEADS, PV_ROWS, s), BF16),
                        pltpu.VMEM((N_KV_HEADS, 1, LANES), F32)],
        compiler_params=pltpu.CompilerParams(
            dimension_semantics=("arbitrary", "arbitrary"), vmem_limit_bytes=VMEM_LIMIT),
        name="attention",
    )(qt, k, vt, gcol)


def _shift_seg_down(x):
    row = lax.broadcasted_iota(jnp.int32, x.shape, 0)
    return jnp.where(row == 0, 0.0, pltpu.roll(x, 1, 0))


def _shift_seg_up(x):
    row = lax.broadcasted_iota(jnp.int32, x.shape, 0)
    return jnp.where(row == SUBLANES - 1, 0.0, pltpu.roll(x, SUBLANES - 1, 0))


def _sigmoid(x):
    return 0.5 * jnp.tanh(0.5 * x) + 0.5


def _gelu_tanh(x):
    k = 0.7978845608028654
    hx = 0.5 * x
    t = jnp.tanh(x * (k + (k * 0.044715) * (x * x)))
    return hx * t + hx


def _lru_kernel(xr_ref, cw_ref, cb_ref, wa_ref, wx_ref, ba_ref, bx_ref, lam_ref, y_ref,
                xh_ref, h0_ref, p0_ref, h1_ref, p1_ref, ystage_ref):
    t_len = xr_ref.shape[1]
    lanes = xr_ref.shape[3]
    n_chunks = t_len // GATE_CHUNK

    cw = cw_ref[0] * 0.5
    cb = cb_ref[...] * 0.5

    halo_lo = [_shift_seg_down(xr_ref[0, t_len - 2])[None], _shift_seg_down(xr_ref[0, t_len - 1])[None]]
    halo_hi = [_shift_seg_up(xr_ref[0, 0])[None]]

    def shifted(t0, n, k):
        lo = t0 + k - CONV_PAD_LEFT
        pieces = halo_lo[lo + CONV_PAD_LEFT:] if lo < 0 else []
        pieces = pieces + [xr_ref[0, max(lo, 0):min(lo + n, t_len)]]
        if lo + n > t_len:
            pieces = pieces + halo_hi[:lo + n - t_len]
        return pieces[0] if len(pieces) == 1 else jnp.concatenate(pieces, axis=0)

    for t0 in range(0, t_len, GATE_CHUNK):
        xh = cb[None]
        for kk in range(CONV_W):
            xh = xh + shifted(t0, GATE_CHUNK, kk) * cw[kk:kk + 1][None]
        xh_ref[t0:t0 + GATE_CHUNK] = xh
    lam = lam_ref[0]
    hc = (0.5 * LRU_C * 1.4426950408889634) * (
        jnp.minimum(lam, 0.0) - jnp.log1p(jnp.exp(-jnp.abs(lam))))

    zblk = jnp.zeros((LRU_BLOCK_W, LRU_BLOCK_W), F32)
    wg, bg_half = [], []
    for e in range(2):
        rows = []
        for i in range(lanes // LRU_BLOCK_W):
            blocks = []
            for w_ref in (wa_ref, wx_ref):
                blocks += [w_ref[0, e, i] if c == i else zblk for c in range(lanes // LRU_BLOCK_W)]
            rows.append(jnp.concatenate(blocks, axis=1))
        wg.append(jnp.concatenate(rows, axis=0).astype(BF16))
        bg_half.append(0.5 * jnp.concatenate([ba_ref[0, e:e + 1], bx_ref[0, e:e + 1]], axis=1))

    def gates(t0, e):
        xh = xh_ref[t0:t0 + SCAN_PART].reshape(SCAN_PART * SUBLANES, lanes)
        th = jnp.tanh(_dot(xh.astype(BF16), wg[e]) + bg_half[e])
        tr = th[:, :lanes]
        ti = th[:, lanes:]
        a = jnp.exp2(tr * hc[e:e + 1] + hc[e:e + 1])
        y = 1.0 - a * a
        mult = y * lax.rsqrt(jnp.maximum(y, 1e-30))
        u = (ti * xh + xh) * mult
        return a, u

    zero = jnp.zeros((SUBLANES, lanes), F32)
    one = jnp.ones((SUBLANES, lanes), F32)
    e0, q0, e1, q1 = zero, one, zero, one
    for part in range(t_len // SCAN_PART):
        t0 = part * SCAN_PART
        a, u = gates(t0, 0)
        for t in range(SCAN_PART):
            rows = slice(t * SUBLANES, (t + 1) * SUBLANES)
            e0 = a[rows] * e0 + u[rows]
            q0 = a[rows] * q0
            h0_ref[t0 + t] = e0
            p0_ref[t0 + t] = q0
        t1 = t_len - (part + 1) * SCAN_PART
        a, u = gates(t1, 1)
        for t in reversed(range(SCAN_PART)):
            rows = slice(t * SUBLANES, (t + 1) * SUBLANES)
            e1 = a[rows] * e1 + u[rows]
            q1 = a[rows] * q1
            h1_ref[t1 + t] = e1
            p1_ref[t1 + t] = q1

    c0 = zero
    c1 = zero
    for _ in range(N_SEG - 1):
        c0 = _shift_seg_down(e0 + q0 * c0)
        c1 = _shift_seg_up(e1 + q1 * c1)

    def out_body(i, carry):
        t0 = pl.multiple_of(i * GATE_CHUNK, GATE_CHUNK)
        sl = pl.ds(t0, GATE_CHUNK)
        h = (h0_ref[sl] + p0_ref[sl] * c0[None]) + (h1_ref[sl] + p1_ref[sl] * c1[None])
        ystage_ref[...] = h.reshape(GATE_CHUNK * SUBLANES, lanes)
        for seg in range(N_SEG):
            y_ref[0, pl.ds(seg * t_len + t0, GATE_CHUNK)] = (
                ystage_ref[pl.ds(seg, GATE_CHUNK, stride=SUBLANES)])
        return carry

    lax.fori_loop(0, n_chunks, out_body, 0)


def _lru(xr_il, conv_w, conv_b, wa, wx, ba, bx, lam):
    b, t_len, n_seg, d = xr_il.shape
    n_groups = d // LANES
    blocks_per_group = LANES // LRU_BLOCK_W
    slab = pltpu.VMEM((t_len, n_seg, LANES), F32)
    w_spec = pl.BlockSpec((1, 2, blocks_per_group, LRU_BLOCK_W, LRU_BLOCK_W),
                          lambda i, j: (0, 0, j, 0, 0))
    lane_group = pl.BlockSpec((1, 2, LANES), lambda i, j: (0, 0, j))
    return pl.pallas_call(
        _lru_kernel,
        grid=(b, n_groups),
        in_specs=[
            pl.BlockSpec((1, t_len, n_seg, LANES), lambda i, j: (i, 0, 0, j)),
            pl.BlockSpec((1, CONV_W, LANES), lambda i, j: (0, 0, j)),
            pl.BlockSpec((1, LANES), lambda i, j: (0, j)),
            w_spec, w_spec, lane_group, lane_group, lane_group,
        ],
        out_specs=pl.BlockSpec((1, t_len * n_seg, LANES), lambda i, j: (i, 0, j)),
        out_shape=jax.ShapeDtypeStruct((b, t_len * n_seg, d), F32),
        scratch_shapes=[slab] * 5 + [
            pltpu.VMEM((GATE_CHUNK * n_seg, LANES), F32)],
        compiler_params=pltpu.CompilerParams(
            dimension_semantics=("arbitrary", "arbitrary"), vmem_limit_bytes=VMEM_LIMIT),
        name="rglru",
    )(xr_il, conv_w, conv_b, wa, wx, ba, bx, lam)


def _tail_kernel(x_ref, ya_ref, hl_ref, xg_ref, p_ref, glru_ref, wout_ref, gmlp_ref, wup_ref,
                 wdown_ref, gple_ref, wgate_ref, wproj_ref, gfin_ref, o_ref):
    x = x_ref[0]
    yl = hl_ref[0] * _gelu_tanh(xg_ref[0])
    yl_n = (yl * _rms(yl, -1) * glru_ref[...]).astype(BF16)
    h = x + _dot(ya_ref[0], wout_ref[:D_ATTN]) + _dot(yl_n, wout_ref[D_ATTN:])
    hn = (h * _rms(h, -1) * gmlp_ref[...]).astype(BF16)
    m = _dot(hn, wup_ref[...])
    act = jnp.square(jnp.maximum(m, 0.0)).astype(BF16)
    h = h + _dot(act, wdown_ref[...])
    hn = (h * _rms(h, -1) * gple_ref[...]).astype(BF16)
    gate = _sigmoid(_dot(hn, wgate_ref[...]))
    h = h + gate * _dot(p_ref[0, 0].astype(BF16), wproj_ref[...])
    o_ref[0] = h * _rms(h, -1) * gfin_ref[...]


def _tail(x, ya, hl, xg_il, p, glru, wout, gmlp, wup, wdown, gple, wgate, wproj, gfin):
    b, s, d = x.shape
    t = s // N_SEG
    const = lambda *_: (0, 0)

    def resident(arr):
        return pl.BlockSpec(arr.shape, const, pipeline_mode=pl.Buffered(1))

    return pl.pallas_call(
        _tail_kernel,
        grid=(b, N_SEG),
        in_specs=[
            pl.BlockSpec((1, t, d), lambda i, j: (i, j, 0)),
            pl.BlockSpec((1, t, D_ATTN), lambda i, j: (i, j, 0)),
            pl.BlockSpec((1, t, D_LRU), lambda i, j: (i, j, 0)),
            pl.BlockSpec((1, t, D_LRU), lambda i, j: (i, 0, j)),
            pl.BlockSpec((1, 1, t, D_PLE), lambda i, j: (0, i, j, 0)),
            resident(glru), resident(wout), resident(gmlp), resident(wup), resident(wdown),
            resident(gple), resident(wgate), resident(wproj), resident(gfin),
        ],
        out_specs=pl.BlockSpec((1, t, d), lambda i, j: (i, j, 0)),
        out_shape=jax.ShapeDtypeStruct((b, s, d), F32),
        compiler_params=pltpu.CompilerParams(
            dimension_semantics=("arbitrary", "arbitrary"), vmem_limit_bytes=VMEM_LIMIT),
        name="tail",
    )(x, ya, hl, xg_il, p, glru, wout, gmlp, wup, wdown, gple, wgate, wproj, gfin)


def _rope_tables_t(seq_len):
    pos = np.arange(seq_len)
    row = (pos // GRID_W).astype(np.float32)
    col = (pos % GRID_W).astype(np.float32)
    inv_freq = np.float32(ROPE_THETA) ** (-np.arange(N_FREQ, dtype=np.float32) / np.float32(N_FREQ))
    ang_r = (row[None, :] * inv_freq[:, None]).astype(np.float32)
    ang_c = (col[None, :] * inv_freq[:, None]).astype(np.float32)
    cos_t = np.concatenate([np.cos(ang_r)] * 2 + [np.cos(ang_c)] * 2, axis=0)
    sin_t = np.concatenate([-np.sin(ang_r), np.sin(ang_r), -np.sin(ang_c), np.sin(ang_c)], axis=0)
    return jnp.asarray(cos_t, F32), jnp.asarray(sin_t, F32)


def kernel(x, p, mix_norm, w_in, q_norm, k_norm, conv_w, conv_b, lru_wa, lru_ba, lru_wx, lru_bx,
           lru_lambda, attn_out_norm, lru_out_norm, w_out, mlp_norm, w_up, w_down, ple_norm,
           w_ple_gate, w_ple_proj, final_norm):
    b, s, d = x.shape
    assert w_in.shape[0] == 1, "single-layer trunk: the final norm is fused into the layer tail"
    cos_t, sin_t = _rope_tables_t(s)
    wqkv_t = w_in[0, :, :D_ATTN + 2 * D_KV].T.astype(BF16)
    wrg = w_in[0, :, D_ATTN + 2 * D_KV:].astype(BF16)
    qt, k, vt, xr, xg = _in_proj(x, mix_norm, wqkv_t, wrg, q_norm, k_norm, cos_t, sin_t)
    ya = _attention(qt, k, vt, attn_out_norm)
    hl = _lru(xr.reshape(b, s // N_SEG, N_SEG, D_LRU), conv_w, conv_b, lru_wa, lru_wx, lru_ba, lru_bx,
              lru_lambda)
    return _tail(x, ya, hl, xg, p, lru_out_norm, w_out[0].astype(BF16), mlp_norm,
                 w_up[0].astype(BF16), w_down[0].astype(BF16), ple_norm,
                 w_ple_gate[0].astype(BF16), w_ple_proj[0].astype(BF16), final_norm.reshape(1, -1))
```

```python
import jax
import jax.numpy as jnp
import numpy as np
from jax import lax
from jax.experimental import pallas as pl
from jax.experimental.pallas import tpu as pltpu

D_MODEL = 1024
GRID_W = 64
HEAD_DIM = 64
D_ATTN = 512
N_Q_HEADS = 8
N_KV_HEADS = 2
Q_PER_KV = 4
D_KV = 128
ROPE_THETA = 10000.0
N_FREQ = 16
D_LRU = 512
LRU_BLOCK_W = 64
LRU_C = 8.0
CONV_W = 4
CONV_PAD_LEFT = 2
D_FF = 4096
D_PLE = 256
NORM_EPS = 1e-6

SUBLANES = 8
LANES = 128

N_SEG = SUBLANES
Q_TILE = 256
KEY_CHUNK = 256
FAST_CHUNK = 256
FAST_HEADS = 1
PV_ROWS = 80
BOUND_SLACK = 1.001
L_MIN_OK = 2.0 ** -80
GATE_CHUNK = 64
SCAN_PART = 128
VMEM_LIMIT = 56 * 1024 * 1024
Q_SCALE = HEAD_DIM ** -0.5 * 1.4426950408889634

F32 = jnp.float32
BF16 = jnp.bfloat16


def _rms(x, axis):
    return lax.rsqrt(jnp.mean(x * x, axis=axis, keepdims=True) + NORM_EPS)


def _dot(a, b):
    return jnp.dot(a, b, preferred_element_type=F32)


def _as_column(row):
    n = row.shape[1]
    ri = lax.broadcasted_iota(jnp.int32, (n, n), 0)
    ci = lax.broadcasted_iota(jnp.int32, (n, n), 1)
    return jnp.sum(jnp.where(ri == ci, row, 0.0), axis=1, keepdims=True)


def _dot_nt(a, b):
    return lax.dot_general(a, b, (((1,), (1,)), ((), ())), preferred_element_type=F32)


def _norm_rope_t(xt, gcol, cos_t, sin_t, n_heads):
    t = xt.shape[1]
    x3 = xt.reshape(n_heads, HEAD_DIM, t)
    xn = x3 * _rms(x3, 1) * gcol[None]
    x5 = xn.reshape(n_heads * 2, 2, N_FREQ, t)
    xs = jnp.concatenate([x5[:, 1:2], x5[:, 0:1]], axis=1).reshape(n_heads, HEAD_DIM, t)
    out = xn * cos_t[None] + xs * sin_t[None]
    return out.reshape(n_heads * HEAD_DIM, t)


def _in_proj_kernel(x_ref, gmix_ref, wqkv_t_ref, wrg_ref, gq_ref, gk_ref, cos_ref, sin_ref,
                    qt_ref, k_ref, vt_ref, xr_ref, xg_ref):
    x = x_ref[0]
    hn = (x * _rms(x, -1) * gmix_ref[...]).astype(BF16)
    zt = _dot_nt(wqkv_t_ref[...], hn)
    zr = _dot(hn, wrg_ref[...])
    cos_t = cos_ref[...]
    sin_t = sin_ref[...]
    qt = _norm_rope_t(zt[:D_ATTN], _as_column(gq_ref[...]) * Q_SCALE, cos_t, sin_t, N_Q_HEADS)
    kt = _norm_rope_t(zt[D_ATTN:D_ATTN + D_KV], _as_column(gk_ref[...]), cos_t, sin_t, N_KV_HEADS)
    qt_ref[0] = qt.astype(BF16)
    k_ref[0] = kt.T.astype(BF16)
    vt_ref[0] = zt[D_ATTN + D_KV:].astype(BF16)
    xr_ref[0] = zr[:, :D_LRU]
    xg_ref[0] = zr[:, D_LRU:]


def _in_proj(x, gmix, wqkv_t, wrg, gq, gk, cos_t, sin_t):
    b, s, d = x.shape
    t = s // N_SEG
    const = lambda *_: (0, 0)
    return pl.pallas_call(
        _in_proj_kernel,
        grid=(b, N_SEG),
        in_specs=[
            pl.BlockSpec((1, t, d), lambda i, j: (i, j, 0)),
            pl.BlockSpec((1, d), const),
            pl.BlockSpec(wqkv_t.shape, const),
            pl.BlockSpec(wrg.shape, const),
            pl.BlockSpec((1, HEAD_DIM), const),
            pl.BlockSpec((1, HEAD_DIM), const),
            pl.BlockSpec((HEAD_DIM, t), lambda i, j: (0, j)),
            pl.BlockSpec((HEAD_DIM, t), lambda i, j: (0, j)),
        ],
        out_specs=[
            pl.BlockSpec((1, D_ATTN, t), lambda i, j: (i, 0, j)),
            pl.BlockSpec((1, t, D_KV), lambda i, j: (i, j, 0)),
            pl.BlockSpec((1, D_KV, t), lambda i, j: (i, 0, j)),
            pl.BlockSpec((1, t, D_LRU), lambda i, j: (i, 0, j)),
            pl.BlockSpec((1, t, D_LRU), lambda i, j: (i, 0, j)),
        ],
        out_shape=[
            jax.ShapeDtypeStruct((b, D_ATTN, s), BF16),
            jax.ShapeDtypeStruct((b, s, D_KV), BF16),
            jax.ShapeDtypeStruct((b, D_KV, s), BF16),
            jax.ShapeDtypeStruct((b, t, N_SEG * D_LRU), F32),
            jax.ShapeDtypeStruct((b, t, N_SEG * D_LRU), F32),
        ],
        compiler_params=pltpu.CompilerParams(
            dimension_semantics=("arbitrary", "arbitrary"), vmem_limit_bytes=VMEM_LIMIT),
        name="in_proj",
    )(x, gmix, wqkv_t, wrg, gq, gk, cos_t, sin_t)


def _attn_kernel(qt_ref, k_ref, vt_ref, grow_ref, y_ref, vta_ref, kn_ref):
    s = k_ref.shape[1]
    tq = qt_ref.shape[2]

    @pl.when(pl.program_id(1) == 0)
    def _():
        vt = vt_ref[0]
        row = lax.broadcasted_iota(jnp.int32, (PV_ROWS - HEAD_DIM, s), 0)
        ones_pad = jnp.where(row == 0, 1.0, 0.0).astype(BF16)
        kf = k_ref[0].astype(F32)
        lane = lax.broadcasted_iota(jnp.int32, kf.shape, 1)
        ksq = kf * kf
        for kv in range(N_KV_HEADS):
            vta_ref[kv] = jnp.concatenate([vt[kv * HEAD_DIM:(kv + 1) * HEAD_DIM], ones_pad], axis=0)
            in_head = (lane >= kv * HEAD_DIM) & (lane < (kv + 1) * HEAD_DIM)
            n2 = jnp.sum(jnp.where(in_head, ksq, 0.0), axis=1, keepdims=True)
            kn_ref[kv] = jnp.broadcast_to(jnp.sqrt(jnp.max(n2, axis=0, keepdims=True)), (1, LANES))

    qt = qt_ref[0]
    zeros = jnp.zeros((HEAD_DIM, tq), BF16)

    def q_operand(kv, heads=range(Q_PER_KV)):
        cols = []
        for j in heads:
            h = kv * Q_PER_KV + j
            qh = qt[h * HEAD_DIM:(h + 1) * HEAD_DIM]
            cols.append(jnp.concatenate([qh, zeros] if kv == 0 else [zeros, qh], axis=0))
        return jnp.concatenate(cols, axis=1)

    def finish(accs):
        outs = []
        for acc in accs:
            o = acc[:HEAD_DIM] / acc[HEAD_DIM:HEAD_DIM + 1]
            for j in range(acc.shape[1] // tq):
                outs.append(o[:, j * tq:(j + 1) * tq])
        ot = jnp.concatenate(outs, axis=0)
        y_ref[0] = ((ot * _rms(ot, 0)).T * grow_ref[...]).astype(BF16)

    def scores_fn(qst, chunk):
        return lambda c: _dot(k_ref[0, c * chunk:(c + 1) * chunk, :], qst)

    qf = qt.astype(F32).reshape(N_Q_HEADS, HEAD_DIM, tq)
    qn = jnp.sqrt(jnp.sum(qf * qf, axis=1))
    accs = []
    for kv in range(N_KV_HEADS):
        bound = qn[kv * Q_PER_KV:(kv + 1) * Q_PER_KV] * (kn_ref[kv][:, :1] * BOUND_SLACK)
        groups = [range(j0, j0 + FAST_HEADS) for j0 in range(0, Q_PER_KV, FAST_HEADS)]
        ms = [jnp.concatenate([bound[j:j + 1] for j in g], axis=1) for g in groups]
        scores = [scores_fn(q_operand(kv, g), FAST_CHUNK) for g in groups]
        n_chunks = s // FAST_CHUNK
        acc = [None] * len(groups)
        sc_next = [sf(0) for sf in scores]
        for c in range(n_chunks):
            for gi in range(len(groups)):
                sc = sc_next[gi]
                if c + 1 < n_chunks:
                    sc_next[gi] = scores[gi](c + 1)
                p = jnp.exp2(sc - ms[gi]).astype(BF16)
                pv = _dot(vta_ref[kv, :, c * FAST_CHUNK:(c + 1) * FAST_CHUNK], p)
                acc[gi] = pv if acc[gi] is None else acc[gi] + pv
        accs += acc
    finish(accs)
    den = accs[0][HEAD_DIM:HEAD_DIM + 1]
    for acc in accs[1:]:
        den = jnp.minimum(den, acc[HEAD_DIM:HEAD_DIM + 1])
    lmin = jnp.min(den)

    @pl.when(jnp.logical_not(lmin > L_MIN_OK))
    def _():
        accs = []
        for kv in range(N_KV_HEADS):
            scores = scores_fn(q_operand(kv), KEY_CHUNK)
            n_chunks = s // KEY_CHUNK
            m = None
            acc = None
            sc_next = scores(0)
            for c in range(n_chunks):
                sc = sc_next
                if c + 1 < n_chunks:
                    sc_next = scores(c + 1)
                mc = jnp.max(sc, axis=0, keepdims=True)
                m_new = mc if m is None else jnp.maximum(m, mc)
                p = jnp.exp2(sc - m_new).astype(BF16)
                pv = _dot(vta_ref[kv, :, c * KEY_CHUNK:(c + 1) * KEY_CHUNK], p)
                acc = pv if acc is None else acc * jnp.exp2(m - m_new) + pv
                m = m_new
            accs.append(acc)
        finish(accs)


def _attention(qt, k, vt, gcol):
    b, _, s = qt.shape
    return pl.pallas_call(
        _attn_kernel,
        grid=(b, s // Q_TILE),
        in_specs=[
            pl.BlockSpec((1, D_ATTN, Q_TILE), lambda i, j: (i, 0, j)),
            pl.BlockSpec((1, s, D_KV), lambda i, j: (i, 0, 0)),
            pl.BlockSpec((1, D_KV, s), lambda i, j: (i, 0, 0)),
            pl.BlockSpec((1, D_ATTN), lambda i, j: (0, 0)),
        ],
        out_specs=pl.BlockSpec((1, Q_TILE, D_ATTN), lambda i, j: (i, j, 0)),
        out_shape=jax.ShapeDtypeStruct((b, s, D_ATTN), BF16),
        scratch_shapes=[pltpu.VMEM((N_KV_HEADS, PV_ROWS, s), BF16),
                        pltpu.VMEM((N_KV_HEADS, 1, LANES), F32)],
        compiler_params=pltpu.CompilerParams(
            dimension_semantics=("arbitrary", "arbitrary"), vmem_limit_bytes=VMEM_LIMIT),
        name="attention",
    )(qt, k, vt, gcol)


def _shift_seg_down(x):
    row = lax.broadcasted_iota(jnp.int32, x.shape, 0)
    return jnp.where(row == 0, 0.0, pltpu.roll(x, 1, 0))


def _shift_seg_up(x):
    row = lax.broadcasted_iota(jnp.int32, x.shape, 0)
    return jnp.where(row == SUBLANES - 1, 0.0, pltpu.roll(x, SUBLANES - 1, 0))


def _sigmoid(x):
    return 0.5 * jnp.tanh(0.5 * x) + 0.5


def _gelu_tanh(x):
    k = 0.7978845608028654
    hx = 0.5 * x
    t = jnp.tanh(x * (k + (k * 0.044715) * (x * x)))
    return hx * t + hx


def _lru_kernel(xr_ref, cw_ref, cb_ref, wa_ref, wx_ref, ba_ref, bx_ref, lam_ref, y_ref,
                xh_ref, h0_ref, p0_ref, h1_ref, p1_ref, ystage_ref):
    t_len = xr_ref.shape[1]
    lanes = xr_ref.shape[3]
    n_chunks = t_len // GATE_CHUNK

    cw = cw_ref[0] * 0.5
    cb = cb_ref[...] * 0.5

    halo_lo = [_shift_seg_down(xr_ref[0, t_len - 2])[None], _shift_seg_down(xr_ref[0, t_len - 1])[None]]
    halo_hi = [_shift_seg_up(xr_ref[0, 0])[None]]

    def shifted(t0, n, k):
        lo = t0 + k - CONV_PAD_LEFT
        pieces = halo_lo[lo + CONV_PAD_LEFT:] if lo < 0 else []
        pieces = pieces + [xr_ref[0, max(lo, 0):min(lo + n, t_len)]]
        if lo + n > t_len:
            pieces = pieces + halo_hi[:lo + n - t_len]
        return pieces[0] if len(pieces) == 1 else jnp.concatenate(pieces, axis=0)

    for t0 in range(0, t_len, GATE_CHUNK):
        xh = cb[None]
        for kk in range(CONV_W):
            xh = xh + shifted(t0, GATE_CHUNK, kk) * cw[kk:kk + 1][None]
        xh_ref[t0:t0 + GATE_CHUNK] = xh
    lam = lam_ref[0]
    hc = (0.5 * LRU_C * 1.4426950408889634) * (
        jnp.minimum(lam, 0.0) - jnp.log1p(jnp.exp(-jnp.abs(lam))))

    zblk = jnp.zeros((LRU_BLOCK_W, LRU_BLOCK_W), F32)
    wg, bg_half = [], []
    for e in range(2):
        rows = []
        for i in range(lanes // LRU_BLOCK_W):
            blocks = []
            for w_ref in (wa_ref, wx_ref):
                blocks += [w_ref[0, e, i] if c == i else zblk for c in range(lanes // LRU_BLOCK_W)]
            rows.append(jnp.concatenate(blocks, axis=1))
        wg.append(jnp.concatenate(rows, axis=0).astype(BF16))
        bg_half.append(0.5 * jnp.concatenate([ba_ref[0, e:e + 1], bx_ref[0, e:e + 1]], axis=1))

    def gates(t0, e):
        xh = xh_ref[t0:t0 + SCAN_PART].reshape(SCAN_PART * SUBLANES, lanes)
        th = jnp.tanh(_dot(xh.astype(BF16), wg[e]) + bg_half[e])
        tr = th[:, :lanes]
        ti = th[:, lanes:]
        a = jnp.exp2(tr * hc[e:e + 1] + hc[e:e + 1])
        y = 1.0 - a * a
        mult = y * lax.rsqrt(jnp.maximum(y, 1e-30))
        u = (ti * xh + xh) * mult
        return a, u

    zero = jnp.zeros((SUBLANES, lanes), F32)
    one = jnp.ones((SUBLANES, lanes), F32)
    e0, q0, e1, q1 = zero, one, zero, one
    for part in range(t_len // SCAN_PART):
        t0 = part * SCAN_PART
        a, u = gates(t0, 0)
        for t in range(SCAN_PART):
            rows = slice(t * SUBLANES, (t + 1) * SUBLANES)
            e0 = a[rows] * e0 + u[rows]
            q0 = a[rows] * q0
            h0_ref[t0 + t] = e0
            p0_ref[t0 + t] = q0
        t1 = t_len - (part + 1) * SCAN_PART
        a, u = gates(t1, 1)
        for t in reversed(range(SCAN_PART)):
            rows = slice(t * SUBLANES, (t + 1) * SUBLANES)
            e1 = a[rows] * e1 + u[rows]
            q1 = a[rows] * q1
            h1_ref[t1 + t] = e1
            p1_ref[t1 + t] = q1

    c0 = zero
    c1 = zero
    for _ in range(N_SEG - 1):
        c0 = _shift_seg_down(e0 + q0 * c0)
        c1 = _shift_seg_up(e1 + q1 * c1)

    def out_body(i, carry):
        t0 = pl.multiple_of(i * GATE_CHUNK, GATE_CHUNK)
        sl = pl.ds(t0, GATE_CHUNK)
        h = (h0_ref[sl] + p0_ref[sl] * c0[None]) + (h1_ref[sl] + p1_ref[sl] * c1[None])
        ystage_ref[...] = h.reshape(GATE_CHUNK * SUBLANES, lanes)
        for seg in range(N_SEG):
            y_ref[0, pl.ds(seg * t_len + t0, GATE_CHUNK)] = (
                ystage_ref[pl.ds(seg, GATE_CHUNK, stride=SUBLANES)])
        return carry

    lax.fori_loop(0, n_chunks, out_body, 0)


def _lru(xr_il, conv_w, conv_b, wa, wx, ba, bx, lam):
    b, t_len, n_seg, d = xr_il.shape
    n_groups = d // LANES
    blocks_per_group = LANES // LRU_BLOCK_W
    slab = pltpu.VMEM((t_len, n_seg, LANES), F32)
    w_spec = pl.BlockSpec((1, 2, blocks_per_group, LRU_BLOCK_W, LRU_BLOCK_W),
                          lambda i, j: (0, 0, j, 0, 0))
    lane_group = pl.BlockSpec((1, 2, LANES), lambda i, j: (0, 0, j))
    return pl.pallas_call(
        _lru_kernel,
        grid=(b, n_groups),
        in_specs=[
            pl.BlockSpec((1, t_len, n_seg, LANES), lambda i, j: (i, 0, 0, j)),
            pl.BlockSpec((1, CONV_W, LANES), lambda i, j: (0, 0, j)),
            pl.BlockSpec((1, LANES), lambda i, j: (0, j)),
            w_spec, w_spec, lane_group, lane_group, lane_group,
        ],
        out_specs=pl.BlockSpec((1, t_len * n_seg, LANES), lambda i, j: (i, 0, j)),
        out_shape=jax.ShapeDtypeStruct((b, t_len * n_seg, d), F32),
        scratch_shapes=[slab] * 5 + [
            pltpu.VMEM((GATE_CHUNK * n_seg, LANES), F32)],
        compiler_params=pltpu.CompilerParams(
            dimension_semantics=("arbitrary", "arbitrary"), vmem_limit_bytes=VMEM_LIMIT),
        name="rglru",
    )(xr_il, conv_w, conv_b, wa, wx, ba, bx, lam)


def _tail_kernel(x_ref, ya_ref, hl_ref, xg_ref, p_ref, glru_ref, wout_ref, gmlp_ref, wup_ref,
                 wdown_ref, gple_ref, wgate_ref, wproj_ref, gfin_ref, o_ref):
    x = x_ref[0]
    yl = hl_ref[0] * _gelu_tanh(xg_ref[0])
    yl_n = (yl * _rms(yl, -1) * glru_ref[...]).astype(BF16)
    h = x + _dot(ya_ref[0], wout_ref[:D_ATTN]) + _dot(yl_n, wout_ref[D_ATTN:])
    hn = (h * _rms(h, -1) * gmlp_ref[...]).astype(BF16)
    m = _dot(hn, wup_ref[...])
    act = jnp.square(jnp.maximum(m, 0.0)).astype(BF16)
    h = h + _dot(act, wdown_ref[...])
    hn = (h * _rms(h, -1) * gple_ref[...]).astype(BF16)
    gate = _sigmoid(_dot(hn, wgate_ref[...]))
    h = h + gate * _dot(p_ref[0, 0].astype(BF16), wproj_ref[...])
    o_ref[0] = h * _rms(h, -1) * gfin_ref[...]


def _tail(x, ya, hl, xg_il, p, glru, wout, gmlp, wup, wdown, gple, wgate, wproj, gfin):
    b, s, d = x.shape
    t = s // N_SEG
    const = lambda *_: (0, 0)

    def resident(arr):
        return pl.BlockSpec(arr.shape, const, pipeline_mode=pl.Buffered(1))

    return pl.pallas_call(
        _tail_kernel,
        grid=(b, N_SEG),
        in_specs=[
            pl.BlockSpec((1, t, d), lambda i, j: (i, j, 0)),
            pl.BlockSpec((1, t, D_ATTN), lambda i, j: (i, j, 0)),
            pl.BlockSpec((1, t, D_LRU), lambda i, j: (i, j, 0)),
            pl.BlockSpec((1, t, D_LRU), lambda i, j: (i, 0, j)),
            pl.BlockSpec((1, 1, t, D_PLE), lambda i, j: (0, i, j, 0)),
            resident(glru), resident(wout), resident(gmlp), resident(wup), resident(wdown),
            resident(gple), resident(wgate), resident(wproj), resident(gfin),
        ],
        out_specs=pl.BlockSpec((1, t, d), lambda i, j: (i, j, 0)),
        out_shape=jax.ShapeDtypeStruct((b, s, d), F32),
        compiler_params=pltpu.CompilerParams(
            dimension_semantics=("arbitrary", "arbitrary"), vmem_limit_bytes=VMEM_LIMIT),
        name="tail",
    )(x, ya, hl, xg_il, p, glru, wout, gmlp, wup, wdown, gple, wgate, wproj, gfin)


def _rope_tables_t(seq_len):
    pos = np.arange(seq_len)
    row = (pos // GRID_W).astype(np.float32)
    col = (pos % GRID_W).astype(np.float32)
    inv_freq = np.float32(ROPE_THETA) ** (-np.arange(N_FREQ, dtype=np.float32) / np.float32(N_FREQ))
    ang_r = (row[None, :] * inv_freq[:, None]).astype(np.float32)
    ang_c = (col[None, :] * inv_freq[:, None]).astype(np.float32)
    cos_t = np.concatenate([np.cos(ang_r)] * 2 + [np.cos(ang_c)] * 2, axis=0)
    sin_t = np.concatenate([-np.sin(ang_r), np.sin(ang_r), -np.sin(ang_c), np.sin(ang_c)], axis=0)
    return jnp.asarray(cos_t, F32), jnp.asarray(sin_t, F32)


def kernel(x, p, mix_norm, w_in, q_norm, k_norm, conv_w, conv_b, lru_wa, lru_ba, lru_wx, lru_bx,
           lru_lambda, attn_out_norm, lru_out_norm, w_out, mlp_norm, w_up, w_down, ple_norm,
           w_ple_gate, w_ple_proj, final_norm):
    b, s, d = x.shape
    assert w_in.shape[0] == 1, "single-layer trunk: the final norm is fused into the layer tail"
    cos_t, sin_t = _rope_tables_t(s)
    wqkv_t = w_in[0, :, :D_ATTN + 2 * D_KV].T.astype(BF16)
    wrg = w_in[0, :, D_ATTN + 2 * D_KV:].astype(BF16)
    qt, k, vt, xr, xg = _in_proj(x, mix_norm, wqkv_t, wrg, q_norm, k_norm, cos_t, sin_t)
    ya = _attention(qt, k, vt, attn_out_norm)
    hl = _lru(xr.reshape(b, s // N_SEG, N_SEG, D_LRU), conv_w, conv_b, lru_wa, lru_wx, lru_ba, lru_bx,
              lru_lambda)
    return _tail(x, ya, hl, xg, p, lru_out_norm, w_out[0].astype(BF16), mlp_norm,
                 w_up[0].astype(BF16), w_down[0].astype(BF16), ple_norm,
                 w_ple_gate[0].astype(BF16), w_ple_proj[0].astype(BF16), final_norm.reshape(1, -1))
```

```python
import jax
import jax.numpy as jnp
import numpy as np
from jax import lax
from jax.experimental import pallas as pl
from jax.experimental.pallas import tpu as pltpu

D_MODEL = 1024
GRID_W = 64
HEAD_DIM = 64
D_ATTN = 512
N_Q_HEADS = 8
N_KV_HEADS = 2
Q_PER_KV = 4
D_KV = 128
ROPE_THETA = 10000.0
N_FREQ = 16
D_LRU = 512
LRU_BLOCK_W = 64
LRU_C = 8.0
CONV_W = 4
CONV_PAD_LEFT = 2
D_FF = 4096
D_PLE = 256
NORM_EPS = 1e-6

SUBLANES = 8
LANES = 128

N_SEG = SUBLANES
Q_TILE = 256
KEY_CHUNK = 256
FAST_CHUNK = 256
FAST_HEADS = 1
PV_ROWS = 80
BOUND_SLACK = 1.001
L_MIN_OK = 2.0 ** -80
GATE_CHUNK = 64
SCAN_PART = 128
TAIL_SPLIT = 2
IN_SPLIT = 2
VMEM_LIMIT = 56 * 1024 * 1024
Q_SCALE = HEAD_DIM ** -0.5 * 1.4426950408889634

F32 = jnp.float32
BF16 = jnp.bfloat16


def _rms(x, axis):
    return lax.rsqrt(jnp.mean(x * x, axis=axis, keepdims=True) + NORM_EPS)


def _dot(a, b):
    return jnp.dot(a, b, preferred_element_type=F32)


def _as_column(row):
    n = row.shape[1]
    ri = lax.broadcasted_iota(jnp.int32, (n, n), 0)
    ci = lax.broadcasted_iota(jnp.int32, (n, n), 1)
    return jnp.sum(jnp.where(ri == ci, row, 0.0), axis=1, keepdims=True)


def _dot_nt(a, b):
    return lax.dot_general(a, b, (((1,), (1,)), ((), ())), preferred_element_type=F32)


def _norm_rope_t(xt, gcol, cos_t, sin_t, n_heads):
    t = xt.shape[1]
    x3 = xt.reshape(n_heads, HEAD_DIM, t)
    xn = x3 * _rms(x3, 1) * gcol[None]
    x5 = xn.reshape(n_heads * 2, 2, N_FREQ, t)
    xs = jnp.concatenate([x5[:, 1:2], x5[:, 0:1]], axis=1).reshape(n_heads, HEAD_DIM, t)
    out = xn * cos_t[None] + xs * sin_t[None]
    return out.reshape(n_heads * HEAD_DIM, t)


def _in_proj_kernel(x_ref, gmix_ref, wqkv_t_ref, wrg_ref, gq_ref, gk_ref, cos_ref, sin_ref,
                    qt_ref, k_ref, vt_ref, xr_ref, xg_ref):
    gq = _as_column(gq_ref[...]) * Q_SCALE
    gk = _as_column(gk_ref[...])
    t = x_ref.shape[1]
    groups = [slice(i * t // IN_SPLIT, (i + 1) * t // IN_SPLIT) for i in range(IN_SPLIT)]
    hns = []
    for r in groups:
        x = x_ref[0, r]
        hns.append((x * _rms(x, -1) * gmix_ref[...]).astype(BF16))
    lru_proj = lambda hn: _dot(hn, wrg_ref[...])
    qkv_proj = lambda hn: _dot_nt(wqkv_t_ref[...], hn)
    zrs = [lru_proj(hn) for hn in hns[:-1]]
    zts = [qkv_proj(hn) for hn in hns]
    zrs.append(lru_proj(hns[-1]))
    for r, zt, zr in zip(groups, zts, zrs):
        cos_t = cos_ref[:, r]
        sin_t = sin_ref[:, r]
        qt = _norm_rope_t(zt[:D_ATTN], gq, cos_t, sin_t, N_Q_HEADS)
        kt = _norm_rope_t(zt[D_ATTN:D_ATTN + D_KV], gk, cos_t, sin_t, N_KV_HEADS)
        qt_ref[0, :, r] = qt.astype(BF16)
        k_ref[0, r] = kt.T.astype(BF16)
        vt_ref[0, :, r] = zt[D_ATTN + D_KV:].astype(BF16)
        xr_ref[0, r] = zr[:, :D_LRU]
        xg_ref[0, r] = zr[:, D_LRU:]


def _in_proj(x, gmix, wqkv_t, wrg, gq, gk, cos_t, sin_t):
    b, s, d = x.shape
    t = s // N_SEG
    const = lambda *_: (0, 0)
    return pl.pallas_call(
        _in_proj_kernel,
        grid=(b, N_SEG),
        in_specs=[
            pl.BlockSpec((1, t, d), lambda i, j: (i, j, 0)),
            pl.BlockSpec((1, d), const),
            pl.BlockSpec(wqkv_t.shape, const),
            pl.BlockSpec(wrg.shape, const),
            pl.BlockSpec((1, HEAD_DIM), const),
            pl.BlockSpec((1, HEAD_DIM), const),
            pl.BlockSpec((HEAD_DIM, t), lambda i, j: (0, j)),
            pl.BlockSpec((HEAD_DIM, t), lambda i, j: (0, j)),
        ],
        out_specs=[
            pl.BlockSpec((1, D_ATTN, t), lambda i, j: (i, 0, j)),
            pl.BlockSpec((1, t, D_KV), lambda i, j: (i, j, 0)),
            pl.BlockSpec((1, D_KV, t), lambda i, j: (i, 0, j)),
            pl.BlockSpec((1, t, D_LRU), lambda i, j: (i, 0, j)),
            pl.BlockSpec((1, t, D_LRU), lambda i, j: (i, 0, j)),
        ],
        out_shape=[
            jax.ShapeDtypeStruct((b, D_ATTN, s), BF16),
            jax.ShapeDtypeStruct((b, s, D_KV), BF16),
            jax.ShapeDtypeStruct((b, D_KV, s), BF16),
            jax.ShapeDtypeStruct((b, t, N_SEG * D_LRU), F32),
            jax.ShapeDtypeStruct((b, t, N_SEG * D_LRU), F32),
        ],
        compiler_params=pltpu.CompilerParams(
            dimension_semantics=("arbitrary", "arbitrary"), vmem_limit_bytes=VMEM_LIMIT),
        name="in_proj",
    )(x, gmix, wqkv_t, wrg, gq, gk, cos_t, sin_t)


def _attn_kernel(qt_ref, k_ref, vt_ref, grow_ref, y_ref, vta_ref, kn_ref):
    s = k_ref.shape[1]
    tq = qt_ref.shape[2]

    @pl.when(pl.program_id(1) == 0)
    def _():
        vt = vt_ref[0]
        row = lax.broadcasted_iota(jnp.int32, (PV_ROWS - HEAD_DIM, s), 0)
        ones_pad = jnp.where(row == 0, 1.0, 0.0).astype(BF16)
        kf = k_ref[0].astype(F32)
        lane = lax.broadcasted_iota(jnp.int32, kf.shape, 1)
        ksq = kf * kf
        for kv in range(N_KV_HEADS):
            vta_ref[kv] = jnp.concatenate([vt[kv * HEAD_DIM:(kv + 1) * HEAD_DIM], ones_pad], axis=0)
            in_head = (lane >= kv * HEAD_DIM) & (lane < (kv + 1) * HEAD_DIM)
            n2 = jnp.sum(jnp.where(in_head, ksq, 0.0), axis=1, keepdims=True)
            kn_ref[kv] = jnp.broadcast_to(jnp.sqrt(jnp.max(n2, axis=0, keepdims=True)), (1, LANES))

    qt = qt_ref[0]
    zeros = jnp.zeros((HEAD_DIM, tq), BF16)

    def q_operand(kv, heads=range(Q_PER_KV)):
        cols = []
        for j in heads:
            h = kv * Q_PER_KV + j
            qh = qt[h * HEAD_DIM:(h + 1) * HEAD_DIM]
            cols.append(jnp.concatenate([qh, zeros] if kv == 0 else [zeros, qh], axis=0))
        return jnp.concatenate(cols, axis=1)

    def finish(accs):
        outs = []
        for acc in accs:
            o = acc[:HEAD_DIM] / acc[HEAD_DIM:HEAD_DIM + 1]
            for j in range(acc.shape[1] // tq):
                outs.append(o[:, j * tq:(j + 1) * tq])
        ot = jnp.concatenate(outs, axis=0)
        y_ref[0] = ((ot * _rms(ot, 0)).T * grow_ref[...]).astype(BF16)

    def scores_fn(qst, chunk):
        return lambda c: _dot(k_ref[0, c * chunk:(c + 1) * chunk, :], qst)

    qf = qt.astype(F32).reshape(N_Q_HEADS, HEAD_DIM, tq)
    qn = jnp.sqrt(jnp.sum(qf * qf, axis=1))
    accs = []
    for kv in range(N_KV_HEADS):
        bound = qn[kv * Q_PER_KV:(kv + 1) * Q_PER_KV] * (kn_ref[kv][:, :1] * BOUND_SLACK)
        groups = [range(j0, j0 + FAST_HEADS) for j0 in range(0, Q_PER_KV, FAST_HEADS)]
        ms = [jnp.concatenate([bound[j:j + 1] for j in g], axis=1) for g in groups]
        scores = [scores_fn(q_operand(kv, g), FAST_CHUNK) for g in groups]
        n_chunks = s // FAST_CHUNK
        acc = [None] * len(groups)
        sc_next = [sf(0) for sf in scores]
        for c in range(n_chunks):
            for gi in range(len(groups)):
                sc = sc_next[gi]
                if c + 1 < n_chunks:
                    sc_next[gi] = scores[gi](c + 1)
                p = jnp.exp2(sc - ms[gi]).astype(BF16)
                pv = _dot(vta_ref[kv, :, c * FAST_CHUNK:(c + 1) * FAST_CHUNK], p)
                acc[gi] = pv if acc[gi] is None else acc[gi] + pv
        accs += acc
    finish(accs)
    den = accs[0][HEAD_DIM:HEAD_DIM + 1]
    for acc in accs[1:]:
        den = jnp.minimum(den, acc[HEAD_DIM:HEAD_DIM + 1])
    lmin = jnp.min(den)

    @pl.when(jnp.logical_not(lmin > L_MIN_OK))
    def _():
        accs = []
        for kv in range(N_KV_HEADS):
            scores = scores_fn(q_operand(kv), KEY_CHUNK)
            n_chunks = s // KEY_CHUNK
            m = None
            acc = None
            sc_next = scores(0)
            for c in range(n_chunks):
                sc = sc_next
                if c + 1 < n_chunks:
                    sc_next = scores(c + 1)
                mc = jnp.max(sc, axis=0, keepdims=True)
                m_new = mc if m is None else jnp.maximum(m, mc)
                p = jnp.exp2(sc - m_new).astype(BF16)
                pv = _dot(vta_ref[kv, :, c * KEY_CHUNK:(c + 1) * KEY_CHUNK], p)
                acc = pv if acc is None else acc * jnp.exp2(m - m_new) + pv
                m = m_new
            accs.append(acc)
        finish(accs)


def _attention(qt, k, vt, gcol):
    b, _, s = qt.shape
    return pl.pallas_call(
        _attn_kernel,
        grid=(b, s // Q_TILE),
        in_specs=[
            pl.BlockSpec((1, D_ATTN, Q_TILE), lambda i, j: (i, 0, j)),
            pl.BlockSpec((1, s, D_KV), lambda i, j: (i, 0, 0)),
            pl.BlockSpec((1, D_KV, s), lambda i, j: (i, 0, 0)),
            pl.BlockSpec((1, D_ATTN), lambda i, j: (0, 0)),
        ],
        out_specs=pl.BlockSpec((1, Q_TILE, D_ATTN), lambda i, j: (i, j, 0)),
        out_shape=jax.ShapeDtypeStruct((b, s, D_ATTN), BF16),
        scratch_shapes=[pltpu.VMEM((N_KV_HEADS, PV_ROWS, s), BF16),
                        pltpu.VMEM((N_KV_HEADS, 1, LANES), F32)],
        compiler_params=pltpu.CompilerParams(
            dimension_semantics=("arbitrary", "arbitrary"), vmem_limit_bytes=VMEM_LIMIT),
        name="attention",
    )(qt, k, vt, gcol)


def _shift_seg_down(x):
    row = lax.broadcasted_iota(jnp.int32, x.shape, 0)
    return jnp.where(row == 0, 0.0, pltpu.roll(x, 1, 0))


def _shift_seg_up(x):
    row = lax.broadcasted_iota(jnp.int32, x.shape, 0)
    return jnp.where(row == SUBLANES - 1, 0.0, pltpu.roll(x, SUBLANES - 1, 0))


def _sigmoid(x):
    return 0.5 * jnp.tanh(0.5 * x) + 0.5


def _gelu_tanh(x):
    k = 0.7978845608028654
    hx = 0.5 * x
    t = jnp.tanh(x * (k + (k * 0.044715) * (x * x)))
    return hx * t + hx


def _lru_kernel(xr_ref, cw_ref, cb_ref, wa_ref, wx_ref, ba_ref, bx_ref, lam_ref, y_ref,
                xh_ref, h0_ref, p0_ref, h1_ref, p1_ref, ystage_ref):
    t_len = xr_ref.shape[1]
    lanes = xr_ref.shape[3]
    n_chunks = t_len // GATE_CHUNK

    cw = cw_ref[0] * 0.5
    cb = cb_ref[...] * 0.5

    halo_lo = [_shift_seg_down(xr_ref[0, t_len - 2])[None], _shift_seg_down(xr_ref[0, t_len - 1])[None]]
    halo_hi = [_shift_seg_up(xr_ref[0, 0])[None]]

    def shifted(t0, n, k):
        lo = t0 + k - CONV_PAD_LEFT
        pieces = halo_lo[lo + CONV_PAD_LEFT:] if lo < 0 else []
        pieces = pieces + [xr_ref[0, max(lo, 0):min(lo + n, t_len)]]
        if lo + n > t_len:
            pieces = pieces + halo_hi[:lo + n - t_len]
        return pieces[0] if len(pieces) == 1 else jnp.concatenate(pieces, axis=0)

    for t0 in range(0, t_len, GATE_CHUNK):
        xh = cb[None]
        for kk in range(CONV_W):
            xh = xh + shifted(t0, GATE_CHUNK, kk) * cw[kk:kk + 1][None]
        xh_ref[t0:t0 + GATE_CHUNK] = xh
    lam = lam_ref[0]
    hc = (0.5 * LRU_C * 1.4426950408889634) * (
        jnp.minimum(lam, 0.0) - jnp.log1p(jnp.exp(-jnp.abs(lam))))

    zblk = jnp.zeros((LRU_BLOCK_W, LRU_BLOCK_W), F32)
    wg, bg_half = [], []
    for e in range(2):
        rows = []
        for i in range(lanes // LRU_BLOCK_W):
            blocks = []
            for w_ref in (wa_ref, wx_ref):
                blocks += [w_ref[0, e, i] if c == i else zblk for c in range(lanes // LRU_BLOCK_W)]
            rows.append(jnp.concatenate(blocks, axis=1))
        wg.append(jnp.concatenate(rows, axis=0).astype(BF16))
        bg_half.append(0.5 * jnp.concatenate([ba_ref[0, e:e + 1], bx_ref[0, e:e + 1]], axis=1))

    def gates(t0, e):
        xh = xh_ref[t0:t0 + SCAN_PART].reshape(SCAN_PART * SUBLANES, lanes)
        th = jnp.tanh(_dot(xh.astype(BF16), wg[e]) + bg_half[e])
        tr = th[:, :lanes]
        ti = th[:, lanes:]
        a = jnp.exp2(tr * hc[e:e + 1] + hc[e:e + 1])
        y = 1.0 - a * a
        mult = y * lax.rsqrt(jnp.maximum(y, 1e-30))
        u = (ti * xh + xh) * mult
        return a, u

    zero = jnp.zeros((SUBLANES, lanes), F32)
    one = jnp.ones((SUBLANES, lanes), F32)
    e0, q0, e1, q1 = zero, one, zero, one
    for part in range(t_len // SCAN_PART):
        t0 = part * SCAN_PART
        a, u = gates(t0, 0)
        for t in range(SCAN_PART):
            rows = slice(t * SUBLANES, (t + 1) * SUBLANES)
            e0 = a[rows] * e0 + u[rows]
            q0 = a[rows] * q0
            h0_ref[t0 + t] = e0
            p0_ref[t0 + t] = q0
        t1 = t_len - (part + 1) * SCAN_PART
        a, u = gates(t1, 1)
        for t in reversed(range(SCAN_PART)):
            rows = slice(t * SUBLANES, (t + 1) * SUBLANES)
            e1 = a[rows] * e1 + u[rows]
            q1 = a[rows] * q1
            h1_ref[t1 + t] = e1
            p1_ref[t1 + t] = q1

    c0 = zero
    c1 = zero
    for _ in range(N_SEG - 1):
        c0 = _shift_seg_down(e0 + q0 * c0)
        c1 = _shift_seg_up(e1 + q1 * c1)

    def out_body(i, carry):
        t0 = pl.multiple_of(i * GATE_CHUNK, GATE_CHUNK)
        sl = pl.ds(t0, GATE_CHUNK)
        h = (h0_ref[sl] + p0_ref[sl] * c0[None]) + (h1_ref[sl] + p1_ref[sl] * c1[None])
        ystage_ref[...] = h.reshape(GATE_CHUNK * SUBLANES, lanes)
        for seg in range(N_SEG):
            y_ref[0, pl.ds(seg * t_len + t0, GATE_CHUNK)] = (
                ystage_ref[pl.ds(seg, GATE_CHUNK, stride=SUBLANES)])
        return carry

    lax.fori_loop(0, n_chunks, out_body, 0)


def _lru(xr_il, conv_w, conv_b, wa, wx, ba, bx, lam):
    b, t_len, n_seg, d = xr_il.shape
    n_groups = d // LANES
    blocks_per_group = LANES // LRU_BLOCK_W
    slab = pltpu.VMEM((t_len, n_seg, LANES), F32)
    w_spec = pl.BlockSpec((1, 2, blocks_per_group, LRU_BLOCK_W, LRU_BLOCK_W),
                          lambda i, j: (0, 0, j, 0, 0))
    lane_group = pl.BlockSpec((1, 2, LANES), lambda i, j: (0, 0, j))
    return pl.pallas_call(
        _lru_kernel,
        grid=(b, n_groups),
        in_specs=[
            pl.BlockSpec((1, t_len, n_seg, LANES), lambda i, j: (i, 0, 0, j)),
            pl.BlockSpec((1, CONV_W, LANES), lambda i, j: (0, 0, j)),
            pl.BlockSpec((1, LANES), lambda i, j: (0, j)),
            w_spec, w_spec, lane_group, lane_group, lane_group,
        ],
        out_specs=pl.BlockSpec((1, t_len * n_seg, LANES), lambda i, j: (i, 0, j)),
        out_shape=jax.ShapeDtypeStruct((b, t_len * n_seg, d), F32),
        scratch_shapes=[slab] * 5 + [
            pltpu.VMEM((GATE_CHUNK * n_seg, LANES), F32)],
        compiler_params=pltpu.CompilerParams(
            dimension_semantics=("arbitrary", "arbitrary"), vmem_limit_bytes=VMEM_LIMIT),
        name="rglru",
    )(xr_il, conv_w, conv_b, wa, wx, ba, bx, lam)


def _tail_kernel(x_ref, ya_ref, hl_ref, xg_ref, p_ref, glru_ref, wout_ref, gmlp_ref, wup_ref,
                 wdown_ref, gple_ref, wgate_ref, wproj_ref, gfin_ref, o_ref):
    t = x_ref.shape[1]
    groups = [slice(i * t // TAIL_SPLIT, (i + 1) * t // TAIL_SPLIT) for i in range(TAIL_SPLIT)]
    h = []
    for r in groups:
        yl = hl_ref[0, r] * _gelu_tanh(xg_ref[0, r])
        yl_n = (yl * _rms(yl, -1) * glru_ref[...]).astype(BF16)
        h.append(x_ref[0, r] + _dot(ya_ref[0, r], wout_ref[:D_ATTN]) + _dot(yl_n, wout_ref[D_ATTN:]))
    act = []
    for i in range(TAIL_SPLIT):
        hn = (h[i] * _rms(h[i], -1) * gmlp_ref[...]).astype(BF16)
        act.append(jnp.square(jnp.maximum(_dot(hn, wup_ref[...]), 0.0)).astype(BF16))
    for i in range(TAIL_SPLIT):
        h[i] = h[i] + _dot(act[i], wdown_ref[...])
    pp = [_dot(p_ref[0, 0, r].astype(BF16), wproj_ref[...]) for r in groups]
    for i, r in enumerate(groups):
        hn = (h[i] * _rms(h[i], -1) * gple_ref[...]).astype(BF16)
        hf = h[i] + _sigmoid(_dot(hn, wgate_ref[...])) * pp[i]
        o_ref[0, r] = hf * _rms(hf, -1) * gfin_ref[...]


def _tail(x, ya, hl, xg_il, p, glru, wout, gmlp, wup, wdown, gple, wgate, wproj, gfin):
    b, s, d = x.shape
    t = s // N_SEG
    const = lambda *_: (0, 0)

    def resident(arr):
        return pl.BlockSpec(arr.shape, const, pipeline_mode=pl.Buffered(1))

    return pl.pallas_call(
        _tail_kernel,
        grid=(b, N_SEG),
        in_specs=[
            pl.BlockSpec((1, t, d), lambda i, j: (i, j, 0)),
            pl.BlockSpec((1, t, D_ATTN), lambda i, j: (i, j, 0)),
            pl.BlockSpec((1, t, D_LRU), lambda i, j: (i, j, 0)),
            pl.BlockSpec((1, t, D_LRU), lambda i, j: (i, 0, j)),
            pl.BlockSpec((1, 1, t, D_PLE), lambda i, j: (0, i, j, 0)),
            resident(glru), resident(wout), resident(gmlp), resident(wup), resident(wdown),
            resident(gple), resident(wgate), resident(wproj), resident(gfin),
        ],
        out_specs=pl.BlockSpec((1, t, d), lambda i, j: (i, j, 0)),
        out_shape=jax.ShapeDtypeStruct((b, s, d), F32),
        compiler_params=pltpu.CompilerParams(
            dimension_semantics=("arbitrary", "arbitrary"), vmem_limit_bytes=VMEM_LIMIT),
        name="tail",
    )(x, ya, hl, xg_il, p, glru, wout, gmlp, wup, wdown, gple, wgate, wproj, gfin)


def _rope_tables_t(seq_len):
    pos = np.arange(seq_len)
    row = (pos // GRID_W).astype(np.float32)
    col = (pos % GRID_W).astype(np.float32)
    inv_freq = np.float32(ROPE_THETA) ** (-np.arange(N_FREQ, dtype=np.float32) / np.float32(N_FREQ))
    ang_r = (row[None, :] * inv_freq[:, None]).astype(np.float32)
    ang_c = (col[None, :] * inv_freq[:, None]).astype(np.float32)
    cos_t = np.concatenate([np.cos(ang_r)] * 2 + [np.cos(ang_c)] * 2, axis=0)
    sin_t = np.concatenate([-np.sin(ang_r), np.sin(ang_r), -np.sin(ang_c), np.sin(ang_c)], axis=0)
    return jnp.asarray(cos_t, F32), jnp.asarray(sin_t, F32)


def kernel(x, p, mix_norm, w_in, q_norm, k_norm, conv_w, conv_b, lru_wa, lru_ba, lru_wx, lru_bx,
           lru_lambda, attn_out_norm, lru_out_norm, w_out, mlp_norm, w_up, w_down, ple_norm,
           w_ple_gate, w_ple_proj, final_norm):
    b, s, d = x.shape
    assert w_in.shape[0] == 1, "single-layer trunk: the final norm is fused into the layer tail"
    cos_t, sin_t = _rope_tables_t(s)
    wqkv_t = w_in[0, :, :D_ATTN + 2 * D_KV].T.astype(BF16)
    wrg = w_in[0, :, D_ATTN + 2 * D_KV:].astype(BF16)
    qt, k, vt, xr, xg = _in_proj(x, mix_norm, wqkv_t, wrg, q_norm, k_norm, cos_t, sin_t)
    ya = _attention(qt, k, vt, attn_out_norm)
    hl = _lru(xr.reshape(b, s // N_SEG, N_SEG, D_LRU), conv_w, conv_b, lru_wa, lru_wx, lru_ba, lru_bx,
              lru_lambda)
    return _tail(x, ya, hl, xg, p, lru_out_norm, w_out[0].astype(BF16), mlp_norm,
                 w_up[0].astype(BF16), w_down[0].astype(BF16), ple_norm,
                 w_ple_gate[0].astype(BF16), w_ple_proj[0].astype(BF16), final_norm.reshape(1, -1))
```

```python
import jax
import jax.numpy as jnp
import numpy as np
from jax import lax
from jax.experimental import pallas as pl
from jax.experimental.pallas import tpu as pltpu

D_MODEL = 1024
GRID_W = 64
HEAD_DIM = 64
D_ATTN = 512
N_Q_HEADS = 8
N_KV_HEADS = 2
Q_PER_KV = 4
D_KV = 128
ROPE_THETA = 10000.0
N_FREQ = 16
D_LRU = 512
LRU_BLOCK_W = 64
LRU_C = 8.0
CONV_W = 4
CONV_PAD_LEFT = 2
D_FF = 4096
D_PLE = 256
NORM_EPS = 1e-6

SUBLANES = 8
LANES = 128

N_SEG = SUBLANES
Q_TILE = 256
KEY_CHUNK = 256
FAST_CHUNK = 256
FAST_HEADS = 1
PV_ROWS = 80
BOUND_SLACK = 1.001
L_MIN_OK = 2.0 ** -80
GATE_CHUNK = 64
SCAN_PART = 128
IN_SPLIT = 2
WIDE_STAGE_ROWS = 128
NARROW_STAGE_ROWS = 512
VMEM_LIMIT = 56 * 1024 * 1024
Q_SCALE = HEAD_DIM ** -0.5 * 1.4426950408889634

F32 = jnp.float32
BF16 = jnp.bfloat16


def _rms(x, axis):
    return lax.rsqrt(jnp.mean(x * x, axis=axis, keepdims=True) + NORM_EPS)


def _dot(a, b):
    return jnp.dot(a, b, preferred_element_type=F32)


def _as_column(row):
    n = row.shape[1]
    ri = lax.broadcasted_iota(jnp.int32, (n, n), 0)
    ci = lax.broadcasted_iota(jnp.int32, (n, n), 1)
    return jnp.sum(jnp.where(ri == ci, row, 0.0), axis=1, keepdims=True)


def _dot_nt(a, b):
    return lax.dot_general(a, b, (((1,), (1,)), ((), ())), preferred_element_type=F32)


def _norm_rope_t(xt, gcol, cos_t, sin_t, n_heads):
    t = xt.shape[1]
    x3 = xt.reshape(n_heads, HEAD_DIM, t)
    xn = x3 * _rms(x3, 1) * gcol[None]
    x5 = xn.reshape(n_heads * 2, 2, N_FREQ, t)
    xs = jnp.concatenate([x5[:, 1:2], x5[:, 0:1]], axis=1).reshape(n_heads, HEAD_DIM, t)
    out = xn * cos_t[None] + xs * sin_t[None]
    return out.reshape(n_heads * HEAD_DIM, t)


def _in_proj_kernel(x_ref, gmix_ref, wqkv_t_ref, wrg_ref, gq_ref, gk_ref, cos_ref, sin_ref,
                    qt_ref, k_ref, vt_ref, xr_ref, xg_ref):
    gq = _as_column(gq_ref[...]) * Q_SCALE
    gk = _as_column(gk_ref[...])
    t = x_ref.shape[1]
    groups = [slice(i * t // IN_SPLIT, (i + 1) * t // IN_SPLIT) for i in range(IN_SPLIT)]
    hns = []
    for r in groups:
        x = x_ref[0, r]
        hns.append((x * _rms(x, -1) * gmix_ref[...]).astype(BF16))
    lru_proj = lambda hn: _dot(hn, wrg_ref[...])
    qkv_proj = lambda hn: _dot_nt(wqkv_t_ref[...], hn)
    zrs = [lru_proj(hn) for hn in hns[:-1]]
    zts = [qkv_proj(hn) for hn in hns]
    zrs.append(lru_proj(hns[-1]))
    for r, zt, zr in zip(groups, zts, zrs):
        cos_t = cos_ref[:, r]
        sin_t = sin_ref[:, r]
        qt = _norm_rope_t(zt[:D_ATTN], gq, cos_t, sin_t, N_Q_HEADS)
        kt = _norm_rope_t(zt[D_ATTN:D_ATTN + D_KV], gk, cos_t, sin_t, N_KV_HEADS)
        qt_ref[0, :, r] = qt.astype(BF16)
        k_ref[0, r] = kt.T.astype(BF16)
        vt_ref[0, :, r] = zt[D_ATTN + D_KV:].astype(BF16)
        xr_ref[0, r] = zr[:, :D_LRU]
        xg_ref[0, r] = zr[:, D_LRU:]


def _in_proj(x, gmix, wqkv_t, wrg, gq, gk, cos_t, sin_t):
    b, s, d = x.shape
    t = s // N_SEG
    const = lambda *_: (0, 0)
    return pl.pallas_call(
        _in_proj_kernel,
        grid=(b, N_SEG),
        in_specs=[
            pl.BlockSpec((1, t, d), lambda i, j: (i, j, 0)),
            pl.BlockSpec((1, d), const),
            pl.BlockSpec(wqkv_t.shape, const),
            pl.BlockSpec(wrg.shape, const),
            pl.BlockSpec((1, HEAD_DIM), const),
            pl.BlockSpec((1, HEAD_DIM), const),
            pl.BlockSpec((HEAD_DIM, t), lambda i, j: (0, j)),
            pl.BlockSpec((HEAD_DIM, t), lambda i, j: (0, j)),
        ],
        out_specs=[
            pl.BlockSpec((1, D_ATTN, t), lambda i, j: (i, 0, j)),
            pl.BlockSpec((1, t, D_KV), lambda i, j: (i, j, 0)),
            pl.BlockSpec((1, D_KV, t), lambda i, j: (i, 0, j)),
            pl.BlockSpec((1, t, D_LRU), lambda i, j: (i, 0, j)),
            pl.BlockSpec((1, t, D_LRU), lambda i, j: (i, 0, j)),
        ],
        out_shape=[
            jax.ShapeDtypeStruct((b, D_ATTN, s), BF16),
            jax.ShapeDtypeStruct((b, s, D_KV), BF16),
            jax.ShapeDtypeStruct((b, D_KV, s), BF16),
            jax.ShapeDtypeStruct((b, t, N_SEG * D_LRU), F32),
            jax.ShapeDtypeStruct((b, t, N_SEG * D_LRU), F32),
        ],
        compiler_params=pltpu.CompilerParams(
            dimension_semantics=("arbitrary", "arbitrary"), vmem_limit_bytes=VMEM_LIMIT),
        name="in_proj",
    )(x, gmix, wqkv_t, wrg, gq, gk, cos_t, sin_t)


def _attn_kernel(qt_ref, k_ref, vt_ref, grow_ref, y_ref, vta_ref, kn_ref):
    s = k_ref.shape[1]
    tq = qt_ref.shape[2]

    @pl.when(pl.program_id(1) == 0)
    def _():
        vt = vt_ref[0]
        row = lax.broadcasted_iota(jnp.int32, (PV_ROWS - HEAD_DIM, s), 0)
        ones_pad = jnp.where(row == 0, 1.0, 0.0).astype(BF16)
        kf = k_ref[0].astype(F32)
        lane = lax.broadcasted_iota(jnp.int32, kf.shape, 1)
        ksq = kf * kf
        for kv in range(N_KV_HEADS):
            vta_ref[kv] = jnp.concatenate([vt[kv * HEAD_DIM:(kv + 1) * HEAD_DIM], ones_pad], axis=0)
            in_head = (lane >= kv * HEAD_DIM) & (lane < (kv + 1) * HEAD_DIM)
            n2 = jnp.sum(jnp.where(in_head, ksq, 0.0), axis=1, keepdims=True)
            kn_ref[kv] = jnp.broadcast_to(jnp.sqrt(jnp.max(n2, axis=0, keepdims=True)), (1, LANES))

    qt = qt_ref[0]
    zeros = jnp.zeros((HEAD_DIM, tq), BF16)

    def q_operand(kv, heads=range(Q_PER_KV)):
        cols = []
        for j in heads:
            h = kv * Q_PER_KV + j
            qh = qt[h * HEAD_DIM:(h + 1) * HEAD_DIM]
            cols.append(jnp.concatenate([qh, zeros] if kv == 0 else [zeros, qh], axis=0))
        return jnp.concatenate(cols, axis=1)

    def finish(accs):
        outs = []
        for acc in accs:
            o = acc[:HEAD_DIM] / acc[HEAD_DIM:HEAD_DIM + 1]
            for j in range(acc.shape[1] // tq):
                outs.append(o[:, j * tq:(j + 1) * tq])
        ot = jnp.concatenate(outs, axis=0)
        y_ref[0] = ((ot * _rms(ot, 0)).T * grow_ref[...]).astype(BF16)

    def scores_fn(qst, chunk):
        return lambda c: _dot(k_ref[0, c * chunk:(c + 1) * chunk, :], qst)

    qf = qt.astype(F32).reshape(N_Q_HEADS, HEAD_DIM, tq)
    qn = jnp.sqrt(jnp.sum(qf * qf, axis=1))
    accs = []
    for kv in range(N_KV_HEADS):
        bound = qn[kv * Q_PER_KV:(kv + 1) * Q_PER_KV] * (kn_ref[kv][:, :1] * BOUND_SLACK)
        groups = [range(j0, j0 + FAST_HEADS) for j0 in range(0, Q_PER_KV, FAST_HEADS)]
        ms = [jnp.concatenate([bound[j:j + 1] for j in g], axis=1) for g in groups]
        scores = [scores_fn(q_operand(kv, g), FAST_CHUNK) for g in groups]
        n_chunks = s // FAST_CHUNK
        acc = [None] * len(groups)
        sc_next = [sf(0) for sf in scores]
        for c in range(n_chunks):
            for gi in range(len(groups)):
                sc = sc_next[gi]
                if c + 1 < n_chunks:
                    sc_next[gi] = scores[gi](c + 1)
                p = jnp.exp2(sc - ms[gi]).astype(BF16)
                pv = _dot(vta_ref[kv, :, c * FAST_CHUNK:(c + 1) * FAST_CHUNK], p)
                acc[gi] = pv if acc[gi] is None else acc[gi] + pv
        accs += acc
    finish(accs)
    den = accs[0][HEAD_DIM:HEAD_DIM + 1]
    for acc in accs[1:]:
        den = jnp.minimum(den, acc[HEAD_DIM:HEAD_DIM + 1])
    lmin = jnp.min(den)

    @pl.when(jnp.logical_not(lmin > L_MIN_OK))
    def _():
        accs = []
        for kv in range(N_KV_HEADS):
            scores = scores_fn(q_operand(kv), KEY_CHUNK)
            n_chunks = s // KEY_CHUNK
            m = None
            acc = None
            sc_next = scores(0)
            for c in range(n_chunks):
                sc = sc_next
                if c + 1 < n_chunks:
                    sc_next = scores(c + 1)
                mc = jnp.max(sc, axis=0, keepdims=True)
                m_new = mc if m is None else jnp.maximum(m, mc)
                p = jnp.exp2(sc - m_new).astype(BF16)
                pv = _dot(vta_ref[kv, :, c * KEY_CHUNK:(c + 1) * KEY_CHUNK], p)
                acc = pv if acc is None else acc * jnp.exp2(m - m_new) + pv
                m = m_new
            accs.append(acc)
        finish(accs)


def _attention(qt, k, vt, gcol):
    b, _, s = qt.shape
    return pl.pallas_call(
        _attn_kernel,
        grid=(b, s // Q_TILE),
        in_specs=[
            pl.BlockSpec((1, D_ATTN, Q_TILE), lambda i, j: (i, 0, j)),
            pl.BlockSpec((1, s, D_KV), lambda i, j: (i, 0, 0)),
            pl.BlockSpec((1, D_KV, s), lambda i, j: (i, 0, 0)),
            pl.BlockSpec((1, D_ATTN), lambda i, j: (0, 0)),
        ],
        out_specs=pl.BlockSpec((1, Q_TILE, D_ATTN), lambda i, j: (i, j, 0)),
        out_shape=jax.ShapeDtypeStruct((b, s, D_ATTN), BF16),
        scratch_shapes=[pltpu.VMEM((N_KV_HEADS, PV_ROWS, s), BF16),
                        pltpu.VMEM((N_KV_HEADS, 1, LANES), F32)],
        compiler_params=pltpu.CompilerParams(
            dimension_semantics=("arbitrary", "arbitrary"), vmem_limit_bytes=VMEM_LIMIT),
        name="attention",
    )(qt, k, vt, gcol)


def _shift_seg_down(x):
    row = lax.broadcasted_iota(jnp.int32, x.shape, 0)
    return jnp.where(row == 0, 0.0, pltpu.roll(x, 1, 0))


def _shift_seg_up(x):
    row = lax.broadcasted_iota(jnp.int32, x.shape, 0)
    return jnp.where(row == SUBLANES - 1, 0.0, pltpu.roll(x, SUBLANES - 1, 0))


def _sigmoid(x):
    return 0.5 * jnp.tanh(0.5 * x) + 0.5


def _gelu_tanh(x):
    k = 0.7978845608028654
    hx = 0.5 * x
    t = jnp.tanh(x * (k + (k * 0.044715) * (x * x)))
    return hx * t + hx


def _lru_kernel(xr_ref, cw_ref, cb_ref, wa_ref, wx_ref, ba_ref, bx_ref, lam_ref, y_ref,
                xh_ref, h0_ref, p0_ref, h1_ref, p1_ref, ystage_ref):
    t_len = xr_ref.shape[1]
    lanes = xr_ref.shape[3]
    n_chunks = t_len // GATE_CHUNK

    cw = cw_ref[0] * 0.5
    cb = cb_ref[...] * 0.5

    halo_lo = [_shift_seg_down(xr_ref[0, t_len - 2])[None], _shift_seg_down(xr_ref[0, t_len - 1])[None]]
    halo_hi = [_shift_seg_up(xr_ref[0, 0])[None]]

    def shifted(t0, n, k):
        lo = t0 + k - CONV_PAD_LEFT
        pieces = halo_lo[lo + CONV_PAD_LEFT:] if lo < 0 else []
        pieces = pieces + [xr_ref[0, max(lo, 0):min(lo + n, t_len)]]
        if lo + n > t_len:
            pieces = pieces + halo_hi[:lo + n - t_len]
        return pieces[0] if len(pieces) == 1 else jnp.concatenate(pieces, axis=0)

    for t0 in range(0, t_len, GATE_CHUNK):
        xh = cb[None]
        for kk in range(CONV_W):
            xh = xh + shifted(t0, GATE_CHUNK, kk) * cw[kk:kk + 1][None]
        xh_ref[t0:t0 + GATE_CHUNK] = xh
    lam = lam_ref[0]
    hc = (0.5 * LRU_C * 1.4426950408889634) * (
        jnp.minimum(lam, 0.0) - jnp.log1p(jnp.exp(-jnp.abs(lam))))

    zblk = jnp.zeros((LRU_BLOCK_W, LRU_BLOCK_W), F32)
    wg, bg_half = [], []
    for e in range(2):
        rows = []
        for i in range(lanes // LRU_BLOCK_W):
            blocks = []
            for w_ref in (wa_ref, wx_ref):
                blocks += [w_ref[0, e, i] if c == i else zblk for c in range(lanes // LRU_BLOCK_W)]
            rows.append(jnp.concatenate(blocks, axis=1))
        wg.append(jnp.concatenate(rows, axis=0).astype(BF16))
        bg_half.append(0.5 * jnp.concatenate([ba_ref[0, e:e + 1], bx_ref[0, e:e + 1]], axis=1))

    def gates(t0, e):
        xh = xh_ref[t0:t0 + SCAN_PART].reshape(SCAN_PART * SUBLANES, lanes)
        th = jnp.tanh(_dot(xh.astype(BF16), wg[e]) + bg_half[e])
        tr = th[:, :lanes]
        ti = th[:, lanes:]
        a = jnp.exp2(tr * hc[e:e + 1] + hc[e:e + 1])
        y = 1.0 - a * a
        mult = y * lax.rsqrt(jnp.maximum(y, 1e-30))
        u = (ti * xh + xh) * mult
        return a, u

    zero = jnp.zeros((SUBLANES, lanes), F32)
    one = jnp.ones((SUBLANES, lanes), F32)
    e0, q0, e1, q1 = zero, one, zero, one
    for part in range(t_len // SCAN_PART):
        t0 = part * SCAN_PART
        a, u = gates(t0, 0)
        for t in range(SCAN_PART):
            rows = slice(t * SUBLANES, (t + 1) * SUBLANES)
            e0 = a[rows] * e0 + u[rows]
            q0 = a[rows] * q0
            h0_ref[t0 + t] = e0
            p0_ref[t0 + t] = q0
        t1 = t_len - (part + 1) * SCAN_PART
        a, u = gates(t1, 1)
        for t in reversed(range(SCAN_PART)):
            rows = slice(t * SUBLANES, (t + 1) * SUBLANES)
            e1 = a[rows] * e1 + u[rows]
            q1 = a[rows] * q1
            h1_ref[t1 + t] = e1
            p1_ref[t1 + t] = q1

    c0 = zero
    c1 = zero
    for _ in range(N_SEG - 1):
        c0 = _shift_seg_down(e0 + q0 * c0)
        c1 = _shift_seg_up(e1 + q1 * c1)

    def out_body(i, carry):
        t0 = pl.multiple_of(i * GATE_CHUNK, GATE_CHUNK)
        sl = pl.ds(t0, GATE_CHUNK)
        h = (h0_ref[sl] + p0_ref[sl] * c0[None]) + (h1_ref[sl] + p1_ref[sl] * c1[None])
        ystage_ref[...] = h.reshape(GATE_CHUNK * SUBLANES, lanes)
        for seg in range(N_SEG):
            y_ref[0, pl.ds(seg * t_len + t0, GATE_CHUNK)] = (
                ystage_ref[pl.ds(seg, GATE_CHUNK, stride=SUBLANES)])
        return carry

    lax.fori_loop(0, n_chunks, out_body, 0)


def _lru(xr_il, conv_w, conv_b, wa, wx, ba, bx, lam):
    b, t_len, n_seg, d = xr_il.shape
    n_groups = d // LANES
    blocks_per_group = LANES // LRU_BLOCK_W
    slab = pltpu.VMEM((t_len, n_seg, LANES), F32)
    w_spec = pl.BlockSpec((1, 2, blocks_per_group, LRU_BLOCK_W, LRU_BLOCK_W),
                          lambda i, j: (0, 0, j, 0, 0))
    lane_group = pl.BlockSpec((1, 2, LANES), lambda i, j: (0, 0, j))
    return pl.pallas_call(
        _lru_kernel,
        grid=(b, n_groups),
        in_specs=[
            pl.BlockSpec((1, t_len, n_seg, LANES), lambda i, j: (i, 0, 0, j)),
            pl.BlockSpec((1, CONV_W, LANES), lambda i, j: (0, 0, j)),
            pl.BlockSpec((1, LANES), lambda i, j: (0, j)),
            w_spec, w_spec, lane_group, lane_group, lane_group,
        ],
        out_specs=pl.BlockSpec((1, t_len * n_seg, LANES), lambda i, j: (i, 0, j)),
        out_shape=jax.ShapeDtypeStruct((b, t_len * n_seg, d), F32),
        scratch_shapes=[slab] * 5 + [
            pltpu.VMEM((GATE_CHUNK * n_seg, LANES), F32)],
        compiler_params=pltpu.CompilerParams(
            dimension_semantics=("arbitrary", "arbitrary"), vmem_limit_bytes=VMEM_LIMIT),
        name="rglru",
    )(xr_il, conv_w, conv_b, wa, wx, ba, bx, lam)


def _stage_weights(jobs, sems):
    uses = {}
    copies = []
    for src, buf, stage, dst in jobs:
        slot = uses.get(buf, 0) % 2
        uses[buf] = uses.get(buf, 0) + 1
        copies.append((pltpu.make_async_copy(src, stage.at[slot], sems.at[buf, slot]), stage, slot, dst))
    copies[0][0].start()
    for k, (copy, stage, slot, dst) in enumerate(copies):
        if k + 1 < len(copies):
            copies[k + 1][0].start()
        copy.wait()
        dst[...] = stage[slot].astype(BF16)


def _tail_kernel(x_ref, ya_ref, hl_ref, xg_ref, p_ref, glru_ref, gmlp_ref, gple_ref, gfin_ref,
                 wout_hbm, wup_hbm, wdown_hbm, wgate_hbm, wproj_hbm, o_ref,
                 wout_ref, wup_ref, wdown_ref, wgate_ref, wproj_ref, wide_ref, narrow_ref, sems):
    @pl.when((pl.program_id(0) == 0) & (pl.program_id(1) == 0))
    def _():
        jobs = []
        for w_hbm, w_ref in ((wout_hbm, wout_ref), (wup_hbm, wup_ref), (wdown_hbm, wdown_ref),
                             (wgate_hbm, wgate_ref), (wproj_hbm, wproj_ref)):
            n_rows, n_cols = w_ref.shape
            buf, stage_ref = (0, wide_ref) if n_cols == wide_ref.shape[2] else (1, narrow_ref)
            rows = min(n_rows, stage_ref.shape[1])
            for r0 in range(0, n_rows, rows):
                jobs.append((w_hbm.at[0, pl.ds(r0, rows)], buf, stage_ref.at[:, pl.ds(0, rows)],
                             w_ref.at[pl.ds(r0, rows)]))
        _stage_weights(jobs, sems)

    x = x_ref[0]
    yl = hl_ref[0] * _gelu_tanh(xg_ref[0])
    yl_n = (yl * _rms(yl, -1) * glru_ref[...]).astype(BF16)
    h = x + _dot(ya_ref[0], wout_ref[:D_ATTN]) + _dot(yl_n, wout_ref[D_ATTN:])
    hn = (h * _rms(h, -1) * gmlp_ref[...]).astype(BF16)
    m = _dot(hn, wup_ref[...])
    act = jnp.square(jnp.maximum(m, 0.0)).astype(BF16)
    h = h + _dot(act, wdown_ref[...])
    hn = (h * _rms(h, -1) * gple_ref[...]).astype(BF16)
    gate = _sigmoid(_dot(hn, wgate_ref[...]))
    h = h + gate * _dot(p_ref[0, 0].astype(BF16), wproj_ref[...])
    o_ref[0] = h * _rms(h, -1) * gfin_ref[...]


def _tail(x, ya, hl, xg_il, p, glru, gmlp, gple, gfin, wout, wup, wdown, wgate, wproj):
    b, s, d = x.shape
    t = s // N_SEG
    const = lambda *_: (0, 0)
    weights = (wout, wup, wdown, wgate, wproj)

    def resident(arr):
        return pl.BlockSpec(arr.shape, const, pipeline_mode=pl.Buffered(1))

    return pl.pallas_call(
        _tail_kernel,
        grid=(b, N_SEG),
        in_specs=[
            pl.BlockSpec((1, t, d), lambda i, j: (i, j, 0)),
            pl.BlockSpec((1, t, D_ATTN), lambda i, j: (i, j, 0)),
            pl.BlockSpec((1, t, D_LRU), lambda i, j: (i, j, 0)),
            pl.BlockSpec((1, t, D_LRU), lambda i, j: (i, 0, j)),
            pl.BlockSpec((1, 1, t, D_PLE), lambda i, j: (0, i, j, 0)),
            resident(glru), resident(gmlp), resident(gple), resident(gfin),
        ] + [pl.BlockSpec(memory_space=pl.ANY)] * len(weights),
        out_specs=pl.BlockSpec((1, t, d), lambda i, j: (i, j, 0)),
        out_shape=jax.ShapeDtypeStruct((b, s, d), F32),
        scratch_shapes=[pltpu.VMEM(w.shape[1:], BF16) for w in weights] + [
            pltpu.VMEM((2, WIDE_STAGE_ROWS, D_FF), F32),
            pltpu.VMEM((2, NARROW_STAGE_ROWS, D_MODEL), F32),
            pltpu.SemaphoreType.DMA((2, 2)),
        ],
        compiler_params=pltpu.CompilerParams(
            dimension_semantics=("arbitrary", "arbitrary"), vmem_limit_bytes=VMEM_LIMIT),
        name="tail",
    )(x, ya, hl, xg_il, p, glru, gmlp, gple, gfin, *weights)


def _rope_tables_t(seq_len):
    pos = np.arange(seq_len)
    row = (pos // GRID_W).astype(np.float32)
    col = (pos % GRID_W).astype(np.float32)
    inv_freq = np.float32(ROPE_THETA) ** (-np.arange(N_FREQ, dtype=np.float32) / np.float32(N_FREQ))
    ang_r = (row[None, :] * inv_freq[:, None]).astype(np.float32)
    ang_c = (col[None, :] * inv_freq[:, None]).astype(np.float32)
    cos_t = np.concatenate([np.cos(ang_r)] * 2 + [np.cos(ang_c)] * 2, axis=0)
    sin_t = np.concatenate([-np.sin(ang_r), np.sin(ang_r), -np.sin(ang_c), np.sin(ang_c)], axis=0)
    return jnp.asarray(cos_t, F32), jnp.asarray(sin_t, F32)


def kernel(x, p, mix_norm, w_in, q_norm, k_norm, conv_w, conv_b, lru_wa, lru_ba, lru_wx, lru_bx,
           lru_lambda, attn_out_norm, lru_out_norm, w_out, mlp_norm, w_up, w_down, ple_norm,
           w_ple_gate, w_ple_proj, final_norm):
    b, s, d = x.shape
    assert w_in.shape[0] == 1, "single-layer trunk: the final norm is fused into the layer tail"
    cos_t, sin_t = _rope_tables_t(s)
    wqkv_t = w_in[0, :, :D_ATTN + 2 * D_KV].T.astype(BF16)
    wrg = w_in[0, :, D_ATTN + 2 * D_KV:].astype(BF16)
    qt, k, vt, xr, xg = _in_proj(x, mix_norm, wqkv_t, wrg, q_norm, k_norm, cos_t, sin_t)
    ya = _attention(qt, k, vt, attn_out_norm)
    hl = _lru(xr.reshape(b, s // N_SEG, N_SEG, D_LRU), conv_w, conv_b, lru_wa, lru_wx, lru_ba, lru_bx,
              lru_lambda)
    return _tail(x, ya, hl, xg, p, lru_out_norm, mlp_norm, ple_norm, final_norm.reshape(1, -1),
                 w_out, w_up, w_down, w_ple_gate, w_ple_proj)
```

```python
import jax
import jax.numpy as jnp
import numpy as np
from jax import lax
from jax.experimental import pallas as pl
from jax.experimental.pallas import tpu as pltpu

D_MODEL = 1024
GRID_W = 64
HEAD_DIM = 64
D_ATTN = 512
N_Q_HEADS = 8
N_KV_HEADS = 2
Q_PER_KV = 4
D_KV = 128
ROPE_THETA = 10000.0
N_FREQ = 16
D_LRU = 512
LRU_BLOCK_W = 64
LRU_C = 8.0
CONV_W = 4
CONV_PAD_LEFT = 2
D_FF = 4096
D_PLE = 256
NORM_EPS = 1e-6

SUBLANES = 8
BF16_SUBLANES = 16
LANES = 128

N_SEG = SUBLANES
Q_TILE = 256
KEY_CHUNK = 256
FAST_CHUNK = 256
FAST_HEADS = 1
PV_ROWS = 80
BOUND_SLACK = 1.001
L_MIN_OK = 2.0 ** -80
GATE_CHUNK = 64
SCAN_PART = 128
IN_SPLIT = 2
VMEM_LIMIT = 56 * 1024 * 1024
Q_SCALE = HEAD_DIM ** -0.5 * 1.4426950408889634

F32 = jnp.float32
BF16 = jnp.bfloat16


def _rms(x, axis):
    return lax.rsqrt(jnp.mean(x * x, axis=axis, keepdims=True) + NORM_EPS)


def _dot(a, b):
    return jnp.dot(a, b, preferred_element_type=F32)


def _as_column(row):
    n = row.shape[1]
    ri = lax.broadcasted_iota(jnp.int32, (n, n), 0)
    ci = lax.broadcasted_iota(jnp.int32, (n, n), 1)
    return jnp.sum(jnp.where(ri == ci, row, 0.0), axis=1, keepdims=True)


def _dot_nt(a, b):
    return lax.dot_general(a, b, (((1,), (1,)), ((), ())), preferred_element_type=F32)


def _norm_rope_t(xt, gcol, cos_t, sin_t, n_heads):
    t = xt.shape[1]
    x3 = xt.reshape(n_heads, HEAD_DIM, t)
    xn = x3 * _rms(x3, 1) * gcol[None]
    x5 = xn.reshape(n_heads * 2, 2, N_FREQ, t)
    xs = jnp.concatenate([x5[:, 1:2], x5[:, 0:1]], axis=1).reshape(n_heads, HEAD_DIM, t)
    out = xn * cos_t[None] + xs * sin_t[None]
    return out.reshape(n_heads * HEAD_DIM, t)


def _in_proj_kernel(x_ref, gmix_ref, wqkv_t_ref, wrg_ref, gq_ref, gk_ref, cos_ref, sin_ref,
                    qt_ref, k_ref, vt_ref, xr_ref, xg_ref):
    gq = _as_column(gq_ref[...]) * Q_SCALE
    gk = _as_column(gk_ref[...])
    t = x_ref.shape[1]
    groups = [slice(i * t // IN_SPLIT, (i + 1) * t // IN_SPLIT) for i in range(IN_SPLIT)]
    hns = []
    for r in groups:
        x = x_ref[0, r]
        hns.append((x * _rms(x, -1) * gmix_ref[...]).astype(BF16))
    lru_proj = lambda hn: _dot(hn, wrg_ref[...])
    qkv_proj = lambda hn: _dot_nt(wqkv_t_ref[...], hn)
    zrs = [lru_proj(hn) for hn in hns[:-1]]
    zts = [qkv_proj(hn) for hn in hns]
    zrs.append(lru_proj(hns[-1]))
    for r, zt, zr in zip(groups, zts, zrs):
        cos_t = cos_ref[:, r]
        sin_t = sin_ref[:, r]
        qt = _norm_rope_t(zt[:D_ATTN], gq, cos_t, sin_t, N_Q_HEADS)
        kt = _norm_rope_t(zt[D_ATTN:D_ATTN + D_KV], gk, cos_t, sin_t, N_KV_HEADS)
        qt_ref[0, :, r] = qt.astype(BF16)
        k_ref[0, r] = kt.T.astype(BF16)
        vt_ref[0, :, r] = zt[D_ATTN + D_KV:].astype(BF16)
        xr_ref[0, r] = zr[:, :D_LRU]
        xg_ref[0, r] = zr[:, D_LRU:]


def _in_proj(x, gmix, wqkv_t, wrg, gq, gk, cos_t, sin_t):
    b, s, d = x.shape
    t = s // N_SEG
    const = lambda *_: (0, 0)
    return pl.pallas_call(
        _in_proj_kernel,
        grid=(b, N_SEG),
        in_specs=[
            pl.BlockSpec((1, t, d), lambda i, j: (i, j, 0)),
            pl.BlockSpec((1, d), const),
            pl.BlockSpec(wqkv_t.shape, const),
            pl.BlockSpec(wrg.shape, const),
            pl.BlockSpec((1, HEAD_DIM), const),
            pl.BlockSpec((1, HEAD_DIM), const),
            pl.BlockSpec((HEAD_DIM, t), lambda i, j: (0, j)),
            pl.BlockSpec((HEAD_DIM, t), lambda i, j: (0, j)),
        ],
        out_specs=[
            pl.BlockSpec((1, D_ATTN, t), lambda i, j: (i, 0, j)),
            pl.BlockSpec((1, t, D_KV), lambda i, j: (i, j, 0)),
            pl.BlockSpec((1, D_KV, t), lambda i, j: (i, 0, j)),
            pl.BlockSpec((1, t, D_LRU), lambda i, j: (i, 0, j)),
            pl.BlockSpec((1, t, D_LRU), lambda i, j: (i, 0, j)),
        ],
        out_shape=[
            jax.ShapeDtypeStruct((b, D_ATTN, s), BF16),
            jax.ShapeDtypeStruct((b, s, D_KV), BF16),
            jax.ShapeDtypeStruct((b, D_KV, s), BF16),
            jax.ShapeDtypeStruct((b, t, N_SEG * D_LRU), F32),
            jax.ShapeDtypeStruct((b, t, N_SEG * D_LRU), F32),
        ],
        compiler_params=pltpu.CompilerParams(
            dimension_semantics=("arbitrary", "arbitrary"), vmem_limit_bytes=VMEM_LIMIT),
        name="in_proj",
    )(x, gmix, wqkv_t, wrg, gq, gk, cos_t, sin_t)


def _attn_kernel(qt_ref, k_ref, vt_ref, grow_ref, *refs):
    n_w = (len(refs) - 3) // 2
    w_f32_refs, y_ref, w_bf16_refs = refs[:n_w], refs[n_w], refs[n_w + 1:2 * n_w + 1]
    vta_ref, kn_ref = refs[2 * n_w + 1:]
    s = k_ref.shape[1]
    tq = qt_ref.shape[2]

    for w_in_ref, w_out_ref in zip(w_f32_refs, w_bf16_refs):
        w_out_ref[...] = w_in_ref[0].astype(BF16)

    @pl.when(pl.program_id(1) == 0)
    def _():
        vt = vt_ref[0]
        row = lax.broadcasted_iota(jnp.int32, (PV_ROWS - HEAD_DIM, s), 0)
        ones_pad = jnp.where(row == 0, 1.0, 0.0).astype(BF16)
        kf = k_ref[0].astype(F32)
        lane = lax.broadcasted_iota(jnp.int32, kf.shape, 1)
        ksq = kf * kf
        for kv in range(N_KV_HEADS):
            vta_ref[kv] = jnp.concatenate([vt[kv * HEAD_DIM:(kv + 1) * HEAD_DIM], ones_pad], axis=0)
            in_head = (lane >= kv * HEAD_DIM) & (lane < (kv + 1) * HEAD_DIM)
            n2 = jnp.sum(jnp.where(in_head, ksq, 0.0), axis=1, keepdims=True)
            kn_ref[kv] = jnp.broadcast_to(jnp.sqrt(jnp.max(n2, axis=0, keepdims=True)), (1, LANES))

    qt = qt_ref[0]
    zeros = jnp.zeros((HEAD_DIM, tq), BF16)

    def q_operand(kv, heads=range(Q_PER_KV)):
        cols = []
        for j in heads:
            h = kv * Q_PER_KV + j
            qh = qt[h * HEAD_DIM:(h + 1) * HEAD_DIM]
            cols.append(jnp.concatenate([qh, zeros] if kv == 0 else [zeros, qh], axis=0))
        return jnp.concatenate(cols, axis=1)

    def finish(accs):
        outs = []
        for acc in accs:
            o = acc[:HEAD_DIM] / acc[HEAD_DIM:HEAD_DIM + 1]
            for j in range(acc.shape[1] // tq):
                outs.append(o[:, j * tq:(j + 1) * tq])
        ot = jnp.concatenate(outs, axis=0)
        y_ref[0] = ((ot * _rms(ot, 0)).T * grow_ref[...]).astype(BF16)

    def scores_fn(qst, chunk):
        return lambda c: _dot(k_ref[0, c * chunk:(c + 1) * chunk, :], qst)

    qf = qt.astype(F32).reshape(N_Q_HEADS, HEAD_DIM, tq)
    qn = jnp.sqrt(jnp.sum(qf * qf, axis=1))
    accs = []
    for kv in range(N_KV_HEADS):
        bound = qn[kv * Q_PER_KV:(kv + 1) * Q_PER_KV] * (kn_ref[kv][:, :1] * BOUND_SLACK)
        groups = [range(j0, j0 + FAST_HEADS) for j0 in range(0, Q_PER_KV, FAST_HEADS)]
        ms = [jnp.concatenate([bound[j:j + 1] for j in g], axis=1) for g in groups]
        scores = [scores_fn(q_operand(kv, g), FAST_CHUNK) for g in groups]
        n_chunks = s // FAST_CHUNK
        acc = [None] * len(groups)
        sc_next = [sf(0) for sf in scores]
        for c in range(n_chunks):
            for gi in range(len(groups)):
                sc = sc_next[gi]
                if c + 1 < n_chunks:
                    sc_next[gi] = scores[gi](c + 1)
                p = jnp.exp2(sc - ms[gi]).astype(BF16)
                pv = _dot(vta_ref[kv, :, c * FAST_CHUNK:(c + 1) * FAST_CHUNK], p)
                acc[gi] = pv if acc[gi] is None else acc[gi] + pv
        accs += acc
    finish(accs)
    den = accs[0][HEAD_DIM:HEAD_DIM + 1]
    for acc in accs[1:]:
        den = jnp.minimum(den, acc[HEAD_DIM:HEAD_DIM + 1])
    lmin = jnp.min(den)

    @pl.when(jnp.logical_not(lmin > L_MIN_OK))
    def _():
        accs = []
        for kv in range(N_KV_HEADS):
            scores = scores_fn(q_operand(kv), KEY_CHUNK)
            n_chunks = s // KEY_CHUNK
            m = None
            acc = None
            sc_next = scores(0)
            for c in range(n_chunks):
                sc = sc_next
                if c + 1 < n_chunks:
                    sc_next = scores(c + 1)
                mc = jnp.max(sc, axis=0, keepdims=True)
                m_new = mc if m is None else jnp.maximum(m, mc)
                p = jnp.exp2(sc - m_new).astype(BF16)
                pv = _dot(vta_ref[kv, :, c * KEY_CHUNK:(c + 1) * KEY_CHUNK], p)
                acc = pv if acc is None else acc * jnp.exp2(m - m_new) + pv
                m = m_new
            accs.append(acc)
        finish(accs)


def _attention(qt, k, vt, gcol, weights):
    b, _, s = qt.shape
    nq = s // Q_TILE
    n_steps = b * nq
    w_in_specs, w_out_specs, w_out_shapes = [], [], []
    for w in weights:
        _, rows, cols = w.shape
        step_rows = max(rows // n_steps, BF16_SUBLANES)
        last = rows // step_rows - 1
        w_in_specs.append(pl.BlockSpec(
            (1, step_rows, cols), lambda i, j, last=last: (0, jnp.minimum(i * nq + j, last), 0)))
        w_out_specs.append(pl.BlockSpec(
            (step_rows, cols), lambda i, j, last=last: (jnp.minimum(i * nq + j, last), 0)))
        w_out_shapes.append(jax.ShapeDtypeStruct((rows, cols), BF16))
    outs = pl.pallas_call(
        _attn_kernel,
        grid=(b, nq),
        in_specs=[
            pl.BlockSpec((1, D_ATTN, Q_TILE), lambda i, j: (i, 0, j)),
            pl.BlockSpec((1, s, D_KV), lambda i, j: (i, 0, 0)),
            pl.BlockSpec((1, D_KV, s), lambda i, j: (i, 0, 0)),
            pl.BlockSpec((1, D_ATTN), lambda i, j: (0, 0)),
        ] + w_in_specs,
        out_specs=[pl.BlockSpec((1, Q_TILE, D_ATTN), lambda i, j: (i, j, 0))] + w_out_specs,
        out_shape=[jax.ShapeDtypeStruct((b, s, D_ATTN), BF16)] + w_out_shapes,
        scratch_shapes=[pltpu.VMEM((N_KV_HEADS, PV_ROWS, s), BF16),
                        pltpu.VMEM((N_KV_HEADS, 1, LANES), F32)],
        compiler_params=pltpu.CompilerParams(
            dimension_semantics=("arbitrary", "arbitrary"), vmem_limit_bytes=VMEM_LIMIT),
        name="attention",
    )(qt, k, vt, gcol, *weights)
    return outs[0], outs[1:]


def _shift_seg_down(x):
    row = lax.broadcasted_iota(jnp.int32, x.shape, 0)
    return jnp.where(row == 0, 0.0, pltpu.roll(x, 1, 0))


def _shift_seg_up(x):
    row = lax.broadcasted_iota(jnp.int32, x.shape, 0)
    return jnp.where(row == SUBLANES - 1, 0.0, pltpu.roll(x, SUBLANES - 1, 0))


def _sigmoid(x):
    return 0.5 * jnp.tanh(0.5 * x) + 0.5


def _gelu_tanh(x):
    k = 0.7978845608028654
    hx = 0.5 * x
    t = jnp.tanh(x * (k + (k * 0.044715) * (x * x)))
    return hx * t + hx


def _lru_kernel(xr_ref, cw_ref, cb_ref, wa_ref, wx_ref, ba_ref, bx_ref, lam_ref, y_ref,
                xh_ref, h0_ref, p0_ref, h1_ref, p1_ref, ystage_ref):
    t_len = xr_ref.shape[1]
    lanes = xr_ref.shape[3]
    n_chunks = t_len // GATE_CHUNK

    cw = cw_ref[0] * 0.5
    cb = cb_ref[...] * 0.5

    halo_lo = [_shift_seg_down(xr_ref[0, t_len - 2])[None], _shift_seg_down(xr_ref[0, t_len - 1])[None]]
    halo_hi = [_shift_seg_up(xr_ref[0, 0])[None]]

    def shifted(t0, n, k):
        lo = t0 + k - CONV_PAD_LEFT
        pieces = halo_lo[lo + CONV_PAD_LEFT:] if lo < 0 else []
        pieces = pieces + [xr_ref[0, max(lo, 0):min(lo + n, t_len)]]
        if lo + n > t_len:
            pieces = pieces + halo_hi[:lo + n - t_len]
        return pieces[0] if len(pieces) == 1 else jnp.concatenate(pieces, axis=0)

    for t0 in range(0, t_len, GATE_CHUNK):
        xh = cb[None]
        for kk in range(CONV_W):
            xh = xh + shifted(t0, GATE_CHUNK, kk) * cw[kk:kk + 1][None]
        xh_ref[t0:t0 + GATE_CHUNK] = xh
    lam = lam_ref[0]
    hc = (0.5 * LRU_C * 1.4426950408889634) * (
        jnp.minimum(lam, 0.0) - jnp.log1p(jnp.exp(-jnp.abs(lam))))

    zblk = jnp.zeros((LRU_BLOCK_W, LRU_BLOCK_W), F32)
    wg, bg_half = [], []
    for e in range(2):
        rows = []
        for i in range(lanes // LRU_BLOCK_W):
            blocks = []
            for w_ref in (wa_ref, wx_ref):
                blocks += [w_ref[0, e, i] if c == i else zblk for c in range(lanes // LRU_BLOCK_W)]
            rows.append(jnp.concatenate(blocks, axis=1))
        wg.append(jnp.concatenate(rows, axis=0).astype(BF16))
        bg_half.append(0.5 * jnp.concatenate([ba_ref[0, e:e + 1], bx_ref[0, e:e + 1]], axis=1))

    def gates(t0, e):
        xh = xh_ref[t0:t0 + SCAN_PART].reshape(SCAN_PART * SUBLANES, lanes)
        th = jnp.tanh(_dot(xh.astype(BF16), wg[e]) + bg_half[e])
        tr = th[:, :lanes]
        ti = th[:, lanes:]
        a = jnp.exp2(tr * hc[e:e + 1] + hc[e:e + 1])
        y = 1.0 - a * a
        mult = y * lax.rsqrt(jnp.maximum(y, 1e-30))
        u = (ti * xh + xh) * mult
        return a, u

    zero = jnp.zeros((SUBLANES, lanes), F32)
    one = jnp.ones((SUBLANES, lanes), F32)
    e0, q0, e1, q1 = zero, one, zero, one
    for part in range(t_len // SCAN_PART):
        t0 = part * SCAN_PART
        a, u = gates(t0, 0)
        for t in range(SCAN_PART):
            rows = slice(t * SUBLANES, (t + 1) * SUBLANES)
            e0 = a[rows] * e0 + u[rows]
            q0 = a[rows] * q0
            h0_ref[t0 + t] = e0
            p0_ref[t0 + t] = q0
        t1 = t_len - (part + 1) * SCAN_PART
        a, u = gates(t1, 1)
        for t in reversed(range(SCAN_PART)):
            rows = slice(t * SUBLANES, (t + 1) * SUBLANES)
            e1 = a[rows] * e1 + u[rows]
            q1 = a[rows] * q1
            h1_ref[t1 + t] = e1
            p1_ref[t1 + t] = q1

    c0 = zero
    c1 = zero
    for _ in range(N_SEG - 1):
        c0 = _shift_seg_down(e0 + q0 * c0)
        c1 = _shift_seg_up(e1 + q1 * c1)

    def out_body(i, carry):
        t0 = pl.multiple_of(i * GATE_CHUNK, GATE_CHUNK)
        sl = pl.ds(t0, GATE_CHUNK)
        h = (h0_ref[sl] + p0_ref[sl] * c0[None]) + (h1_ref[sl] + p1_ref[sl] * c1[None])
        ystage_ref[...] = h.reshape(GATE_CHUNK * SUBLANES, lanes)
        for seg in range(N_SEG):
            y_ref[0, pl.ds(seg * t_len + t0, GATE_CHUNK)] = (
                ystage_ref[pl.ds(seg, GATE_CHUNK, stride=SUBLANES)])
        return carry

    lax.fori_loop(0, n_chunks, out_body, 0)


def _lru(xr_il, conv_w, conv_b, wa, wx, ba, bx, lam):
    b, t_len, n_seg, d = xr_il.shape
    n_groups = d // LANES
    blocks_per_group = LANES // LRU_BLOCK_W
    slab = pltpu.VMEM((t_len, n_seg, LANES), F32)
    w_spec = pl.BlockSpec((1, 2, blocks_per_group, LRU_BLOCK_W, LRU_BLOCK_W),
                          lambda i, j: (0, 0, j, 0, 0))
    lane_group = pl.BlockSpec((1, 2, LANES), lambda i, j: (0, 0, j))
    return pl.pallas_call(
        _lru_kernel,
        grid=(b, n_groups),
        in_specs=[
            pl.BlockSpec((1, t_len, n_seg, LANES), lambda i, j: (i, 0, 0, j)),
            pl.BlockSpec((1, CONV_W, LANES), lambda i, j: (0, 0, j)),
            pl.BlockSpec((1, LANES), lambda i, j: (0, j)),
            w_spec, w_spec, lane_group, lane_group, lane_group,
        ],
        out_specs=pl.BlockSpec((1, t_len * n_seg, LANES), lambda i, j: (i, 0, j)),
        out_shape=jax.ShapeDtypeStruct((b, t_len * n_seg, d), F32),
        scratch_shapes=[slab] * 5 + [
            pltpu.VMEM((GATE_CHUNK * n_seg, LANES), F32)],
        compiler_params=pltpu.CompilerParams(
            dimension_semantics=("arbitrary", "arbitrary"), vmem_limit_bytes=VMEM_LIMIT),
        name="rglru",
    )(xr_il, conv_w, conv_b, wa, wx, ba, bx, lam)


def _tail_kernel(x_ref, ya_ref, hl_ref, xg_ref, p_ref, glru_ref, gmlp_ref, gple_ref, gfin_ref,
                 wout_ref, wup_ref, wdown_ref, wgate_ref, wproj_ref, o_ref):
    x = x_ref[0]
    yl = hl_ref[0] * _gelu_tanh(xg_ref[0])
    yl_n = (yl * _rms(yl, -1) * glru_ref[...]).astype(BF16)
    h = x + _dot(ya_ref[0], wout_ref[:D_ATTN]) + _dot(yl_n, wout_ref[D_ATTN:])
    hn = (h * _rms(h, -1) * gmlp_ref[...]).astype(BF16)
    m = _dot(hn, wup_ref[...])
    act = jnp.square(jnp.maximum(m, 0.0)).astype(BF16)
    h = h + _dot(act, wdown_ref[...])
    hn = (h * _rms(h, -1) * gple_ref[...]).astype(BF16)
    gate = _sigmoid(_dot(hn, wgate_ref[...]))
    h = h + gate * _dot(p_ref[0, 0].astype(BF16), wproj_ref[...])
    o_ref[0] = h * _rms(h, -1) * gfin_ref[...]


def _tail(x, ya, hl, xg_il, p, glru, gmlp, gple, gfin, wout, wup, wdown, wgate, wproj):
    b, s, d = x.shape
    t = s // N_SEG
    const = lambda *_: (0, 0)
    weights = (wout, wup, wdown, wgate, wproj)

    def resident(arr):
        return pl.BlockSpec(arr.shape, const, pipeline_mode=pl.Buffered(1))

    return pl.pallas_call(
        _tail_kernel,
        grid=(b, N_SEG),
        in_specs=[
            pl.BlockSpec((1, t, d), lambda i, j: (i, j, 0)),
            pl.BlockSpec((1, t, D_ATTN), lambda i, j: (i, j, 0)),
            pl.BlockSpec((1, t, D_LRU), lambda i, j: (i, j, 0)),
            pl.BlockSpec((1, t, D_LRU), lambda i, j: (i, 0, j)),
            pl.BlockSpec((1, 1, t, D_PLE), lambda i, j: (0, i, j, 0)),
            resident(glru), resident(gmlp), resident(gple), resident(gfin),
        ] + [resident(w) for w in weights],
        out_specs=pl.BlockSpec((1, t, d), lambda i, j: (i, j, 0)),
        out_shape=jax.ShapeDtypeStruct((b, s, d), F32),
        compiler_params=pltpu.CompilerParams(
            dimension_semantics=("arbitrary", "arbitrary"), vmem_limit_bytes=VMEM_LIMIT),
        name="tail",
    )(x, ya, hl, xg_il, p, glru, gmlp, gple, gfin, *weights)


def _rope_tables_t(seq_len):
    pos = np.arange(seq_len)
    row = (pos // GRID_W).astype(np.float32)
    col = (pos % GRID_W).astype(np.float32)
    inv_freq = np.float32(ROPE_THETA) ** (-np.arange(N_FREQ, dtype=np.float32) / np.float32(N_FREQ))
    ang_r = (row[None, :] * inv_freq[:, None]).astype(np.float32)
    ang_c = (col[None, :] * inv_freq[:, None]).astype(np.float32)
    cos_t = np.concatenate([np.cos(ang_r)] * 2 + [np.cos(ang_c)] * 2, axis=0)
    sin_t = np.concatenate([-np.sin(ang_r), np.sin(ang_r), -np.sin(ang_c), np.sin(ang_c)], axis=0)
    return jnp.asarray(cos_t, F32), jnp.asarray(sin_t, F32)


def kernel(x, p, mix_norm, w_in, q_norm, k_norm, conv_w, conv_b, lru_wa, lru_ba, lru_wx, lru_bx,
           lru_lambda, attn_out_norm, lru_out_norm, w_out, mlp_norm, w_up, w_down, ple_norm,
           w_ple_gate, w_ple_proj, final_norm):
    b, s, d = x.shape
    assert w_in.shape[0] == 1, "single-layer trunk: the final norm is fused into the layer tail"
    cos_t, sin_t = _rope_tables_t(s)
    wqkv_t = w_in[0, :, :D_ATTN + 2 * D_KV].T.astype(BF16)
    wrg = w_in[0, :, D_ATTN + 2 * D_KV:].astype(BF16)
    qt, k, vt, xr, xg = _in_proj(x, mix_norm, wqkv_t, wrg, q_norm, k_norm, cos_t, sin_t)
    ya, tail_weights = _attention(qt, k, vt, attn_out_norm,
                                  (w_out, w_up, w_down, w_ple_gate, w_ple_proj))
    hl = _lru(xr.reshape(b, s // N_SEG, N_SEG, D_LRU), conv_w, conv_b, lru_wa, lru_wx, lru_ba, lru_bx,
              lru_lambda)
    return _tail(x, ya, hl, xg, p, lru_out_norm, mlp_norm, ple_norm, final_norm.reshape(1, -1),
                 *tail_weights)
```

```python
import jax
import jax.numpy as jnp
import numpy as np
from jax import lax
from jax.experimental import pallas as pl
from jax.experimental.pallas import tpu as pltpu

D_MODEL = 1024
GRID_W = 64
HEAD_DIM = 64
D_ATTN = 512
N_Q_HEADS = 8
N_KV_HEADS = 2
Q_PER_KV = 4
D_KV = 128
ROPE_THETA = 10000.0
N_FREQ = 16
D_LRU = 512
LRU_BLOCK_W = 64
LRU_C = 8.0
CONV_W = 4
CONV_PAD_LEFT = 2
D_FF = 4096
D_PLE = 256
NORM_EPS = 1e-6

SUBLANES = 8
BF16_SUBLANES = 16
LANES = 128

N_SEG = SUBLANES
Q_TILE = 256
KEY_CHUNK = 256
FAST_CHUNK = 256
FAST_HEADS = 1
PV_ROWS = 80
BOUND_SLACK = 1.001
L_MIN_OK = 2.0 ** -80
GATE_CHUNK = 64
SCAN_PART = 128
IN_SPLIT = 2
VMEM_LIMIT = 56 * 1024 * 1024
Q_SCALE = HEAD_DIM ** -0.5 * 1.4426950408889634

F32 = jnp.float32
BF16 = jnp.bfloat16


def _rms(x, axis):
    return lax.rsqrt(jnp.mean(x * x, axis=axis, keepdims=True) + NORM_EPS)


def _dot(a, b):
    return jnp.dot(a, b, preferred_element_type=F32)


def _as_column(row):
    n = row.shape[1]
    ri = lax.broadcasted_iota(jnp.int32, (n, n), 0)
    ci = lax.broadcasted_iota(jnp.int32, (n, n), 1)
    return jnp.sum(jnp.where(ri == ci, row, 0.0), axis=1, keepdims=True)


def _dot_nt(a, b):
    return lax.dot_general(a, b, (((1,), (1,)), ((), ())), preferred_element_type=F32)


def _norm_rope_t(xt, gcol, cos_t, sin_t, n_heads):
    t = xt.shape[1]
    x3 = xt.reshape(n_heads, HEAD_DIM, t)
    xn = x3 * _rms(x3, 1) * gcol[None]
    x5 = xn.reshape(n_heads * 2, 2, N_FREQ, t)
    xs = jnp.concatenate([x5[:, 1:2], x5[:, 0:1]], axis=1).reshape(n_heads, HEAD_DIM, t)
    out = xn * cos_t[None] + xs * sin_t[None]
    return out.reshape(n_heads * HEAD_DIM, t)


def _in_proj_kernel(x_ref, gmix_ref, wqkv_t_ref, wrg_ref, gq_ref, gk_ref, cos_ref, sin_ref,
                    qt_ref, k_ref, vt_ref, xr_ref, xg_ref):
    gq = _as_column(gq_ref[...]) * Q_SCALE
    gk = _as_column(gk_ref[...])
    t = x_ref.shape[1]
    groups = [slice(i * t // IN_SPLIT, (i + 1) * t // IN_SPLIT) for i in range(IN_SPLIT)]
    hns = []
    for r in groups:
        x = x_ref[0, r]
        hns.append((x * _rms(x, -1) * gmix_ref[...]).astype(BF16))
    lru_proj = lambda hn: _dot(hn, wrg_ref[...])
    qkv_proj = lambda hn: _dot_nt(wqkv_t_ref[...], hn)
    zrs = [lru_proj(hn) for hn in hns[:-1]]
    zts = [qkv_proj(hn) for hn in hns]
    zrs.append(lru_proj(hns[-1]))
    for r, zt, zr in zip(groups, zts, zrs):
        cos_t = cos_ref[:, r]
        sin_t = sin_ref[:, r]
        qt = _norm_rope_t(zt[:D_ATTN], gq, cos_t, sin_t, N_Q_HEADS)
        kt = _norm_rope_t(zt[D_ATTN:D_ATTN + D_KV], gk, cos_t, sin_t, N_KV_HEADS)
        qt_ref[0, :, r] = qt.astype(BF16)
        k_ref[0, r] = kt.T.astype(BF16)
        vt_ref[0, :, r] = zt[D_ATTN + D_KV:].astype(BF16)
        xr_ref[0, r] = zr[:, :D_LRU]
        xg_ref[0, r] = zr[:, D_LRU:]


def _in_proj(x, gmix, wqkv_t, wrg, gq, gk, cos_t, sin_t):
    b, s, d = x.shape
    t = s // N_SEG
    const = lambda *_: (0, 0)
    return pl.pallas_call(
        _in_proj_kernel,
        grid=(b, N_SEG),
        in_specs=[
            pl.BlockSpec((1, t, d), lambda i, j: (i, j, 0)),
            pl.BlockSpec((1, d), const),
            pl.BlockSpec(wqkv_t.shape, const),
            pl.BlockSpec(wrg.shape, const),
            pl.BlockSpec((1, HEAD_DIM), const),
            pl.BlockSpec((1, HEAD_DIM), const),
            pl.BlockSpec((HEAD_DIM, t), lambda i, j: (0, j)),
            pl.BlockSpec((HEAD_DIM, t), lambda i, j: (0, j)),
        ],
        out_specs=[
            pl.BlockSpec((1, D_ATTN, t), lambda i, j: (i, 0, j)),
            pl.BlockSpec((1, t, D_KV), lambda i, j: (i, j, 0)),
            pl.BlockSpec((1, D_KV, t), lambda i, j: (i, 0, j)),
            pl.BlockSpec((1, t, D_LRU), lambda i, j: (i, 0, j)),
            pl.BlockSpec((1, t, D_LRU), lambda i, j: (i, 0, j)),
        ],
        out_shape=[
            jax.ShapeDtypeStruct((b, D_ATTN, s), BF16),
            jax.ShapeDtypeStruct((b, s, D_KV), BF16),
            jax.ShapeDtypeStruct((b, D_KV, s), BF16),
            jax.ShapeDtypeStruct((b, t, N_SEG * D_LRU), F32),
            jax.ShapeDtypeStruct((b, t, N_SEG * D_LRU), F32),
        ],
        compiler_params=pltpu.CompilerParams(
            dimension_semantics=("arbitrary", "arbitrary"), vmem_limit_bytes=VMEM_LIMIT),
        name="in_proj",
    )(x, gmix, wqkv_t, wrg, gq, gk, cos_t, sin_t)


def _attn_kernel(qt_ref, k_ref, vt_ref, grow_ref, *refs):
    n_w = (len(refs) - 3) // 2
    w_f32_refs, y_ref, w_bf16_refs = refs[:n_w], refs[n_w], refs[n_w + 1:2 * n_w + 1]
    vta_ref, kn_ref = refs[2 * n_w + 1:]
    s = k_ref.shape[1]
    tq = qt_ref.shape[2]

    for w_in_ref, w_out_ref in zip(w_f32_refs, w_bf16_refs):
        w_out_ref[...] = w_in_ref[0].astype(BF16)

    @pl.when(pl.program_id(1) == 0)
    def _():
        vt = vt_ref[0]
        row = lax.broadcasted_iota(jnp.int32, (PV_ROWS - HEAD_DIM, s), 0)
        ones_pad = jnp.where(row == 0, 1.0, 0.0).astype(BF16)
        kf = k_ref[0].astype(F32)
        lane = lax.broadcasted_iota(jnp.int32, kf.shape, 1)
        ksq = kf * kf
        for kv in range(N_KV_HEADS):
            vta_ref[kv] = jnp.concatenate([vt[kv * HEAD_DIM:(kv + 1) * HEAD_DIM], ones_pad], axis=0)
            in_head = (lane >= kv * HEAD_DIM) & (lane < (kv + 1) * HEAD_DIM)
            n2 = jnp.sum(jnp.where(in_head, ksq, 0.0), axis=1, keepdims=True)
            kn_ref[kv] = jnp.broadcast_to(jnp.sqrt(jnp.max(n2, axis=0, keepdims=True)), (1, LANES))

    qt = qt_ref[0]
    zeros = jnp.zeros((HEAD_DIM, tq), BF16)

    def q_operand(kv, heads=range(Q_PER_KV)):
        cols = []
        for j in heads:
            h = kv * Q_PER_KV + j
            qh = qt[h * HEAD_DIM:(h + 1) * HEAD_DIM]
            cols.append(jnp.concatenate([qh, zeros] if kv == 0 else [zeros, qh], axis=0))
        return jnp.concatenate(cols, axis=1)

    def finish(accs):
        outs = []
        for acc in accs:
            o = acc[:HEAD_DIM] / acc[HEAD_DIM:HEAD_DIM + 1]
            for j in range(acc.shape[1] // tq):
                outs.append(o[:, j * tq:(j + 1) * tq])
        ot = jnp.concatenate(outs, axis=0)
        y_ref[0] = ((ot * _rms(ot, 0)).T * grow_ref[...]).astype(BF16)

    def scores_fn(qst, chunk):
        return lambda c: _dot(k_ref[0, c * chunk:(c + 1) * chunk, :], qst)

    qf = qt.astype(F32).reshape(N_Q_HEADS, HEAD_DIM, tq)
    qn = jnp.sqrt(jnp.sum(qf * qf, axis=1))
    groups = [range(j0, j0 + FAST_HEADS) for j0 in range(0, Q_PER_KV, FAST_HEADS)]
    n_chunks = s // FAST_CHUNK
    ms, scores = [], []
    for kv in range(N_KV_HEADS):
        bound = qn[kv * Q_PER_KV:(kv + 1) * Q_PER_KV] * (kn_ref[kv][:, :1] * BOUND_SLACK)
        ms.append([jnp.concatenate([bound[j:j + 1] for j in g], axis=1) for g in groups])
        scores.append([scores_fn(q_operand(kv, g), FAST_CHUNK) for g in groups])
    steps = [(kv, c) for kv in range(N_KV_HEADS) for c in range(n_chunks)]
    acc = [[None] * len(groups) for _ in range(N_KV_HEADS)]
    sc_next = [sf(0) for sf in scores[0]]
    for i, (kv, c) in enumerate(steps):
        for gi in range(len(groups)):
            sc = sc_next[gi]
            if i + 1 < len(steps):
                kv_n, c_n = steps[i + 1]
                sc_next[gi] = scores[kv_n][gi](c_n)
            p = jnp.exp2(sc - ms[kv][gi]).astype(BF16)
            pv = _dot(vta_ref[kv, :, c * FAST_CHUNK:(c + 1) * FAST_CHUNK], p)
            acc[kv][gi] = pv if acc[kv][gi] is None else acc[kv][gi] + pv
    accs = acc[0] + acc[1]
    finish(accs)
    den = accs[0][HEAD_DIM:HEAD_DIM + 1]
    for acc in accs[1:]:
        den = jnp.minimum(den, acc[HEAD_DIM:HEAD_DIM + 1])
    lmin = jnp.min(den)

    @pl.when(jnp.logical_not(lmin > L_MIN_OK))
    def _():
        accs = []
        for kv in range(N_KV_HEADS):
            scores = scores_fn(q_operand(kv), KEY_CHUNK)
            n_chunks = s // KEY_CHUNK
            m = None
            acc = None
            sc_next = scores(0)
            for c in range(n_chunks):
                sc = sc_next
                if c + 1 < n_chunks:
                    sc_next = scores(c + 1)
                mc = jnp.max(sc, axis=0, keepdims=True)
                m_new = mc if m is None else jnp.maximum(m, mc)
                p = jnp.exp2(sc - m_new).astype(BF16)
                pv = _dot(vta_ref[kv, :, c * KEY_CHUNK:(c + 1) * KEY_CHUNK], p)
                acc = pv if acc is None else acc * jnp.exp2(m - m_new) + pv
                m = m_new
            accs.append(acc)
        finish(accs)


def _attention(qt, k, vt, gcol, weights):
    b, _, s = qt.shape
    nq = s // Q_TILE
    n_steps = b * nq
    w_in_specs, w_out_specs, w_out_shapes = [], [], []
    for w in weights:
        _, rows, cols = w.shape
        step_rows = max(rows // n_steps, BF16_SUBLANES)
        last = rows // step_rows - 1
        w_in_specs.append(pl.BlockSpec(
            (1, step_rows, cols), lambda i, j, last=last: (0, jnp.minimum(i * nq + j, last), 0)))
        w_out_specs.append(pl.BlockSpec(
            (step_rows, cols), lambda i, j, last=last: (jnp.minimum(i * nq + j, last), 0)))
        w_out_shapes.append(jax.ShapeDtypeStruct((rows, cols), BF16))
    outs = pl.pallas_call(
        _attn_kernel,
        grid=(b, nq),
        in_specs=[
            pl.BlockSpec((1, D_ATTN, Q_TILE), lambda i, j: (i, 0, j)),
            pl.BlockSpec((1, s, D_KV), lambda i, j: (i, 0, 0)),
            pl.BlockSpec((1, D_KV, s), lambda i, j: (i, 0, 0)),
            pl.BlockSpec((1, D_ATTN), lambda i, j: (0, 0)),
        ] + w_in_specs,
        out_specs=[pl.BlockSpec((1, Q_TILE, D_ATTN), lambda i, j: (i, j, 0))] + w_out_specs,
        out_shape=[jax.ShapeDtypeStruct((b, s, D_ATTN), BF16)] + w_out_shapes,
        scratch_shapes=[pltpu.VMEM((N_KV_HEADS, PV_ROWS, s), BF16),
                        pltpu.VMEM((N_KV_HEADS, 1, LANES), F32)],
        compiler_params=pltpu.CompilerParams(
            dimension_semantics=("arbitrary", "arbitrary"), vmem_limit_bytes=VMEM_LIMIT),
        name="attention",
    )(qt, k, vt, gcol, *weights)
    return outs[0], outs[1:]


def _shift_seg_down(x):
    row = lax.broadcasted_iota(jnp.int32, x.shape, 0)
    return jnp.where(row == 0, 0.0, pltpu.roll(x, 1, 0))


def _shift_seg_up(x):
    row = lax.broadcasted_iota(jnp.int32, x.shape, 0)
    return jnp.where(row == SUBLANES - 1, 0.0, pltpu.roll(x, SUBLANES - 1, 0))


def _sigmoid(x):
    return 0.5 * jnp.tanh(0.5 * x) + 0.5


def _gelu_tanh(x):
    k = 0.7978845608028654
    hx = 0.5 * x
    t = jnp.tanh(x * (k + (k * 0.044715) * (x * x)))
    return hx * t + hx


def _lru_kernel(xr_ref, cw_ref, cb_ref, wa_ref, wx_ref, ba_ref, bx_ref, lam_ref, y_ref,
                xh_ref, h0_ref, p0_ref, h1_ref, p1_ref, ystage_ref):
    t_len = xr_ref.shape[1]
    lanes = xr_ref.shape[3]
    n_chunks = t_len // GATE_CHUNK

    cw = cw_ref[0] * 0.5
    cb = cb_ref[...] * 0.5

    halo_lo = [_shift_seg_down(xr_ref[0, t_len - 2])[None], _shift_seg_down(xr_ref[0, t_len - 1])[None]]
    halo_hi = [_shift_seg_up(xr_ref[0, 0])[None]]

    def shifted(t0, n, k):
        lo = t0 + k - CONV_PAD_LEFT
        pieces = halo_lo[lo + CONV_PAD_LEFT:] if lo < 0 else []
        pieces = pieces + [xr_ref[0, max(lo, 0):min(lo + n, t_len)]]
        if lo + n > t_len:
            pieces = pieces + halo_hi[:lo + n - t_len]
        return pieces[0] if len(pieces) == 1 else jnp.concatenate(pieces, axis=0)

    for t0 in range(0, t_len, GATE_CHUNK):
        xh = cb[None]
        for kk in range(CONV_W):
            xh = xh + shifted(t0, GATE_CHUNK, kk) * cw[kk:kk + 1][None]
        xh_ref[t0:t0 + GATE_CHUNK] = xh
    lam = lam_ref[0]
    hc = (0.5 * LRU_C * 1.4426950408889634) * (
        jnp.minimum(lam, 0.0) - jnp.log1p(jnp.exp(-jnp.abs(lam))))

    zblk = jnp.zeros((LRU_BLOCK_W, LRU_BLOCK_W), F32)
    brow = lax.broadcasted_iota(jnp.int32, (lanes, 2 * lanes), 0)
    wg = []
    for e in range(2):
        rows = []
        for i in range(lanes // LRU_BLOCK_W):
            blocks = []
            for w_ref in (wa_ref, wx_ref):
                blocks += [w_ref[0, e, i] if c == i else zblk for c in range(lanes // LRU_BLOCK_W)]
            rows.append(jnp.concatenate(blocks, axis=1))
        b_half = 0.5 * jnp.concatenate([ba_ref[0, e:e + 1], bx_ref[0, e:e + 1]], axis=1)
        b_hi = b_half.astype(BF16).astype(F32)
        b_lo = b_half - b_hi
        bias_rows = jnp.where(brow == 0, b_hi, jnp.where(brow == 1, b_lo, 0.0))
        wg.append(jnp.concatenate([jnp.concatenate(rows, axis=0), bias_rows], axis=0).astype(BF16))
    ones_cols = jnp.where(
        lax.broadcasted_iota(jnp.int32, (SCAN_PART * SUBLANES, lanes), 1) < 2, 1.0, 0.0).astype(BF16)

    def gates(t0, e):
        xh = xh_ref[t0:t0 + SCAN_PART].reshape(SCAN_PART * SUBLANES, lanes)
        th = jnp.tanh(_dot(jnp.concatenate([xh.astype(BF16), ones_cols], axis=1), wg[e]))
        tr = th[:, :lanes]
        ti = th[:, lanes:]
        a = jnp.exp2(tr * hc[e:e + 1] + hc[e:e + 1])
        y = 1.0 - a * a
        mult = y * lax.rsqrt(jnp.maximum(y, 1e-30))
        u = (ti * xh + xh) * mult
        return a, u

    zero = jnp.zeros((SUBLANES, lanes), F32)
    one = jnp.ones((SUBLANES, lanes), F32)
    e0, q0, e1, q1 = zero, one, zero, one
    for part in range(t_len // SCAN_PART):
        t0 = part * SCAN_PART
        a, u = gates(t0, 0)
        for t in range(SCAN_PART):
            rows = slice(t * SUBLANES, (t + 1) * SUBLANES)
            e0 = a[rows] * e0 + u[rows]
            q0 = a[rows] * q0
            h0_ref[t0 + t] = e0
            p0_ref[t0 + t] = q0
        t1 = t_len - (part + 1) * SCAN_PART
        a, u = gates(t1, 1)
        for t in reversed(range(SCAN_PART)):
            rows = slice(t * SUBLANES, (t + 1) * SUBLANES)
            e1 = a[rows] * e1 + u[rows]
            q1 = a[rows] * q1
            h1_ref[t1 + t] = e1
            p1_ref[t1 + t] = q1

    c0 = zero
    c1 = zero
    for _ in range(N_SEG - 1):
        c0 = _shift_seg_down(e0 + q0 * c0)
        c1 = _shift_seg_up(e1 + q1 * c1)

    def out_body(i, carry):
        t0 = pl.multiple_of(i * GATE_CHUNK, GATE_CHUNK)
        sl = pl.ds(t0, GATE_CHUNK)
        h = (h0_ref[sl] + p0_ref[sl] * c0[None]) + (h1_ref[sl] + p1_ref[sl] * c1[None])
        ystage_ref[...] = h.reshape(GATE_CHUNK * SUBLANES, lanes)
        for seg in range(N_SEG):
            y_ref[0, pl.ds(seg * t_len + t0, GATE_CHUNK)] = (
                ystage_ref[pl.ds(seg, GATE_CHUNK, stride=SUBLANES)])
        return carry

    lax.fori_loop(0, n_chunks, out_body, 0)


def _lru(xr_il, conv_w, conv_b, wa, wx, ba, bx, lam):
    b, t_len, n_seg, d = xr_il.shape
    n_groups = d // LANES
    blocks_per_group = LANES // LRU_BLOCK_W
    slab = pltpu.VMEM((t_len, n_seg, LANES), F32)
    w_spec = pl.BlockSpec((1, 2, blocks_per_group, LRU_BLOCK_W, LRU_BLOCK_W),
                          lambda i, j: (0, 0, j, 0, 0))
    lane_group = pl.BlockSpec((1, 2, LANES), lambda i, j: (0, 0, j))
    return pl.pallas_call(
        _lru_kernel,
        grid=(b, n_groups),
        in_specs=[
            pl.BlockSpec((1, t_len, n_seg, LANES), lambda i, j: (i, 0, 0, j)),
            pl.BlockSpec((1, CONV_W, LANES), lambda i, j: (0, 0, j)),
            pl.BlockSpec((1, LANES), lambda i, j: (0, j)),
            w_spec, w_spec, lane_group, lane_group, lane_group,
        ],
        out_specs=pl.BlockSpec((1, t_len * n_seg, LANES), lambda i, j: (i, 0, j)),
        out_shape=jax.ShapeDtypeStruct((b, t_len * n_seg, d), F32),
        scratch_shapes=[slab] * 5 + [
            pltpu.VMEM((GATE_CHUNK * n_seg, LANES), F32)],
        compiler_params=pltpu.CompilerParams(
            dimension_semantics=("arbitrary", "arbitrary"), vmem_limit_bytes=VMEM_LIMIT),
        name="rglru",
    )(xr_il, conv_w, conv_b, wa, wx, ba, bx, lam)


def _tail_kernel(x_ref, ya_ref, hl_ref, xg_ref, p_ref, glru_ref, gmlp_ref, gple_ref, gfin_ref,
                 wout_ref, wup_ref, wdown_ref, wgate_ref, wproj_ref, o_ref):
    x = x_ref[0]
    yl = hl_ref[0] * _gelu_tanh(xg_ref[0])
    yl_n = (yl * _rms(yl, -1) * glru_ref[...]).astype(BF16)
    h = x + _dot(ya_ref[0], wout_ref[:D_ATTN]) + _dot(yl_n, wout_ref[D_ATTN:])
    hn = (h * _rms(h, -1) * gmlp_ref[...]).astype(BF16)
    m = _dot(hn, wup_ref[...])
    act = jnp.square(jnp.maximum(m, 0.0)).astype(BF16)
    h = h + _dot(act, wdown_ref[...])
    hn = (h * _rms(h, -1) * gple_ref[...]).astype(BF16)
    gate = _sigmoid(_dot(hn, wgate_ref[...]))
    h = h + gate * _dot(p_ref[0, 0].astype(BF16), wproj_ref[...])
    o_ref[0] = h * _rms(h, -1) * gfin_ref[...]


def _tail(x, ya, hl, xg_il, p, glru, gmlp, gple, gfin, wout, wup, wdown, wgate, wproj):
    b, s, d = x.shape
    t = s // N_SEG
    const = lambda *_: (0, 0)
    weights = (wout, wup, wdown, wgate, wproj)

    def resident(arr):
        return pl.BlockSpec(arr.shape, const, pipeline_mode=pl.Buffered(1))

    return pl.pallas_call(
        _tail_kernel,
        grid=(b, N_SEG),
        in_specs=[
            pl.BlockSpec((1, t, d), lambda i, j: (i, j, 0)),
            pl.BlockSpec((1, t, D_ATTN), lambda i, j: (i, j, 0)),
            pl.BlockSpec((1, t, D_LRU), lambda i, j: (i, j, 0)),
            pl.BlockSpec((1, t, D_LRU), lambda i, j: (i, 0, j)),
            pl.BlockSpec((1, 1, t, D_PLE), lambda i, j: (0, i, j, 0)),
            resident(glru), resident(gmlp), resident(gple), resident(gfin),
        ] + [resident(w) for w in weights],
        out_specs=pl.BlockSpec((1, t, d), lambda i, j: (i, j, 0)),
        out_shape=jax.ShapeDtypeStruct((b, s, d), F32),
        compiler_params=pltpu.CompilerParams(
            dimension_semantics=("arbitrary", "arbitrary"), vmem_limit_bytes=VMEM_LIMIT),
        name="tail",
    )(x, ya, hl, xg_il, p, glru, gmlp, gple, gfin, *weights)


def _rope_tables_t(seq_len):
    pos = np.arange(seq_len)
    row = (pos // GRID_W).astype(np.float32)
    col = (pos % GRID_W).astype(np.float32)
    inv_freq = np.float32(ROPE_THETA) ** (-np.arange(N_FREQ, dtype=np.float32) / np.float32(N_FREQ))
    ang_r = (row[None, :] * inv_freq[:, None]).astype(np.float32)
    ang_c = (col[None, :] * inv_freq[:, None]).astype(np.float32)
    cos_t = np.concatenate([np.cos(ang_r)] * 2 + [np.cos(ang_c)] * 2, axis=0)
    sin_t = np.concatenate([-np.sin(ang_r), np.sin(ang_r), -np.sin(ang_c), np.sin(ang_c)], axis=0)
    return jnp.asarray(cos_t, F32), jnp.asarray(sin_t, F32)


def kernel(x, p, mix_norm, w_in, q_norm, k_norm, conv_w, conv_b, lru_wa, lru_ba, lru_wx, lru_bx,
           lru_lambda, attn_out_norm, lru_out_norm, w_out, mlp_norm, w_up, w_down, ple_norm,
           w_ple_gate, w_ple_proj, final_norm):
    b, s, d = x.shape
    assert w_in.shape[0] == 1, "single-layer trunk: the final norm is fused into the layer tail"
    cos_t, sin_t = _rope_tables_t(s)
    wqkv_t = w_in[0, :, :D_ATTN + 2 * D_KV].T.astype(BF16)
    wrg = w_in[0, :, D_ATTN + 2 * D_KV:].astype(BF16)
    qt, k, vt, xr, xg = _in_proj(x, mix_norm, wqkv_t, wrg, q_norm, k_norm, cos_t, sin_t)
    ya, tail_weights = _attention(qt, k, vt, attn_out_norm,
                                  (w_out, w_up, w_down, w_ple_gate, w_ple_proj))
    hl = _lru(xr.reshape(b, s // N_SEG, N_SEG, D_LRU), conv_w, conv_b, lru_wa, lru_wx, lru_ba, lru_bx,
              lru_lambda)
    return _tail(x, ya, hl, xg, p, lru_out_norm, mlp_norm, ple_norm, final_norm.reshape(1, -1),
                 *tail_weights)
```

```python
import jax
import jax.numpy as jnp
import numpy as np
from jax import lax
from jax.experimental import pallas as pl
from jax.experimental.pallas import tpu as pltpu

D_MODEL = 1024
GRID_W = 64
HEAD_DIM = 64
D_ATTN = 512
N_Q_HEADS = 8
N_KV_HEADS = 2
Q_PER_KV = 4
D_KV = 128
ROPE_THETA = 10000.0
N_FREQ = 16
D_LRU = 512
LRU_BLOCK_W = 64
LRU_C = 8.0
CONV_W = 4
CONV_PAD_LEFT = 2
D_FF = 4096
D_PLE = 256
NORM_EPS = 1e-6

SUBLANES = 8
BF16_SUBLANES = 16
LANES = 128

N_SEG = SUBLANES
Q_TILE = 256
KEY_CHUNK = 256
FAST_CHUNK = 256
FAST_HEADS = 1
PV_ROWS = 80
BOUND_SLACK = 1.001
L_MIN_OK = 2.0 ** -80
GATE_CHUNK = 64
SCAN_PART = 128
IN_SEGS = 2
IN_SPLIT = 2
VMEM_LIMIT = 56 * 1024 * 1024
Q_SCALE = HEAD_DIM ** -0.5 * 1.4426950408889634

F32 = jnp.float32
BF16 = jnp.bfloat16


def _rms(x, axis):
    return lax.rsqrt(jnp.mean(x * x, axis=axis, keepdims=True) + NORM_EPS)


def _dot(a, b):
    return jnp.dot(a, b, preferred_element_type=F32)


def _as_column(row):
    n = row.shape[1]
    ri = lax.broadcasted_iota(jnp.int32, (n, n), 0)
    ci = lax.broadcasted_iota(jnp.int32, (n, n), 1)
    return jnp.sum(jnp.where(ri == ci, row, 0.0), axis=1, keepdims=True)


def _dot_nt(a, b):
    return lax.dot_general(a, b, (((1,), (1,)), ((), ())), preferred_element_type=F32)


def _norm_rope_t(xt, gcol, cos_t, sin_t, n_heads):
    t = xt.shape[1]
    x3 = xt.reshape(n_heads, HEAD_DIM, t)
    xn = x3 * _rms(x3, 1) * gcol[None]
    x5 = xn.reshape(n_heads * 2, 2, N_FREQ, t)
    xs = jnp.concatenate([x5[:, 1:2], x5[:, 0:1]], axis=1).reshape(n_heads, HEAD_DIM, t)
    out = xn * cos_t[None] + xs * sin_t[None]
    return out.reshape(n_heads * HEAD_DIM, t)


def _in_proj_kernel(x_ref, gmix_ref, wqkv_t_ref, wrg_ref, gq_ref, gk_ref, cos_ref, sin_ref,
                    qt_ref, k_ref, vt_ref, xr_ref, xg_ref, kn2_ref):
    gq = _as_column(gq_ref[...]) * Q_SCALE
    gk = _as_column(gk_ref[...])
    t_len = xr_ref.shape[1]
    rows = t_len // IN_SPLIT
    groups = [slice(i * rows, (i + 1) * rows) for i in range(x_ref.shape[1] // rows)]
    hns = []
    for r in groups:
        x = x_ref[0, r]
        hns.append((x * _rms(x, -1) * gmix_ref[...]).astype(BF16))
    lru_proj = lambda hn: _dot(hn, wrg_ref[...])
    qkv_proj = lambda hn: _dot_nt(wqkv_t_ref[...], hn)
    zrs = [lru_proj(hn) for hn in hns[:-1]]
    zts = [qkv_proj(hn) for hn in hns]
    zrs.append(lru_proj(hns[-1]))
    for r, zt, zr in zip(groups, zts, zrs):
        cos_t = cos_ref[:, r]
        sin_t = sin_ref[:, r]
        qt = _norm_rope_t(zt[:D_ATTN], gq, cos_t, sin_t, N_Q_HEADS)
        kt = _norm_rope_t(zt[D_ATTN:D_ATTN + D_KV], gk, cos_t, sin_t, N_KV_HEADS)
        qt_ref[0, :, r] = qt.astype(BF16)
        k_ref[0, r] = kt.T.astype(BF16)
        kf = kt.astype(BF16).astype(F32).reshape(N_KV_HEADS, HEAD_DIM, kt.shape[1])
        kn2_ref[0, :, r] = jnp.sum(kf * kf, axis=1)
        vt_ref[0, :, r] = zt[D_ATTN + D_KV:].astype(BF16)
        seg, t0 = divmod(r.start, t_len)
        xr_ref[0, t0:t0 + rows, seg * D_LRU:(seg + 1) * D_LRU] = zr[:, :D_LRU]
        xg_ref[0, t0:t0 + rows, seg * D_LRU:(seg + 1) * D_LRU] = zr[:, D_LRU:]


def _in_proj(x, gmix, wqkv_t, wrg, gq, gk, cos_t, sin_t):
    b, s, d = x.shape
    t = s // N_SEG
    tt = IN_SEGS * t
    const = lambda *_: (0, 0)
    return pl.pallas_call(
        _in_proj_kernel,
        grid=(b, N_SEG // IN_SEGS),
        in_specs=[
            pl.BlockSpec((1, tt, d), lambda i, j: (i, j, 0)),
            pl.BlockSpec((1, d), const),
            pl.BlockSpec(wqkv_t.shape, const),
            pl.BlockSpec(wrg.shape, const),
            pl.BlockSpec((1, HEAD_DIM), const),
            pl.BlockSpec((1, HEAD_DIM), const),
            pl.BlockSpec((HEAD_DIM, tt), lambda i, j: (0, j)),
            pl.BlockSpec((HEAD_DIM, tt), lambda i, j: (0, j)),
        ],
        out_specs=[
            pl.BlockSpec((1, D_ATTN, tt), lambda i, j: (i, 0, j)),
            pl.BlockSpec((1, tt, D_KV), lambda i, j: (i, j, 0)),
            pl.BlockSpec((1, D_KV, tt), lambda i, j: (i, 0, j)),
            pl.BlockSpec((1, t, IN_SEGS * D_LRU), lambda i, j: (i, 0, j)),
            pl.BlockSpec((1, t, IN_SEGS * D_LRU), lambda i, j: (i, 0, j)),
            pl.BlockSpec((1, N_KV_HEADS, tt), lambda i, j: (i, 0, j)),
        ],
        out_shape=[
            jax.ShapeDtypeStruct((b, D_ATTN, s), BF16),
            jax.ShapeDtypeStruct((b, s, D_KV), BF16),
            jax.ShapeDtypeStruct((b, D_KV, s), BF16),
            jax.ShapeDtypeStruct((b, t, N_SEG * D_LRU), F32),
            jax.ShapeDtypeStruct((b, t, N_SEG * D_LRU), F32),
            jax.ShapeDtypeStruct((b, N_KV_HEADS, s), F32),
        ],
        compiler_params=pltpu.CompilerParams(
            dimension_semantics=("arbitrary", "arbitrary"), vmem_limit_bytes=VMEM_LIMIT),
        name="in_proj",
    )(x, gmix, wqkv_t, wrg, gq, gk, cos_t, sin_t)


def _attn_kernel(qt_ref, k_ref, vt_ref, kn2_ref, grow_ref, *refs):
    n_w = (len(refs) - 3) // 2
    w_f32_refs, y_ref, w_bf16_refs = refs[:n_w], refs[n_w], refs[n_w + 1:2 * n_w + 1]
    vta_ref, kn_ref = refs[2 * n_w + 1:]
    s = k_ref.shape[1]
    tq = qt_ref.shape[2]

    for w_in_ref, w_out_ref in zip(w_f32_refs, w_bf16_refs):
        w_out_ref[...] = w_in_ref[0].astype(BF16)

    @pl.when(pl.program_id(1) == 0)
    def _():
        vt = vt_ref[0]
        row = lax.broadcasted_iota(jnp.int32, (PV_ROWS - HEAD_DIM, s), 0)
        ones_pad = jnp.where(row == 0, 1.0, 0.0).astype(BF16)
        kn_max = jnp.sqrt(jnp.max(kn2_ref[0], axis=1, keepdims=True))
        for kv in range(N_KV_HEADS):
            vta_ref[kv] = jnp.concatenate([vt[kv * HEAD_DIM:(kv + 1) * HEAD_DIM], ones_pad], axis=0)
            kn_ref[kv] = jnp.broadcast_to(kn_max[kv:kv + 1], (1, LANES))

    qt = qt_ref[0]
    zeros = jnp.zeros((HEAD_DIM, tq), BF16)

    def q_operand(kv, heads=range(Q_PER_KV)):
        cols = []
        for j in heads:
            h = kv * Q_PER_KV + j
            qh = qt[h * HEAD_DIM:(h + 1) * HEAD_DIM]
            cols.append(jnp.concatenate([qh, zeros] if kv == 0 else [zeros, qh], axis=0))
        return jnp.concatenate(cols, axis=1)

    def finish(accs):
        outs = []
        for acc in accs:
            o = acc[:HEAD_DIM] / acc[HEAD_DIM:HEAD_DIM + 1]
            for j in range(acc.shape[1] // tq):
                outs.append(o[:, j * tq:(j + 1) * tq])
        ot = jnp.concatenate(outs, axis=0)
        y_ref[0] = ((ot * _rms(ot, 0)).T * grow_ref[...]).astype(BF16)

    def scores_fn(qst, chunk):
        return lambda c: _dot(k_ref[0, c * chunk:(c + 1) * chunk, :], qst)

    qf = qt.astype(F32).reshape(N_Q_HEADS, HEAD_DIM, tq)
    qn = jnp.sqrt(jnp.sum(qf * qf, axis=1))
    groups = [range(j0, j0 + FAST_HEADS) for j0 in range(0, Q_PER_KV, FAST_HEADS)]
    n_chunks = s // FAST_CHUNK
    ms, scores = [], []
    for kv in range(N_KV_HEADS):
        bound = qn[kv * Q_PER_KV:(kv + 1) * Q_PER_KV] * (kn_ref[kv][:, :1] * BOUND_SLACK)
        ms.append([jnp.concatenate([bound[j:j + 1] for j in g], axis=1) for g in groups])
        scores.append([scores_fn(q_operand(kv, g), FAST_CHUNK) for g in groups])
    steps = [(kv, c) for kv in range(N_KV_HEADS) for c in range(n_chunks)]
    acc = [[None] * len(groups) for _ in range(N_KV_HEADS)]
    sc_next = [sf(0) for sf in scores[0]]
    for i, (kv, c) in enumerate(steps):
        for gi in range(len(groups)):
            sc = sc_next[gi]
            if i + 1 < len(steps):
                kv_n, c_n = steps[i + 1]
                sc_next[gi] = scores[kv_n][gi](c_n)
            p = jnp.exp2(sc - ms[kv][gi]).astype(BF16)
            pv = _dot(vta_ref[kv, :, c * FAST_CHUNK:(c + 1) * FAST_CHUNK], p)
            acc[kv][gi] = pv if acc[kv][gi] is None else acc[kv][gi] + pv
    accs = acc[0] + acc[1]
    finish(accs)
    den = accs[0][HEAD_DIM:HEAD_DIM + 1]
    for acc in accs[1:]:
        den = jnp.minimum(den, acc[HEAD_DIM:HEAD_DIM + 1])
    lmin = jnp.min(den)

    @pl.when(jnp.logical_not(lmin > L_MIN_OK))
    def _():
        accs = []
        for kv in range(N_KV_HEADS):
            scores = scores_fn(q_operand(kv), KEY_CHUNK)
            n_chunks = s // KEY_CHUNK
            m = None
            acc = None
            sc_next = scores(0)
            for c in range(n_chunks):
                sc = sc_next
                if c + 1 < n_chunks:
                    sc_next = scores(c + 1)
                mc = jnp.max(sc, axis=0, keepdims=True)
                m_new = mc if m is None else jnp.maximum(m, mc)
                p = jnp.exp2(sc - m_new).astype(BF16)
                pv = _dot(vta_ref[kv, :, c * KEY_CHUNK:(c + 1) * KEY_CHUNK], p)
                acc = pv if acc is None else acc * jnp.exp2(m - m_new) + pv
                m = m_new
            accs.append(acc)
        finish(accs)


def _attention(qt, k, vt, kn2, gcol, weights):
    b, _, s = qt.shape
    nq = s // Q_TILE
    n_steps = b * nq
    w_in_specs, w_out_specs, w_out_shapes = [], [], []
    for w in weights:
        _, rows, cols = w.shape
        step_rows = max(rows // n_steps, BF16_SUBLANES)
        last = rows // step_rows - 1
        w_in_specs.append(pl.BlockSpec(
            (1, step_rows, cols), lambda i, j, last=last: (0, jnp.minimum(i * nq + j, last), 0)))
        w_out_specs.append(pl.BlockSpec(
            (step_rows, cols), lambda i, j, last=last: (jnp.minimum(i * nq + j, last), 0)))
        w_out_shapes.append(jax.ShapeDtypeStruct((rows, cols), BF16))
    outs = pl.pallas_call(
        _attn_kernel,
        grid=(b, nq),
        in_specs=[
            pl.BlockSpec((1, D_ATTN, Q_TILE), lambda i, j: (i, 0, j)),
            pl.BlockSpec((1, s, D_KV), lambda i, j: (i, 0, 0)),
            pl.BlockSpec((1, D_KV, s), lambda i, j: (i, 0, 0)),
            pl.BlockSpec((1, N_KV_HEADS, s), lambda i, j: (i, 0, 0)),
            pl.BlockSpec((1, D_ATTN), lambda i, j: (0, 0)),
        ] + w_in_specs,
        out_specs=[pl.BlockSpec((1, Q_TILE, D_ATTN), lambda i, j: (i, j, 0))] + w_out_specs,
        out_shape=[jax.ShapeDtypeStruct((b, s, D_ATTN), BF16)] + w_out_shapes,
        scratch_shapes=[pltpu.VMEM((N_KV_HEADS, PV_ROWS, s), BF16),
                        pltpu.VMEM((N_KV_HEADS, 1, LANES), F32)],
        compiler_params=pltpu.CompilerParams(
            dimension_semantics=("arbitrary", "arbitrary"), vmem_limit_bytes=VMEM_LIMIT),
        name="attention",
    )(qt, k, vt, kn2, gcol, *weights)
    return outs[0], outs[1:]


def _shift_seg_down(x):
    row = lax.broadcasted_iota(jnp.int32, x.shape, 0)
    return jnp.where(row == 0, 0.0, pltpu.roll(x, 1, 0))


def _shift_seg_up(x):
    row = lax.broadcasted_iota(jnp.int32, x.shape, 0)
    return jnp.where(row == SUBLANES - 1, 0.0, pltpu.roll(x, SUBLANES - 1, 0))


def _sigmoid(x):
    return 0.5 * jnp.tanh(0.5 * x) + 0.5


def _gelu_tanh(x):
    k = 0.7978845608028654
    hx = 0.5 * x
    t = jnp.tanh(x * (k + (k * 0.044715) * (x * x)))
    return hx * t + hx


def _lru_kernel(xr_ref, cw_ref, cb_ref, wa_ref, wx_ref, ba_ref, bx_ref, lam_ref, y_ref,
                xh_ref, h0_ref, p0_ref, h1_ref, p1_ref, ystage_ref):
    t_len = xr_ref.shape[1]
    lanes = xr_ref.shape[3]
    n_chunks = t_len // GATE_CHUNK

    cw = cw_ref[0] * 0.5
    cb = cb_ref[...] * 0.5

    halo_lo = [_shift_seg_down(xr_ref[0, t_len - 2])[None], _shift_seg_down(xr_ref[0, t_len - 1])[None]]
    halo_hi = [_shift_seg_up(xr_ref[0, 0])[None]]

    def shifted(t0, n, k):
        lo = t0 + k - CONV_PAD_LEFT
        pieces = halo_lo[lo + CONV_PAD_LEFT:] if lo < 0 else []
        pieces = pieces + [xr_ref[0, max(lo, 0):min(lo + n, t_len)]]
        if lo + n > t_len:
            pieces = pieces + halo_hi[:lo + n - t_len]
        return pieces[0] if len(pieces) == 1 else jnp.concatenate(pieces, axis=0)

    for t0 in range(0, t_len, GATE_CHUNK):
        xh = cb[None]
        for kk in range(CONV_W):
            xh = xh + shifted(t0, GATE_CHUNK, kk) * cw[kk:kk + 1][None]
        xh_ref[t0:t0 + GATE_CHUNK] = xh
    lam = lam_ref[0]
    hc = (0.5 * LRU_C * 1.4426950408889634) * (
        jnp.minimum(lam, 0.0) - jnp.log1p(jnp.exp(-jnp.abs(lam))))

    zblk = jnp.zeros((LRU_BLOCK_W, LRU_BLOCK_W), F32)
    brow = lax.broadcasted_iota(jnp.int32, (lanes, 2 * lanes), 0)
    wg = []
    for e in range(2):
        rows = []
        for i in range(lanes // LRU_BLOCK_W):
            blocks = []
            for w_ref in (wa_ref, wx_ref):
                blocks += [w_ref[0, e, i] if c == i else zblk for c in range(lanes // LRU_BLOCK_W)]
            rows.append(jnp.concatenate(blocks, axis=1))
        b_half = 0.5 * jnp.concatenate([ba_ref[0, e:e + 1], bx_ref[0, e:e + 1]], axis=1)
        b_hi = b_half.astype(BF16).astype(F32)
        b_lo = b_half - b_hi
        bias_rows = jnp.where(brow == 0, b_hi, jnp.where(brow == 1, b_lo, 0.0))
        wg.append(jnp.concatenate([jnp.concatenate(rows, axis=0), bias_rows], axis=0).astype(BF16))
    ones_cols = jnp.where(
        lax.broadcasted_iota(jnp.int32, (SCAN_PART * SUBLANES, lanes), 1) < 2, 1.0, 0.0).astype(BF16)

    def gates(t0, e):
        xh = xh_ref[t0:t0 + SCAN_PART].reshape(SCAN_PART * SUBLANES, lanes)
        th = jnp.tanh(_dot(jnp.concatenate([xh.astype(BF16), ones_cols], axis=1), wg[e]))
        tr = th[:, :lanes]
        ti = th[:, lanes:]
        a = jnp.exp2(tr * hc[e:e + 1] + hc[e:e + 1])
        y = 1.0 - a * a
        mult = y * lax.rsqrt(jnp.maximum(y, 1e-30))
        u = (ti * xh + xh) * mult
        return a, u

    zero = jnp.zeros((SUBLANES, lanes), F32)
    one = jnp.ones((SUBLANES, lanes), F32)
    e0, q0, e1, q1 = zero, one, zero, one
    for part in range(t_len // SCAN_PART):
        t0 = part * SCAN_PART
        a, u = gates(t0, 0)
        for t in range(SCAN_PART):
            rows = slice(t * SUBLANES, (t + 1) * SUBLANES)
            e0 = a[rows] * e0 + u[rows]
            q0 = a[rows] * q0
            h0_ref[t0 + t] = e0
            p0_ref[t0 + t] = q0
        t1 = t_len - (part + 1) * SCAN_PART
        a, u = gates(t1, 1)
        for t in reversed(range(SCAN_PART)):
            rows = slice(t * SUBLANES, (t + 1) * SUBLANES)
            e1 = a[rows] * e1 + u[rows]
            q1 = a[rows] * q1
            h1_ref[t1 + t] = e1
            p1_ref[t1 + t] = q1

    c0 = zero
    c1 = zero
    for _ in range(N_SEG - 1):
        c0 = _shift_seg_down(e0 + q0 * c0)
        c1 = _shift_seg_up(e1 + q1 * c1)

    def out_body(i, carry):
        t0 = pl.multiple_of(i * GATE_CHUNK, GATE_CHUNK)
        sl = pl.ds(t0, GATE_CHUNK)
        h = (h0_ref[sl] + p0_ref[sl] * c0[None]) + (h1_ref[sl] + p1_ref[sl] * c1[None])
        ystage_ref[...] = h.reshape(GATE_CHUNK * SUBLANES, lanes)
        for seg in range(N_SEG):
            y_ref[0, pl.ds(seg * t_len + t0, GATE_CHUNK)] = (
                ystage_ref[pl.ds(seg, GATE_CHUNK, stride=SUBLANES)])
        return carry

    lax.fori_loop(0, n_chunks, out_body, 0)


def _lru(xr_il, conv_w, conv_b, wa, wx, ba, bx, lam):
    b, t_len, n_seg, d = xr_il.shape
    n_groups = d // LANES
    blocks_per_group = LANES // LRU_BLOCK_W
    slab = pltpu.VMEM((t_len, n_seg, LANES), F32)
    w_spec = pl.BlockSpec((1, 2, blocks_per_group, LRU_BLOCK_W, LRU_BLOCK_W),
                          lambda i, j: (0, 0, j, 0, 0))
    lane_group = pl.BlockSpec((1, 2, LANES), lambda i, j: (0, 0, j))
    return pl.pallas_call(
        _lru_kernel,
        grid=(b, n_groups),
        in_specs=[
            pl.BlockSpec((1, t_len, n_seg, LANES), lambda i, j: (i, 0, 0, j)),
            pl.BlockSpec((1, CONV_W, LANES), lambda i, j: (0, 0, j)),
            pl.BlockSpec((1, LANES), lambda i, j: (0, j)),
            w_spec, w_spec, lane_group, lane_group, lane_group,
        ],
        out_specs=pl.BlockSpec((1, t_len * n_seg, LANES), lambda i, j: (i, 0, j)),
        out_shape=jax.ShapeDtypeStruct((b, t_len * n_seg, d), F32),
        scratch_shapes=[slab] * 5 + [
            pltpu.VMEM((GATE_CHUNK * n_seg, LANES), F32)],
        compiler_params=pltpu.CompilerParams(
            dimension_semantics=("arbitrary", "arbitrary"), vmem_limit_bytes=VMEM_LIMIT),
        name="rglru",
    )(xr_il, conv_w, conv_b, wa, wx, ba, bx, lam)


def _tail_kernel(x_ref, ya_ref, hl_ref, xg_ref, p_ref, glru_ref, gmlp_ref, gple_ref, gfin_ref,
                 wout_ref, wup_ref, wdown_ref, wgate_ref, wproj_ref, o_ref):
    x = x_ref[0]
    yl = hl_ref[0] * _gelu_tanh(xg_ref[0])
    yl_n = (yl * _rms(yl, -1) * glru_ref[...]).astype(BF16)
    h = x + _dot(ya_ref[0], wout_ref[:D_ATTN]) + _dot(yl_n, wout_ref[D_ATTN:])
    hn = (h * _rms(h, -1) * gmlp_ref[...]).astype(BF16)
    m = _dot(hn, wup_ref[...])
    act = jnp.square(jnp.maximum(m, 0.0)).astype(BF16)
    h = h + _dot(act, wdown_ref[...])
    hn = (h * _rms(h, -1) * gple_ref[...]).astype(BF16)
    gate = _sigmoid(_dot(hn, wgate_ref[...]))
    h = h + gate * _dot(p_ref[0, 0].astype(BF16), wproj_ref[...])
    o_ref[0] = h * _rms(h, -1) * gfin_ref[...]


def _tail(x, ya, hl, xg_il, p, glru, gmlp, gple, gfin, wout, wup, wdown, wgate, wproj):
    b, s, d = x.shape
    t = s // N_SEG
    const = lambda *_: (0, 0)
    weights = (wout, wup, wdown, wgate, wproj)

    def resident(arr):
        return pl.BlockSpec(arr.shape, const, pipeline_mode=pl.Buffered(1))

    return pl.pallas_call(
        _tail_kernel,
        grid=(b, N_SEG),
        in_specs=[
            pl.BlockSpec((1, t, d), lambda i, j: (i, j, 0)),
            pl.BlockSpec((1, t, D_ATTN), lambda i, j: (i, j, 0)),
            pl.BlockSpec((1, t, D_LRU), lambda i, j: (i, j, 0)),
            pl.BlockSpec((1, t, D_LRU), lambda i, j: (i, 0, j)),
            pl.BlockSpec((1, 1, t, D_PLE), lambda i, j: (0, i, j, 0)),
            resident(glru), resident(gmlp), resident(gple), resident(gfin),
        ] + [resident(w) for w in weights],
        out_specs=pl.BlockSpec((1, t, d), lambda i, j: (i, j, 0)),
        out_shape=jax.ShapeDtypeStruct((b, s, d), F32),
        compiler_params=pltpu.CompilerParams(
            dimension_semantics=("arbitrary", "arbitrary"), vmem_limit_bytes=VMEM_LIMIT),
        name="tail",
    )(x, ya, hl, xg_il, p, glru, gmlp, gple, gfin, *weights)


def _rope_tables_t(seq_len):
    pos = np.arange(seq_len)
    row = (pos // GRID_W).astype(np.float32)
    col = (pos % GRID_W).astype(np.float32)
    inv_freq = np.float32(ROPE_THETA) ** (-np.arange(N_FREQ, dtype=np.float32) / np.float32(N_FREQ))
    ang_r = (row[None, :] * inv_freq[:, None]).astype(np.float32)
    ang_c = (col[None, :] * inv_freq[:, None]).astype(np.float32)
    cos_t = np.concatenate([np.cos(ang_r)] * 2 + [np.cos(ang_c)] * 2, axis=0)
    sin_t = np.concatenate([-np.sin(ang_r), np.sin(ang_r), -np.sin(ang_c), np.sin(ang_c)], axis=0)
    return jnp.asarray(cos_t, F32), jnp.asarray(sin_t, F32)


def kernel(x, p, mix_norm, w_in, q_norm, k_norm, conv_w, conv_b, lru_wa, lru_ba, lru_wx, lru_bx,
           lru_lambda, attn_out_norm, lru_out_norm, w_out, mlp_norm, w_up, w_down, ple_norm,
           w_ple_gate, w_ple_proj, final_norm):
    b, s, d = x.shape
    assert w_in.shape[0] == 1, "single-layer trunk: the final norm is fused into the layer tail"
    cos_t, sin_t = _rope_tables_t(s)
    wqkv_t = w_in[0, :, :D_ATTN + 2 * D_KV].T.astype(BF16)
    wrg = w_in[0, :, D_ATTN + 2 * D_KV:].astype(BF16)
    qt, k, vt, xr, xg, kn2 = _in_proj(x, mix_norm, wqkv_t, wrg, q_norm, k_norm, cos_t, sin_t)
    ya, tail_weights = _attention(qt, k, vt, kn2, attn_out_norm,
                                  (w_out, w_up, w_down, w_ple_gate, w_ple_proj))
    hl = _lru(xr.reshape(b, s // N_SEG, N_SEG, D_LRU), conv_w, conv_b, lru_wa, lru_wx, lru_ba, lru_bx,
              lru_lambda)
    return _tail(x, ya, hl, xg, p, lru_out_norm, mlp_norm, ple_norm, final_norm.reshape(1, -1),
                 *tail_weights)
```

```python
import jax
import jax.numpy as jnp
import numpy as np
from jax import lax
from jax.experimental import pallas as pl
from jax.experimental.pallas import tpu as pltpu

D_MODEL = 1024
GRID_W = 64
HEAD_DIM = 64
D_ATTN = 512
N_Q_HEADS = 8
N_KV_HEADS = 2
Q_PER_KV = 4
D_KV = 128
ROPE_THETA = 10000.0
N_FREQ = 16
D_LRU = 512
LRU_BLOCK_W = 64
LRU_C = 8.0
CONV_W = 4
CONV_PAD_LEFT = 2
D_FF = 4096
D_PLE = 256
NORM_EPS = 1e-6

SUBLANES = 8
BF16_SUBLANES = 16
LANES = 128

N_SEG = SUBLANES
Q_TILE = 256
KEY_CHUNK = 256
FAST_CHUNK = 256
FAST_HEADS = 1
PV_ROWS = 80
BOUND_SLACK = 1.001
L_MIN_OK = 2.0 ** -80
GATE_CHUNK = 64
SCAN_PART = 128
IN_SEGS = 2
IN_SPLIT = 2
TAIL_SEGS = 2
VMEM_LIMIT = 61 * 1024 * 1024
Q_SCALE = HEAD_DIM ** -0.5 * 1.4426950408889634

F32 = jnp.float32
BF16 = jnp.bfloat16


def _rms(x, axis):
    return lax.rsqrt(jnp.mean(x * x, axis=axis, keepdims=True) + NORM_EPS)


def _dot(a, b):
    return jnp.dot(a, b, preferred_element_type=F32)


def _as_column(row):
    n = row.shape[1]
    ri = lax.broadcasted_iota(jnp.int32, (n, n), 0)
    ci = lax.broadcasted_iota(jnp.int32, (n, n), 1)
    return jnp.sum(jnp.where(ri == ci, row, 0.0), axis=1, keepdims=True)


def _dot_nt(a, b):
    return lax.dot_general(a, b, (((1,), (1,)), ((), ())), preferred_element_type=F32)


def _norm_rope_t(xt, gcol, cos_t, sin_t, n_heads):
    t = xt.shape[1]
    x3 = xt.reshape(n_heads, HEAD_DIM, t)
    xn = x3 * _rms(x3, 1) * gcol[None]
    x5 = xn.reshape(n_heads * 2, 2, N_FREQ, t)
    xs = jnp.concatenate([x5[:, 1:2], x5[:, 0:1]], axis=1).reshape(n_heads, HEAD_DIM, t)
    out = xn * cos_t[None] + xs * sin_t[None]
    return out.reshape(n_heads * HEAD_DIM, t)


def _in_proj_kernel(x_ref, gmix_ref, wqkv_t_ref, wrg_ref, gq_ref, gk_ref, cos_ref, sin_ref,
                    qt_ref, k_ref, vt_ref, xr_ref, xg_ref, kn2_ref):
    gq = _as_column(gq_ref[...]) * Q_SCALE
    gk = _as_column(gk_ref[...])
    t_len = xr_ref.shape[1]
    rows = t_len // IN_SPLIT
    groups = [slice(i * rows, (i + 1) * rows) for i in range(x_ref.shape[1] // rows)]
    hns = []
    for r in groups:
        x = x_ref[0, r]
        hns.append((x * _rms(x, -1) * gmix_ref[...]).astype(BF16))
    lru_proj = lambda hn: _dot(hn, wrg_ref[...])
    qkv_proj = lambda hn: _dot_nt(wqkv_t_ref[...], hn)
    zrs = [lru_proj(hn) for hn in hns[:-1]]
    zts = [qkv_proj(hn) for hn in hns]
    zrs.append(lru_proj(hns[-1]))
    for r, zt, zr in zip(groups, zts, zrs):
        cos_t = cos_ref[:, r]
        sin_t = sin_ref[:, r]
        qt = _norm_rope_t(zt[:D_ATTN], gq, cos_t, sin_t, N_Q_HEADS)
        kt = _norm_rope_t(zt[D_ATTN:D_ATTN + D_KV], gk, cos_t, sin_t, N_KV_HEADS)
        qt_ref[0, :, r] = qt.astype(BF16)
        k_ref[0, r] = kt.T.astype(BF16)
        kf = kt.astype(BF16).astype(F32).reshape(N_KV_HEADS, HEAD_DIM, kt.shape[1])
        kn2_ref[0, :, r] = jnp.sum(kf * kf, axis=1)
        vt_ref[0, :, r] = zt[D_ATTN + D_KV:].astype(BF16)
        seg, t0 = divmod(r.start, t_len)
        xr_ref[0, t0:t0 + rows, seg * D_LRU:(seg + 1) * D_LRU] = zr[:, :D_LRU]
        xg_ref[0, t0:t0 + rows, seg * D_LRU:(seg + 1) * D_LRU] = zr[:, D_LRU:]


def _in_proj(x, gmix, wqkv_t, wrg, gq, gk, cos_t, sin_t):
    b, s, d = x.shape
    t = s // N_SEG
    tt = IN_SEGS * t
    const = lambda *_: (0, 0)
    return pl.pallas_call(
        _in_proj_kernel,
        grid=(b, N_SEG // IN_SEGS),
        in_specs=[
            pl.BlockSpec((1, tt, d), lambda i, j: (i, j, 0)),
            pl.BlockSpec((1, d), const),
            pl.BlockSpec(wqkv_t.shape, const),
            pl.BlockSpec(wrg.shape, const),
            pl.BlockSpec((1, HEAD_DIM), const),
            pl.BlockSpec((1, HEAD_DIM), const),
            pl.BlockSpec((HEAD_DIM, tt), lambda i, j: (0, j)),
            pl.BlockSpec((HEAD_DIM, tt), lambda i, j: (0, j)),
        ],
        out_specs=[
            pl.BlockSpec((1, D_ATTN, tt), lambda i, j: (i, 0, j)),
            pl.BlockSpec((1, tt, D_KV), lambda i, j: (i, j, 0)),
            pl.BlockSpec((1, D_KV, tt), lambda i, j: (i, 0, j)),
            pl.BlockSpec((1, t, IN_SEGS * D_LRU), lambda i, j: (i, 0, j)),
            pl.BlockSpec((1, t, IN_SEGS * D_LRU), lambda i, j: (i, 0, j)),
            pl.BlockSpec((1, N_KV_HEADS, tt), lambda i, j: (i, 0, j)),
        ],
        out_shape=[
            jax.ShapeDtypeStruct((b, D_ATTN, s), BF16),
            jax.ShapeDtypeStruct((b, s, D_KV), BF16),
            jax.ShapeDtypeStruct((b, D_KV, s), BF16),
            jax.ShapeDtypeStruct((b, t, N_SEG * D_LRU), F32),
            jax.ShapeDtypeStruct((b, t, N_SEG * D_LRU), F32),
            jax.ShapeDtypeStruct((b, N_KV_HEADS, s), F32),
        ],
        compiler_params=pltpu.CompilerParams(
            dimension_semantics=("arbitrary", "arbitrary"), vmem_limit_bytes=VMEM_LIMIT),
        name="in_proj",
    )(x, gmix, wqkv_t, wrg, gq, gk, cos_t, sin_t)


def _attn_kernel(qt_ref, k_ref, vt_ref, kn2_ref, grow_ref, *refs):
    n_w = (len(refs) - 3) // 2
    w_f32_refs, y_ref, w_bf16_refs = refs[:n_w], refs[n_w], refs[n_w + 1:2 * n_w + 1]
    vta_ref, kn_ref = refs[2 * n_w + 1:]
    s = k_ref.shape[1]
    tq = qt_ref.shape[2]

    for w_in_ref, w_out_ref in zip(w_f32_refs, w_bf16_refs):
        w_out_ref[...] = w_in_ref[0].astype(BF16)

    @pl.when(pl.program_id(1) == 0)
    def _():
        vt = vt_ref[0]
        row = lax.broadcasted_iota(jnp.int32, (PV_ROWS - HEAD_DIM, s), 0)
        ones_pad = jnp.where(row == 0, 1.0, 0.0).astype(BF16)
        kn_max = jnp.sqrt(jnp.max(kn2_ref[0], axis=1, keepdims=True))
        for kv in range(N_KV_HEADS):
            vta_ref[kv] = jnp.concatenate([vt[kv * HEAD_DIM:(kv + 1) * HEAD_DIM], ones_pad], axis=0)
            kn_ref[kv] = jnp.broadcast_to(kn_max[kv:kv + 1], (1, LANES))

    qt = qt_ref[0]
    zeros = jnp.zeros((HEAD_DIM, tq), BF16)

    def q_operand(kv, heads=range(Q_PER_KV)):
        cols = []
        for j in heads:
            h = kv * Q_PER_KV + j
            qh = qt[h * HEAD_DIM:(h + 1) * HEAD_DIM]
            cols.append(jnp.concatenate([qh, zeros] if kv == 0 else [zeros, qh], axis=0))
        return jnp.concatenate(cols, axis=1)

    def finish(accs):
        outs = []
        for acc in accs:
            o = acc[:HEAD_DIM] / acc[HEAD_DIM:HEAD_DIM + 1]
            for j in range(acc.shape[1] // tq):
                outs.append(o[:, j * tq:(j + 1) * tq])
        ot = jnp.concatenate(outs, axis=0)
        y_ref[0] = ((ot * _rms(ot, 0)).T * grow_ref[...]).astype(BF16)

    def scores_fn(qst, chunk):
        return lambda c: _dot(k_ref[0, c * chunk:(c + 1) * chunk, :], qst)

    qf = qt.astype(F32).reshape(N_Q_HEADS, HEAD_DIM, tq)
    qn = jnp.sqrt(jnp.sum(qf * qf, axis=1))
    groups = [range(j0, j0 + FAST_HEADS) for j0 in range(0, Q_PER_KV, FAST_HEADS)]
    n_chunks = s // FAST_CHUNK
    ms, scores = [], []
    for kv in range(N_KV_HEADS):
        bound = qn[kv * Q_PER_KV:(kv + 1) * Q_PER_KV] * (kn_ref[kv][:, :1] * BOUND_SLACK)
        ms.append([jnp.concatenate([bound[j:j + 1] for j in g], axis=1) for g in groups])
        scores.append([scores_fn(q_operand(kv, g), FAST_CHUNK) for g in groups])
    steps = [(kv, c) for kv in range(N_KV_HEADS) for c in range(n_chunks)]
    acc = [[None] * len(groups) for _ in range(N_KV_HEADS)]
    sc_next = [sf(0) for sf in scores[0]]
    for i, (kv, c) in enumerate(steps):
        for gi in range(len(groups)):
            sc = sc_next[gi]
            if i + 1 < len(steps):
                kv_n, c_n = steps[i + 1]
                sc_next[gi] = scores[kv_n][gi](c_n)
            p = jnp.exp2(sc - ms[kv][gi]).astype(BF16)
            pv = _dot(vta_ref[kv, :, c * FAST_CHUNK:(c + 1) * FAST_CHUNK], p)
            acc[kv][gi] = pv if acc[kv][gi] is None else acc[kv][gi] + pv
    accs = acc[0] + acc[1]
    finish(accs)
    den = accs[0][HEAD_DIM:HEAD_DIM + 1]
    for acc in accs[1:]:
        den = jnp.minimum(den, acc[HEAD_DIM:HEAD_DIM + 1])
    lmin = jnp.min(den)

    @pl.when(jnp.logical_not(lmin > L_MIN_OK))
    def _():
        accs = []
        for kv in range(N_KV_HEADS):
            scores = scores_fn(q_operand(kv), KEY_CHUNK)
            n_chunks = s // KEY_CHUNK
            m = None
            acc = None
            sc_next = scores(0)
            for c in range(n_chunks):
                sc = sc_next
                if c + 1 < n_chunks:
                    sc_next = scores(c + 1)
                mc = jnp.max(sc, axis=0, keepdims=True)
                m_new = mc if m is None else jnp.maximum(m, mc)
                p = jnp.exp2(sc - m_new).astype(BF16)
                pv = _dot(vta_ref[kv, :, c * KEY_CHUNK:(c + 1) * KEY_CHUNK], p)
                acc = pv if acc is None else acc * jnp.exp2(m - m_new) + pv
                m = m_new
            accs.append(acc)
        finish(accs)


def _attention(qt, k, vt, kn2, gcol, weights):
    b, _, s = qt.shape
    nq = s // Q_TILE
    n_steps = b * nq
    w_in_specs, w_out_specs, w_out_shapes = [], [], []
    for w in weights:
        _, rows, cols = w.shape
        step_rows = max(rows // n_steps, BF16_SUBLANES)
        last = rows // step_rows - 1
        w_in_specs.append(pl.BlockSpec(
            (1, step_rows, cols), lambda i, j, last=last: (0, jnp.minimum(i * nq + j, last), 0)))
        w_out_specs.append(pl.BlockSpec(
            (step_rows, cols), lambda i, j, last=last: (jnp.minimum(i * nq + j, last), 0)))
        w_out_shapes.append(jax.ShapeDtypeStruct((rows, cols), BF16))
    outs = pl.pallas_call(
        _attn_kernel,
        grid=(b, nq),
        in_specs=[
            pl.BlockSpec((1, D_ATTN, Q_TILE), lambda i, j: (i, 0, j)),
            pl.BlockSpec((1, s, D_KV), lambda i, j: (i, 0, 0)),
            pl.BlockSpec((1, D_KV, s), lambda i, j: (i, 0, 0)),
            pl.BlockSpec((1, N_KV_HEADS, s), lambda i, j: (i, 0, 0)),
            pl.BlockSpec((1, D_ATTN), lambda i, j: (0, 0)),
        ] + w_in_specs,
        out_specs=[pl.BlockSpec((1, Q_TILE, D_ATTN), lambda i, j: (i, j, 0))] + w_out_specs,
        out_shape=[jax.ShapeDtypeStruct((b, s, D_ATTN), BF16)] + w_out_shapes,
        scratch_shapes=[pltpu.VMEM((N_KV_HEADS, PV_ROWS, s), BF16),
                        pltpu.VMEM((N_KV_HEADS, 1, LANES), F32)],
        compiler_params=pltpu.CompilerParams(
            dimension_semantics=("arbitrary", "arbitrary"), vmem_limit_bytes=VMEM_LIMIT),
        name="attention",
    )(qt, k, vt, kn2, gcol, *weights)
    return outs[0], outs[1:]


def _shift_seg_down(x):
    row = lax.broadcasted_iota(jnp.int32, x.shape, 0)
    return jnp.where(row == 0, 0.0, pltpu.roll(x, 1, 0))


def _shift_seg_up(x):
    row = lax.broadcasted_iota(jnp.int32, x.shape, 0)
    return jnp.where(row == SUBLANES - 1, 0.0, pltpu.roll(x, SUBLANES - 1, 0))


def _sigmoid(x):
    return 0.5 * jnp.tanh(0.5 * x) + 0.5


def _gelu_tanh(x):
    k = 0.7978845608028654
    hx = 0.5 * x
    t = jnp.tanh(x * (k + (k * 0.044715) * (x * x)))
    return hx * t + hx


def _lru_kernel(xr_ref, cw_ref, cb_ref, wa_ref, wx_ref, ba_ref, bx_ref, lam_ref, y_ref,
                xh_ref, h0_ref, p0_ref, h1_ref, p1_ref, ystage_ref):
    t_len = xr_ref.shape[1]
    lanes = xr_ref.shape[3]
    n_chunks = t_len // GATE_CHUNK

    cw = cw_ref[0] * 0.5
    cb = cb_ref[...] * 0.5

    halo_lo = [_shift_seg_down(xr_ref[0, t_len - 2])[None], _shift_seg_down(xr_ref[0, t_len - 1])[None]]
    halo_hi = [_shift_seg_up(xr_ref[0, 0])[None]]

    def shifted(t0, n, k):
        lo = t0 + k - CONV_PAD_LEFT
        pieces = halo_lo[lo + CONV_PAD_LEFT:] if lo < 0 else []
        pieces = pieces + [xr_ref[0, max(lo, 0):min(lo + n, t_len)]]
        if lo + n > t_len:
            pieces = pieces + halo_hi[:lo + n - t_len]
        return pieces[0] if len(pieces) == 1 else jnp.concatenate(pieces, axis=0)

    for t0 in range(0, t_len, GATE_CHUNK):
        xh = cb[None]
        for kk in range(CONV_W):
            xh = xh + shifted(t0, GATE_CHUNK, kk) * cw[kk:kk + 1][None]
        xh_ref[t0:t0 + GATE_CHUNK] = xh
    lam = lam_ref[0]
    hc = (0.5 * LRU_C * 1.4426950408889634) * (
        jnp.minimum(lam, 0.0) - jnp.log1p(jnp.exp(-jnp.abs(lam))))

    zblk = jnp.zeros((LRU_BLOCK_W, LRU_BLOCK_W), F32)
    brow = lax.broadcasted_iota(jnp.int32, (lanes, 2 * lanes), 0)
    wg = []
    for e in range(2):
        rows = []
        for i in range(lanes // LRU_BLOCK_W):
            blocks = []
            for w_ref in (wa_ref, wx_ref):
                blocks += [w_ref[0, e, i] if c == i else zblk for c in range(lanes // LRU_BLOCK_W)]
            rows.append(jnp.concatenate(blocks, axis=1))
        b_half = 0.5 * jnp.concatenate([ba_ref[0, e:e + 1], bx_ref[0, e:e + 1]], axis=1)
        b_hi = b_half.astype(BF16).astype(F32)
        b_lo = b_half - b_hi
        bias_rows = jnp.where(brow == 0, b_hi, jnp.where(brow == 1, b_lo, 0.0))
        wg.append(jnp.concatenate([jnp.concatenate(rows, axis=0), bias_rows], axis=0).astype(BF16))
    ones_cols = jnp.where(
        lax.broadcasted_iota(jnp.int32, (SCAN_PART * SUBLANES, lanes), 1) < 2, 1.0, 0.0).astype(BF16)

    def gates(t0, e):
        xh = xh_ref[t0:t0 + SCAN_PART].reshape(SCAN_PART * SUBLANES, lanes)
        th = jnp.tanh(_dot(jnp.concatenate([xh.astype(BF16), ones_cols], axis=1), wg[e]))
        tr = th[:, :lanes]
        ti = th[:, lanes:]
        a = jnp.exp2(tr * hc[e:e + 1] + hc[e:e + 1])
        y = 1.0 - a * a
        mult = y * lax.rsqrt(jnp.maximum(y, 1e-30))
        u = (ti * xh + xh) * mult
        return a, u

    zero = jnp.zeros((SUBLANES, lanes), F32)
    one = jnp.ones((SUBLANES, lanes), F32)
    e0, q0, e1, q1 = zero, one, zero, one
    for part in range(t_len // SCAN_PART):
        t0 = part * SCAN_PART
        a, u = gates(t0, 0)
        for t in range(SCAN_PART):
            rows = slice(t * SUBLANES, (t + 1) * SUBLANES)
            e0 = a[rows] * e0 + u[rows]
            q0 = a[rows] * q0
            h0_ref[t0 + t] = e0
            p0_ref[t0 + t] = q0
        t1 = t_len - (part + 1) * SCAN_PART
        a, u = gates(t1, 1)
        for t in reversed(range(SCAN_PART)):
            rows = slice(t * SUBLANES, (t + 1) * SUBLANES)
            e1 = a[rows] * e1 + u[rows]
            q1 = a[rows] * q1
            h1_ref[t1 + t] = e1
            p1_ref[t1 + t] = q1

    c0 = zero
    c1 = zero
    for _ in range(N_SEG - 1):
        c0 = _shift_seg_down(e0 + q0 * c0)
        c1 = _shift_seg_up(e1 + q1 * c1)

    def out_body(i, carry):
        t0 = pl.multiple_of(i * GATE_CHUNK, GATE_CHUNK)
        sl = pl.ds(t0, GATE_CHUNK)
        h = (h0_ref[sl] + p0_ref[sl] * c0[None]) + (h1_ref[sl] + p1_ref[sl] * c1[None])
        ystage_ref[...] = h.reshape(GATE_CHUNK * SUBLANES, lanes)
        for seg in range(N_SEG):
            y_ref[0, pl.ds(seg * t_len + t0, GATE_CHUNK)] = (
                ystage_ref[pl.ds(seg, GATE_CHUNK, stride=SUBLANES)])
        return carry

    lax.fori_loop(0, n_chunks, out_body, 0)


def _lru(xr_il, conv_w, conv_b, wa, wx, ba, bx, lam):
    b, t_len, n_seg, d = xr_il.shape
    n_groups = d // LANES
    blocks_per_group = LANES // LRU_BLOCK_W
    slab = pltpu.VMEM((t_len, n_seg, LANES), F32)
    w_spec = pl.BlockSpec((1, 2, blocks_per_group, LRU_BLOCK_W, LRU_BLOCK_W),
                          lambda i, j: (0, 0, j, 0, 0))
    lane_group = pl.BlockSpec((1, 2, LANES), lambda i, j: (0, 0, j))
    return pl.pallas_call(
        _lru_kernel,
        grid=(b, n_groups),
        in_specs=[
            pl.BlockSpec((1, t_len, n_seg, LANES), lambda i, j: (i, 0, 0, j)),
            pl.BlockSpec((1, CONV_W, LANES), lambda i, j: (0, 0, j)),
            pl.BlockSpec((1, LANES), lambda i, j: (0, j)),
            w_spec, w_spec, lane_group, lane_group, lane_group,
        ],
        out_specs=pl.BlockSpec((1, t_len * n_seg, LANES), lambda i, j: (i, 0, j)),
        out_shape=jax.ShapeDtypeStruct((b, t_len * n_seg, d), F32),
        scratch_shapes=[slab] * 5 + [
            pltpu.VMEM((GATE_CHUNK * n_seg, LANES), F32)],
        compiler_params=pltpu.CompilerParams(
            dimension_semantics=("arbitrary", "arbitrary"), vmem_limit_bytes=VMEM_LIMIT),
        name="rglru",
    )(xr_il, conv_w, conv_b, wa, wx, ba, bx, lam)


def _tail_kernel(x_ref, ya_ref, hl_ref, xg_ref, p_ref, glru_ref, gmlp_ref, gple_ref, gfin_ref,
                 wout_ref, wup_ref, wdown_ref, wgate_ref, wproj_ref, o_ref):
    t_len = xg_ref.shape[1]
    for seg in range(x_ref.shape[1] // t_len):
        r = slice(seg * t_len, (seg + 1) * t_len)
        x = x_ref[0, r]
        yl = hl_ref[0, r] * _gelu_tanh(xg_ref[0, :, seg * D_LRU:(seg + 1) * D_LRU])
        yl_n = (yl * _rms(yl, -1) * glru_ref[...]).astype(BF16)
        h = x + _dot(ya_ref[0, r], wout_ref[:D_ATTN]) + _dot(yl_n, wout_ref[D_ATTN:])
        hn = (h * _rms(h, -1) * gmlp_ref[...]).astype(BF16)
        m = _dot(hn, wup_ref[...])
        act = jnp.square(jnp.maximum(m, 0.0)).astype(BF16)
        h = h + _dot(act, wdown_ref[...])
        hn = (h * _rms(h, -1) * gple_ref[...]).astype(BF16)
        gate = _sigmoid(_dot(hn, wgate_ref[...]))
        h = h + gate * _dot(p_ref[0, 0, r].astype(BF16), wproj_ref[...])
        o_ref[0, r] = h * _rms(h, -1) * gfin_ref[...]


def _tail(x, ya, hl, xg_il, p, glru, gmlp, gple, gfin, wout, wup, wdown, wgate, wproj):
    b, s, d = x.shape
    t = s // N_SEG
    tt = TAIL_SEGS * t
    const = lambda *_: (0, 0)
    weights = (wout, wup, wdown, wgate, wproj)

    def resident(arr):
        return pl.BlockSpec(arr.shape, const, pipeline_mode=pl.Buffered(1))

    return pl.pallas_call(
        _tail_kernel,
        grid=(b, N_SEG // TAIL_SEGS),
        in_specs=[
            pl.BlockSpec((1, tt, d), lambda i, j: (i, j, 0)),
            pl.BlockSpec((1, tt, D_ATTN), lambda i, j: (i, j, 0)),
            pl.BlockSpec((1, tt, D_LRU), lambda i, j: (i, j, 0)),
            pl.BlockSpec((1, t, TAIL_SEGS * D_LRU), lambda i, j: (i, 0, j)),
            pl.BlockSpec((1, 1, tt, D_PLE), lambda i, j: (0, i, j, 0)),
            resident(glru), resident(gmlp), resident(gple), resident(gfin),
        ] + [resident(w) for w in weights],
        out_specs=pl.BlockSpec((1, tt, d), lambda i, j: (i, j, 0)),
        out_shape=jax.ShapeDtypeStruct((b, s, d), F32),
        compiler_params=pltpu.CompilerParams(
            dimension_semantics=("arbitrary", "arbitrary"), vmem_limit_bytes=VMEM_LIMIT),
        name="tail",
    )(x, ya, hl, xg_il, p, glru, gmlp, gple, gfin, *weights)


def _rope_tables_t(seq_len):
    pos = np.arange(seq_len)
    row = (pos // GRID_W).astype(np.float32)
    col = (pos % GRID_W).astype(np.float32)
    inv_freq = np.float32(ROPE_THETA) ** (-np.arange(N_FREQ, dtype=np.float32) / np.float32(N_FREQ))
    ang_r = (row[None, :] * inv_freq[:, None]).astype(np.float32)
    ang_c = (col[None, :] * inv_freq[:, None]).astype(np.float32)
    cos_t = np.concatenate([np.cos(ang_r)] * 2 + [np.cos(ang_c)] * 2, axis=0)
    sin_t = np.concatenate([-np.sin(ang_r), np.sin(ang_r), -np.sin(ang_c), np.sin(ang_c)], axis=0)
    return jnp.asarray(cos_t, F32), jnp.asarray(sin_t, F32)


def kernel(x, p, mix_norm, w_in, q_norm, k_norm, conv_w, conv_b, lru_wa, lru_ba, lru_wx, lru_bx,
           lru_lambda, attn_out_norm, lru_out_norm, w_out, mlp_norm, w_up, w_down, ple_norm,
           w_ple_gate, w_ple_proj, final_norm):
    b, s, d = x.shape
    assert w_in.shape[0] == 1, "single-layer trunk: the final norm is fused into the layer tail"
    cos_t, sin_t = _rope_tables_t(s)
    wqkv_t = w_in[0, :, :D_ATTN + 2 * D_KV].T.astype(BF16)
    wrg = w_in[0, :, D_ATTN + 2 * D_KV:].astype(BF16)
    qt, k, vt, xr, xg, kn2 = _in_proj(x, mix_norm, wqkv_t, wrg, q_norm, k_norm, cos_t, sin_t)
    ya, tail_weights = _attention(qt, k, vt, kn2, attn_out_norm,
                                  (w_out, w_up, w_down, w_ple_gate, w_ple_proj))
    hl = _lru(xr.reshape(b, s // N_SEG, N_SEG, D_LRU), conv_w, conv_b, lru_wa, lru_wx, lru_ba, lru_bx,
              lru_lambda)
    return _tail(x, ya, hl, xg, p, lru_out_norm, mlp_norm, ple_norm, final_norm.reshape(1, -1),
                 *tail_weights)
```

```python
import jax
import jax.numpy as jnp
import numpy as np
from jax import lax
from jax.experimental import pallas as pl
from jax.experimental.pallas import tpu as pltpu

D_MODEL = 1024
GRID_W = 64
HEAD_DIM = 64
D_ATTN = 512
N_Q_HEADS = 8
N_KV_HEADS = 2
Q_PER_KV = 4
D_KV = 128
ROPE_THETA = 10000.0
N_FREQ = 16
D_LRU = 512
LRU_BLOCK_W = 64
LRU_C = 8.0
CONV_W = 4
CONV_PAD_LEFT = 2
D_FF = 4096
D_PLE = 256
NORM_EPS = 1e-6

SUBLANES = 8
BF16_SUBLANES = 16
LANES = 128

N_SEG = SUBLANES
Q_TILE = 256
KEY_CHUNK = 256
FAST_CHUNK = 256
FAST_HEADS = 1
PV_ROWS = 80
BOUND_SLACK = 1.001
L_MIN_OK = 2.0 ** -80
GATE_CHUNK = 64
SCAN_PART = 128
IN_SEGS = 2
IN_SPLIT = 2
MIB = 1024 * 1024
VMEM_LIMIT = {"in_proj": 36 * MIB, "attention": 16 * MIB, "rglru": 28 * MIB, "tail": 52 * MIB}
Q_SCALE = HEAD_DIM ** -0.5 * 1.4426950408889634

F32 = jnp.float32
BF16 = jnp.bfloat16


def _rms(x, axis):
    return lax.rsqrt(jnp.mean(x * x, axis=axis, keepdims=True) + NORM_EPS)


def _dot(a, b):
    return jnp.dot(a, b, preferred_element_type=F32)


def _as_column(row):
    n = row.shape[1]
    ri = lax.broadcasted_iota(jnp.int32, (n, n), 0)
    ci = lax.broadcasted_iota(jnp.int32, (n, n), 1)
    return jnp.sum(jnp.where(ri == ci, row, 0.0), axis=1, keepdims=True)


def _dot_nt(a, b):
    return lax.dot_general(a, b, (((1,), (1,)), ((), ())), preferred_element_type=F32)


def _norm_rope_t(xt, gcol, cos_t, sin_t, n_heads):
    t = xt.shape[1]
    x3 = xt.reshape(n_heads, HEAD_DIM, t)
    xn = x3 * _rms(x3, 1) * gcol[None]
    x5 = xn.reshape(n_heads * 2, 2, N_FREQ, t)
    xs = jnp.concatenate([x5[:, 1:2], x5[:, 0:1]], axis=1).reshape(n_heads, HEAD_DIM, t)
    out = xn * cos_t[None] + xs * sin_t[None]
    return out.reshape(n_heads * HEAD_DIM, t)


def _in_proj_kernel(x_ref, gmix_ref, wqkv_t_ref, wrg_ref, gq_ref, gk_ref, cos_ref, sin_ref,
                    qt_ref, k_ref, vt_ref, xr_ref, xg_ref, kn2_ref):
    gq = _as_column(gq_ref[...]) * Q_SCALE
    gk = _as_column(gk_ref[...])
    t_len = xr_ref.shape[1]
    rows = t_len // IN_SPLIT
    groups = [slice(i * rows, (i + 1) * rows) for i in range(x_ref.shape[1] // rows)]
    hns = []
    for r in groups:
        x = x_ref[0, r]
        hns.append((x * _rms(x, -1) * gmix_ref[...]).astype(BF16))
    lru_proj = lambda hn: _dot(hn, wrg_ref[...])
    qkv_proj = lambda hn: _dot_nt(wqkv_t_ref[...], hn)
    zrs = [lru_proj(hn) for hn in hns[:-1]]
    zts = [qkv_proj(hn) for hn in hns]
    zrs.append(lru_proj(hns[-1]))
    for r, zt, zr in zip(groups, zts, zrs):
        cos_t = cos_ref[:, r]
        sin_t = sin_ref[:, r]
        qt = _norm_rope_t(zt[:D_ATTN], gq, cos_t, sin_t, N_Q_HEADS)
        kt = _norm_rope_t(zt[D_ATTN:D_ATTN + D_KV], gk, cos_t, sin_t, N_KV_HEADS)
        qt_ref[0, :, r] = qt.astype(BF16)
        k_ref[0, r] = kt.T.astype(BF16)
        kf = kt.astype(BF16).astype(F32).reshape(N_KV_HEADS, HEAD_DIM, kt.shape[1])
        kn2_ref[0, :, r] = jnp.sum(kf * kf, axis=1)
        vt_ref[0, :, r] = zt[D_ATTN + D_KV:].astype(BF16)
        seg, t0 = divmod(r.start, t_len)
        xr_ref[0, t0:t0 + rows, seg * D_LRU:(seg + 1) * D_LRU] = zr[:, :D_LRU]
        xg_ref[0, t0:t0 + rows, seg * D_LRU:(seg + 1) * D_LRU] = zr[:, D_LRU:]


def _in_proj(x, gmix, wqkv_t, wrg, gq, gk, cos_t, sin_t):
    b, s, d = x.shape
    t = s // N_SEG
    tt = IN_SEGS * t
    const = lambda *_: (0, 0)
    return pl.pallas_call(
        _in_proj_kernel,
        grid=(b, N_SEG // IN_SEGS),
        in_specs=[
            pl.BlockSpec((1, tt, d), lambda i, j: (i, j, 0)),
            pl.BlockSpec((1, d), const),
            pl.BlockSpec(wqkv_t.shape, const),
            pl.BlockSpec(wrg.shape, const),
            pl.BlockSpec((1, HEAD_DIM), const),
            pl.BlockSpec((1, HEAD_DIM), const),
            pl.BlockSpec((HEAD_DIM, tt), lambda i, j: (0, j)),
            pl.BlockSpec((HEAD_DIM, tt), lambda i, j: (0, j)),
        ],
        out_specs=[
            pl.BlockSpec((1, D_ATTN, tt), lambda i, j: (i, 0, j)),
            pl.BlockSpec((1, tt, D_KV), lambda i, j: (i, j, 0)),
            pl.BlockSpec((1, D_KV, tt), lambda i, j: (i, 0, j)),
            pl.BlockSpec((1, t, IN_SEGS * D_LRU), lambda i, j: (i, 0, j)),
            pl.BlockSpec((1, t, IN_SEGS * D_LRU), lambda i, j: (i, 0, j)),
            pl.BlockSpec((1, N_KV_HEADS, tt), lambda i, j: (i, 0, j)),
        ],
        out_shape=[
            jax.ShapeDtypeStruct((b, D_ATTN, s), BF16),
            jax.ShapeDtypeStruct((b, s, D_KV), BF16),
            jax.ShapeDtypeStruct((b, D_KV, s), BF16),
            jax.ShapeDtypeStruct((b, t, N_SEG * D_LRU), F32),
            jax.ShapeDtypeStruct((b, t, N_SEG * D_LRU), F32),
            jax.ShapeDtypeStruct((b, N_KV_HEADS, s), F32),
        ],
        compiler_params=pltpu.CompilerParams(
            dimension_semantics=("arbitrary", "arbitrary"), vmem_limit_bytes=VMEM_LIMIT["in_proj"]),
        name="in_proj",
    )(x, gmix, wqkv_t, wrg, gq, gk, cos_t, sin_t)


def _attn_kernel(qt_ref, k_ref, vt_ref, kn2_ref, grow_ref, *refs):
    n_w = (len(refs) - 3) // 2
    w_f32_refs, y_ref, w_bf16_refs = refs[:n_w], refs[n_w], refs[n_w + 1:2 * n_w + 1]
    vta_ref, kn_ref = refs[2 * n_w + 1:]
    s = k_ref.shape[1]
    tq = qt_ref.shape[2]

    for w_in_ref, w_out_ref in zip(w_f32_refs, w_bf16_refs):
        w_out_ref[...] = w_in_ref[0].astype(BF16)

    @pl.when(pl.program_id(1) == 0)
    def _():
        vt = vt_ref[0]
        row = lax.broadcasted_iota(jnp.int32, (PV_ROWS - HEAD_DIM, s), 0)
        ones_pad = jnp.where(row == 0, 1.0, 0.0).astype(BF16)
        kn_max = jnp.sqrt(jnp.max(kn2_ref[0], axis=1, keepdims=True))
        for kv in range(N_KV_HEADS):
            vta_ref[kv] = jnp.concatenate([vt[kv * HEAD_DIM:(kv + 1) * HEAD_DIM], ones_pad], axis=0)
            kn_ref[kv] = jnp.broadcast_to(kn_max[kv:kv + 1], (1, LANES))

    qt = qt_ref[0]
    zeros = jnp.zeros((HEAD_DIM, tq), BF16)

    def q_operand(kv, heads=range(Q_PER_KV)):
        cols = []
        for j in heads:
            h = kv * Q_PER_KV + j
            qh = qt[h * HEAD_DIM:(h + 1) * HEAD_DIM]
            cols.append(jnp.concatenate([qh, zeros] if kv == 0 else [zeros, qh], axis=0))
        return jnp.concatenate(cols, axis=1)

    def finish(accs):
        outs = []
        for acc in accs:
            o = acc[:HEAD_DIM] / acc[HEAD_DIM:HEAD_DIM + 1]
            for j in range(acc.shape[1] // tq):
                outs.append(o[:, j * tq:(j + 1) * tq])
        ot = jnp.concatenate(outs, axis=0)
        y_ref[0] = ((ot * _rms(ot, 0)).T * grow_ref[...]).astype(BF16)

    def scores_fn(qst, chunk):
        return lambda c: _dot(k_ref[0, c * chunk:(c + 1) * chunk, :], qst)

    qf = qt.astype(F32).reshape(N_Q_HEADS, HEAD_DIM, tq)
    qn = jnp.sqrt(jnp.sum(qf * qf, axis=1))
    groups = [range(j0, j0 + FAST_HEADS) for j0 in range(0, Q_PER_KV, FAST_HEADS)]
    n_chunks = s // FAST_CHUNK
    ms, scores = [], []
    for kv in range(N_KV_HEADS):
        bound = qn[kv * Q_PER_KV:(kv + 1) * Q_PER_KV] * (kn_ref[kv][:, :1] * BOUND_SLACK)
        ms.append([jnp.concatenate([bound[j:j + 1] for j in g], axis=1) for g in groups])
        scores.append([scores_fn(q_operand(kv, g), FAST_CHUNK) for g in groups])
    steps = [(kv, c) for kv in range(N_KV_HEADS) for c in range(n_chunks)]
    acc = [[None] * len(groups) for _ in range(N_KV_HEADS)]
    sc_next = [sf(0) for sf in scores[0]]
    for i, (kv, c) in enumerate(steps):
        for gi in range(len(groups)):
            sc = sc_next[gi]
            if i + 1 < len(steps):
                kv_n, c_n = steps[i + 1]
                sc_next[gi] = scores[kv_n][gi](c_n)
            p = jnp.exp2(sc - ms[kv][gi]).astype(BF16)
            pv = _dot(vta_ref[kv, :, c * FAST_CHUNK:(c + 1) * FAST_CHUNK], p)
            acc[kv][gi] = pv if acc[kv][gi] is None else acc[kv][gi] + pv
    accs = acc[0] + acc[1]
    finish(accs)
    den = accs[0][HEAD_DIM:HEAD_DIM + 1]
    for acc in accs[1:]:
        den = jnp.minimum(den, acc[HEAD_DIM:HEAD_DIM + 1])
    lmin = jnp.min(den)

    @pl.when(jnp.logical_not(lmin > L_MIN_OK))
    def _():
        accs = []
        for kv in range(N_KV_HEADS):
            scores = scores_fn(q_operand(kv), KEY_CHUNK)
            n_chunks = s // KEY_CHUNK
            m = None
            acc = None
            sc_next = scores(0)
            for c in range(n_chunks):
                sc = sc_next
                if c + 1 < n_chunks:
                    sc_next = scores(c + 1)
                mc = jnp.max(sc, axis=0, keepdims=True)
                m_new = mc if m is None else jnp.maximum(m, mc)
                p = jnp.exp2(sc - m_new).astype(BF16)
                pv = _dot(vta_ref[kv, :, c * KEY_CHUNK:(c + 1) * KEY_CHUNK], p)
                acc = pv if acc is None else acc * jnp.exp2(m - m_new) + pv
                m = m_new
            accs.append(acc)
        finish(accs)


def _attention(qt, k, vt, kn2, gcol, weights):
    b, _, s = qt.shape
    nq = s // Q_TILE
    n_steps = b * nq
    w_in_specs, w_out_specs, w_out_shapes = [], [], []
    for w in weights:
        _, rows, cols = w.shape
        step_rows = max(rows // n_steps, BF16_SUBLANES)
        last = rows // step_rows - 1
        w_in_specs.append(pl.BlockSpec(
            (1, step_rows, cols), lambda i, j, last=last: (0, jnp.minimum(i * nq + j, last), 0)))
        w_out_specs.append(pl.BlockSpec(
            (step_rows, cols), lambda i, j, last=last: (jnp.minimum(i * nq + j, last), 0)))
        w_out_shapes.append(jax.ShapeDtypeStruct((rows, cols), BF16))
    outs = pl.pallas_call(
        _attn_kernel,
        grid=(b, nq),
        in_specs=[
            pl.BlockSpec((1, D_ATTN, Q_TILE), lambda i, j: (i, 0, j)),
            pl.BlockSpec((1, s, D_KV), lambda i, j: (i, 0, 0)),
            pl.BlockSpec((1, D_KV, s), lambda i, j: (i, 0, 0)),
            pl.BlockSpec((1, N_KV_HEADS, s), lambda i, j: (i, 0, 0)),
            pl.BlockSpec((1, D_ATTN), lambda i, j: (0, 0)),
        ] + w_in_specs,
        out_specs=[pl.BlockSpec((1, Q_TILE, D_ATTN), lambda i, j: (i, j, 0))] + w_out_specs,
        out_shape=[jax.ShapeDtypeStruct((b, s, D_ATTN), BF16)] + w_out_shapes,
        scratch_shapes=[pltpu.VMEM((N_KV_HEADS, PV_ROWS, s), BF16),
                        pltpu.VMEM((N_KV_HEADS, 1, LANES), F32)],
        compiler_params=pltpu.CompilerParams(
            dimension_semantics=("arbitrary", "arbitrary"), vmem_limit_bytes=VMEM_LIMIT["attention"]),
        name="attention",
    )(qt, k, vt, kn2, gcol, *weights)
    return outs[0], outs[1:]


def _shift_seg_down(x):
    row = lax.broadcasted_iota(jnp.int32, x.shape, 0)
    return jnp.where(row == 0, 0.0, pltpu.roll(x, 1, 0))


def _shift_seg_up(x):
    row = lax.broadcasted_iota(jnp.int32, x.shape, 0)
    return jnp.where(row == SUBLANES - 1, 0.0, pltpu.roll(x, SUBLANES - 1, 0))


def _sigmoid(x):
    return 0.5 * jnp.tanh(0.5 * x) + 0.5


def _gelu_tanh(x):
    k = 0.7978845608028654
    hx = 0.5 * x
    t = jnp.tanh(x * (k + (k * 0.044715) * (x * x)))
    return hx * t + hx


def _lru_kernel(xr_ref, cw_ref, cb_ref, wa_ref, wx_ref, ba_ref, bx_ref, lam_ref, y_ref,
                xh_ref, h0_ref, p0_ref, h1_ref, p1_ref, ystage_ref):
    t_len = xr_ref.shape[1]
    lanes = xr_ref.shape[3]
    n_chunks = t_len // GATE_CHUNK

    cw = cw_ref[0] * 0.5
    cb = cb_ref[...] * 0.5

    halo_lo = [_shift_seg_down(xr_ref[0, t_len - 2])[None], _shift_seg_down(xr_ref[0, t_len - 1])[None]]
    halo_hi = [_shift_seg_up(xr_ref[0, 0])[None]]

    def shifted(t0, n, k):
        lo = t0 + k - CONV_PAD_LEFT
        pieces = halo_lo[lo + CONV_PAD_LEFT:] if lo < 0 else []
        pieces = pieces + [xr_ref[0, max(lo, 0):min(lo + n, t_len)]]
        if lo + n > t_len:
            pieces = pieces + halo_hi[:lo + n - t_len]
        return pieces[0] if len(pieces) == 1 else jnp.concatenate(pieces, axis=0)

    for t0 in range(0, t_len, GATE_CHUNK):
        xh = cb[None]
        for kk in range(CONV_W):
            xh = xh + shifted(t0, GATE_CHUNK, kk) * cw[kk:kk + 1][None]
        xh_ref[t0:t0 + GATE_CHUNK] = xh
    lam = lam_ref[0]
    hc = (0.5 * LRU_C * 1.4426950408889634) * (
        jnp.minimum(lam, 0.0) - jnp.log1p(jnp.exp(-jnp.abs(lam))))

    zblk = jnp.zeros((LRU_BLOCK_W, LRU_BLOCK_W), F32)
    brow = lax.broadcasted_iota(jnp.int32, (lanes, 2 * lanes), 0)
    wg = []
    for e in range(2):
        rows = []
        for i in range(lanes // LRU_BLOCK_W):
            blocks = []
            for w_ref in (wa_ref, wx_ref):
                blocks += [w_ref[0, e, i] if c == i else zblk for c in range(lanes // LRU_BLOCK_W)]
            rows.append(jnp.concatenate(blocks, axis=1))
        b_half = 0.5 * jnp.concatenate([ba_ref[0, e:e + 1], bx_ref[0, e:e + 1]], axis=1)
        b_hi = b_half.astype(BF16).astype(F32)
        b_lo = b_half - b_hi
        bias_rows = jnp.where(brow == 0, b_hi, jnp.where(brow == 1, b_lo, 0.0))
        wg.append(jnp.concatenate([jnp.concatenate(rows, axis=0), bias_rows], axis=0).astype(BF16))
    ones_cols = jnp.where(
        lax.broadcasted_iota(jnp.int32, (SCAN_PART * SUBLANES, lanes), 1) < 2, 1.0, 0.0).astype(BF16)

    def gates(t0, e):
        xh = xh_ref[t0:t0 + SCAN_PART].reshape(SCAN_PART * SUBLANES, lanes)
        th = jnp.tanh(_dot(jnp.concatenate([xh.astype(BF16), ones_cols], axis=1), wg[e]))
        tr = th[:, :lanes]
        ti = th[:, lanes:]
        a = jnp.exp2(tr * hc[e:e + 1] + hc[e:e + 1])
        y = 1.0 - a * a
        mult = y * lax.rsqrt(jnp.maximum(y, 1e-30))
        u = (ti * xh + xh) * mult
        return a, u

    zero = jnp.zeros((SUBLANES, lanes), F32)
    one = jnp.ones((SUBLANES, lanes), F32)
    e0, q0, e1, q1 = zero, one, zero, one
    for part in range(t_len // SCAN_PART):
        t0 = part * SCAN_PART
        a, u = gates(t0, 0)
        for t in range(SCAN_PART):
            rows = slice(t * SUBLANES, (t + 1) * SUBLANES)
            e0 = a[rows] * e0 + u[rows]
            q0 = a[rows] * q0
            h0_ref[t0 + t] = e0
            p0_ref[t0 + t] = q0
        t1 = t_len - (part + 1) * SCAN_PART
        a, u = gates(t1, 1)
        for t in reversed(range(SCAN_PART)):
            rows = slice(t * SUBLANES, (t + 1) * SUBLANES)
            e1 = a[rows] * e1 + u[rows]
            q1 = a[rows] * q1
            h1_ref[t1 + t] = e1
            p1_ref[t1 + t] = q1

    c0 = zero
    c1 = zero
    for _ in range(N_SEG - 1):
        c0 = _shift_seg_down(e0 + q0 * c0)
        c1 = _shift_seg_up(e1 + q1 * c1)

    def out_body(i, carry):
        t0 = pl.multiple_of(i * GATE_CHUNK, GATE_CHUNK)
        sl = pl.ds(t0, GATE_CHUNK)
        h = (h0_ref[sl] + p0_ref[sl] * c0[None]) + (h1_ref[sl] + p1_ref[sl] * c1[None])
        ystage_ref[...] = h.reshape(GATE_CHUNK * SUBLANES, lanes)
        for seg in range(N_SEG):
            y_ref[0, pl.ds(seg * t_len + t0, GATE_CHUNK)] = (
                ystage_ref[pl.ds(seg, GATE_CHUNK, stride=SUBLANES)])
        return carry

    lax.fori_loop(0, n_chunks, out_body, 0)


def _lru(xr_il, conv_w, conv_b, wa, wx, ba, bx, lam):
    b, t_len, n_seg, d = xr_il.shape
    n_groups = d // LANES
    blocks_per_group = LANES // LRU_BLOCK_W
    slab = pltpu.VMEM((t_len, n_seg, LANES), F32)
    w_spec = pl.BlockSpec((1, 2, blocks_per_group, LRU_BLOCK_W, LRU_BLOCK_W),
                          lambda i, j: (0, 0, j, 0, 0))
    lane_group = pl.BlockSpec((1, 2, LANES), lambda i, j: (0, 0, j))
    return pl.pallas_call(
        _lru_kernel,
        grid=(b, n_groups),
        in_specs=[
            pl.BlockSpec((1, t_len, n_seg, LANES), lambda i, j: (i, 0, 0, j)),
            pl.BlockSpec((1, CONV_W, LANES), lambda i, j: (0, 0, j)),
            pl.BlockSpec((1, LANES), lambda i, j: (0, j)),
            w_spec, w_spec, lane_group, lane_group, lane_group,
        ],
        out_specs=pl.BlockSpec((1, t_len * n_seg, LANES), lambda i, j: (i, 0, j)),
        out_shape=jax.ShapeDtypeStruct((b, t_len * n_seg, d), F32),
        scratch_shapes=[slab] * 5 + [
            pltpu.VMEM((GATE_CHUNK * n_seg, LANES), F32)],
        compiler_params=pltpu.CompilerParams(
            dimension_semantics=("arbitrary", "arbitrary"), vmem_limit_bytes=VMEM_LIMIT["rglru"]),
        name="rglru",
    )(xr_il, conv_w, conv_b, wa, wx, ba, bx, lam)


def _tail_kernel(x_ref, ya_ref, hl_ref, xg_ref, p_ref, glru_ref, gmlp_ref, gple_ref, gfin_ref,
                 wout_ref, wup_ref, wdown_ref, wgate_ref, wproj_ref, o_ref):
    x = x_ref[0]
    yl = hl_ref[0] * _gelu_tanh(xg_ref[0])
    yl_n = (yl * _rms(yl, -1) * glru_ref[...]).astype(BF16)
    h = x + _dot(ya_ref[0], wout_ref[:D_ATTN]) + _dot(yl_n, wout_ref[D_ATTN:])
    hn = (h * _rms(h, -1) * gmlp_ref[...]).astype(BF16)
    m = _dot(hn, wup_ref[...])
    act = jnp.square(jnp.maximum(m, 0.0)).astype(BF16)
    h = h + _dot(act, wdown_ref[...])
    hn = (h * _rms(h, -1) * gple_ref[...]).astype(BF16)
    gate = _sigmoid(_dot(hn, wgate_ref[...]))
    h = h + gate * _dot(p_ref[0, 0].astype(BF16), wproj_ref[...])
    o_ref[0] = h * _rms(h, -1) * gfin_ref[...]


def _tail(x, ya, hl, xg_il, p, glru, gmlp, gple, gfin, wout, wup, wdown, wgate, wproj):
    b, s, d = x.shape
    t = s // N_SEG
    const = lambda *_: (0, 0)
    weights = (wout, wup, wdown, wgate, wproj)

    def resident(arr):
        return pl.BlockSpec(arr.shape, const, pipeline_mode=pl.Buffered(1))

    return pl.pallas_call(
        _tail_kernel,
        grid=(b, N_SEG),
        in_specs=[
            pl.BlockSpec((1, t, d), lambda i, j: (i, j, 0)),
            pl.BlockSpec((1, t, D_ATTN), lambda i, j: (i, j, 0)),
            pl.BlockSpec((1, t, D_LRU), lambda i, j: (i, j, 0)),
            pl.BlockSpec((1, t, D_LRU), lambda i, j: (i, 0, j)),
            pl.BlockSpec((1, 1, t, D_PLE), lambda i, j: (0, i, j, 0)),
            resident(glru), resident(gmlp), resident(gple), resident(gfin),
        ] + [resident(w) for w in weights],
        out_specs=pl.BlockSpec((1, t, d), lambda i, j: (i, j, 0)),
        out_shape=jax.ShapeDtypeStruct((b, s, d), F32),
        compiler_params=pltpu.CompilerParams(
            dimension_semantics=("arbitrary", "arbitrary"), vmem_limit_bytes=VMEM_LIMIT["tail"]),
        name="tail",
    )(x, ya, hl, xg_il, p, glru, gmlp, gple, gfin, *weights)


def _rope_tables_t(seq_len):
    pos = np.arange(seq_len)
    row = (pos // GRID_W).astype(np.float32)
    col = (pos % GRID_W).astype(np.float32)
    inv_freq = np.float32(ROPE_THETA) ** (-np.arange(N_FREQ, dtype=np.float32) / np.float32(N_FREQ))
    ang_r = (row[None, :] * inv_freq[:, None]).astype(np.float32)
    ang_c = (col[None, :] * inv_freq[:, None]).astype(np.float32)
    cos_t = np.concatenate([np.cos(ang_r)] * 2 + [np.cos(ang_c)] * 2, axis=0)
    sin_t = np.concatenate([-np.sin(ang_r), np.sin(ang_r), -np.sin(ang_c), np.sin(ang_c)], axis=0)
    return jnp.asarray(cos_t, F32), jnp.asarray(sin_t, F32)


def kernel(x, p, mix_norm, w_in, q_norm, k_norm, conv_w, conv_b, lru_wa, lru_ba, lru_wx, lru_bx,
           lru_lambda, attn_out_norm, lru_out_norm, w_out, mlp_norm, w_up, w_down, ple_norm,
           w_ple_gate, w_ple_proj, final_norm):
    b, s, d = x.shape
    assert w_in.shape[0] == 1, "single-layer trunk: the final norm is fused into the layer tail"
    cos_t, sin_t = _rope_tables_t(s)
    wqkv_t = w_in[0, :, :D_ATTN + 2 * D_KV].T.astype(BF16)
    wrg = w_in[0, :, D_ATTN + 2 * D_KV:].astype(BF16)
    qt, k, vt, xr, xg, kn2 = _in_proj(x, mix_norm, wqkv_t, wrg, q_norm, k_norm, cos_t, sin_t)
    ya, tail_weights = _attention(qt, k, vt, kn2, attn_out_norm,
                                  (w_out, w_up, w_down, w_ple_gate, w_ple_proj))
    hl = _lru(xr.reshape(b, s // N_SEG, N_SEG, D_LRU), conv_w, conv_b, lru_wa, lru_wx, lru_ba, lru_bx,
              lru_lambda)
    return _tail(x, ya, hl, xg, p, lru_out_norm, mlp_norm, ple_norm, final_norm.reshape(1, -1),
                 *tail_weights)
```

```python
import jax
import jax.numpy as jnp
import numpy as np
from jax import lax
from jax.experimental import pallas as pl
from jax.experimental.pallas import tpu as pltpu

D_MODEL = 1024
GRID_W = 64
HEAD_DIM = 64
D_ATTN = 512
N_Q_HEADS = 8
N_KV_HEADS = 2
Q_PER_KV = 4
D_KV = 128
ROPE_THETA = 10000.0
N_FREQ = 16
D_LRU = 512
LRU_BLOCK_W = 64
LRU_C = 8.0
CONV_W = 4
CONV_PAD_LEFT = 2
D_FF = 4096
D_PLE = 256
NORM_EPS = 1e-6

SUBLANES = 8
BF16_SUBLANES = 16
LANES = 128

N_SEG = SUBLANES
Q_TILE = 256
KEY_CHUNK = 256
FAST_CHUNK = 256
FAST_HEADS = 1
PV_ROWS = 80
BOUND_SLACK = 1.001
L_MIN_OK = 2.0 ** -80
GATE_CHUNK = 64
SCAN_PART = 128
IN_SEGS = 2
IN_SPLIT = 2
MIB = 1024 * 1024
VMEM_LIMIT = {"in_proj": 56 * MIB, "attention": 16 * MIB, "rglru": 56 * MIB, "tail": 56 * MIB}
Q_SCALE = HEAD_DIM ** -0.5 * 1.4426950408889634

F32 = jnp.float32
BF16 = jnp.bfloat16


def _rms(x, axis):
    return lax.rsqrt(jnp.mean(x * x, axis=axis, keepdims=True) + NORM_EPS)


def _dot(a, b):
    return jnp.dot(a, b, preferred_element_type=F32)


def _as_column(row):
    n = row.shape[1]
    ri = lax.broadcasted_iota(jnp.int32, (n, n), 0)
    ci = lax.broadcasted_iota(jnp.int32, (n, n), 1)
    return jnp.sum(jnp.where(ri == ci, row, 0.0), axis=1, keepdims=True)


def _dot_nt(a, b):
    return lax.dot_general(a, b, (((1,), (1,)), ((), ())), preferred_element_type=F32)


def _norm_rope_t(xt, gcol, cos_t, sin_t, n_heads):
    t = xt.shape[1]
    x3 = xt.reshape(n_heads, HEAD_DIM, t)
    xn = x3 * _rms(x3, 1) * gcol[None]
    x5 = xn.reshape(n_heads * 2, 2, N_FREQ, t)
    xs = jnp.concatenate([x5[:, 1:2], x5[:, 0:1]], axis=1).reshape(n_heads, HEAD_DIM, t)
    out = xn * cos_t[None] + xs * sin_t[None]
    return out.reshape(n_heads * HEAD_DIM, t)


def _in_proj_kernel(x_ref, gmix_ref, wqkv_t_ref, wrg_ref, gq_ref, gk_ref, cos_ref, sin_ref,
                    qt_ref, k_ref, vt_ref, xr_ref, xg_ref, kn2_ref):
    gq = _as_column(gq_ref[...]) * Q_SCALE
    gk = _as_column(gk_ref[...])
    t_len = xr_ref.shape[1]
    rows = t_len // IN_SPLIT
    groups = [slice(i * rows, (i + 1) * rows) for i in range(x_ref.shape[1] // rows)]
    hns = []
    for r in groups:
        x = x_ref[0, r]
        hns.append((x * _rms(x, -1) * gmix_ref[...]).astype(BF16))
    lru_proj = lambda hn: _dot(hn, wrg_ref[...])
    qkv_proj = lambda hn: _dot_nt(wqkv_t_ref[...], hn)
    zrs = [lru_proj(hn) for hn in hns[:-1]]
    zts = [qkv_proj(hn) for hn in hns]
    zrs.append(lru_proj(hns[-1]))
    for r, zt, zr in zip(groups, zts, zrs):
        cos_t = cos_ref[:, r]
        sin_t = sin_ref[:, r]
        qt = _norm_rope_t(zt[:D_ATTN], gq, cos_t, sin_t, N_Q_HEADS)
        kt = _norm_rope_t(zt[D_ATTN:D_ATTN + D_KV], gk, cos_t, sin_t, N_KV_HEADS)
        qt_ref[0, :, r] = qt.astype(BF16)
        k_ref[0, r] = kt.T.astype(BF16)
        kf = kt.astype(BF16).astype(F32).reshape(N_KV_HEADS, HEAD_DIM, kt.shape[1])
        kn2_ref[0, :, r] = jnp.sum(kf * kf, axis=1)
        vt_ref[0, :, r] = zt[D_ATTN + D_KV:].astype(BF16)
        seg, t0 = divmod(r.start, t_len)
        xr_ref[0, t0:t0 + rows, seg * D_LRU:(seg + 1) * D_LRU] = zr[:, :D_LRU]
        xg_ref[0, t0:t0 + rows, seg * D_LRU:(seg + 1) * D_LRU] = zr[:, D_LRU:]


def _in_proj(x, gmix, wqkv_t, wrg, gq, gk, cos_t, sin_t):
    b, s, d = x.shape
    t = s // N_SEG
    tt = IN_SEGS * t
    const = lambda *_: (0, 0)
    return pl.pallas_call(
        _in_proj_kernel,
        grid=(b, N_SEG // IN_SEGS),
        in_specs=[
            pl.BlockSpec((1, tt, d), lambda i, j: (i, j, 0)),
            pl.BlockSpec((1, d), const),
            pl.BlockSpec(wqkv_t.shape, const),
            pl.BlockSpec(wrg.shape, const),
            pl.BlockSpec((1, HEAD_DIM), const),
            pl.BlockSpec((1, HEAD_DIM), const),
            pl.BlockSpec((HEAD_DIM, tt), lambda i, j: (0, j)),
            pl.BlockSpec((HEAD_DIM, tt), lambda i, j: (0, j)),
        ],
        out_specs=[
            pl.BlockSpec((1, D_ATTN, tt), lambda i, j: (i, 0, j)),
            pl.BlockSpec((1, tt, D_KV), lambda i, j: (i, j, 0)),
            pl.BlockSpec((1, D_KV, tt), lambda i, j: (i, 0, j)),
            pl.BlockSpec((1, t, IN_SEGS * D_LRU), lambda i, j: (i, 0, j)),
            pl.BlockSpec((1, t, IN_SEGS * D_LRU), lambda i, j: (i, 0, j)),
            pl.BlockSpec((1, N_KV_HEADS, tt), lambda i, j: (i, 0, j)),
        ],
        out_shape=[
            jax.ShapeDtypeStruct((b, D_ATTN, s), BF16),
            jax.ShapeDtypeStruct((b, s, D_KV), BF16),
            jax.ShapeDtypeStruct((b, D_KV, s), BF16),
            jax.ShapeDtypeStruct((b, t, N_SEG * D_LRU), F32),
            jax.ShapeDtypeStruct((b, t, N_SEG * D_LRU), F32),
            jax.ShapeDtypeStruct((b, N_KV_HEADS, s), F32),
        ],
        compiler_params=pltpu.CompilerParams(
            dimension_semantics=("arbitrary", "arbitrary"), vmem_limit_bytes=VMEM_LIMIT["in_proj"]),
        name="in_proj",
    )(x, gmix, wqkv_t, wrg, gq, gk, cos_t, sin_t)


def _attn_kernel(qt_ref, k_ref, vt_ref, kn2_ref, grow_ref, *refs):
    n_w = (len(refs) - 3) // 2
    w_f32_refs, y_ref, w_bf16_refs = refs[:n_w], refs[n_w], refs[n_w + 1:2 * n_w + 1]
    vta_ref, kn_ref = refs[2 * n_w + 1:]
    s = k_ref.shape[1]
    tq = qt_ref.shape[2]

    for w_in_ref, w_out_ref in zip(w_f32_refs, w_bf16_refs):
        w_out_ref[...] = w_in_ref[0].astype(BF16)

    @pl.when(pl.program_id(1) == 0)
    def _():
        vt = vt_ref[0]
        row = lax.broadcasted_iota(jnp.int32, (PV_ROWS - HEAD_DIM, s), 0)
        ones_pad = jnp.where(row == 0, 1.0, 0.0).astype(BF16)
        kn_max = jnp.sqrt(jnp.max(kn2_ref[0], axis=1, keepdims=True))
        for kv in range(N_KV_HEADS):
            vta_ref[kv] = jnp.concatenate([vt[kv * HEAD_DIM:(kv + 1) * HEAD_DIM], ones_pad], axis=0)
            kn_ref[kv] = jnp.broadcast_to(kn_max[kv:kv + 1], (1, LANES))

    qt = qt_ref[0]
    zeros = jnp.zeros((HEAD_DIM, tq), BF16)

    def q_operand(kv, heads=range(Q_PER_KV)):
        cols = []
        for j in heads:
            h = kv * Q_PER_KV + j
            qh = qt[h * HEAD_DIM:(h + 1) * HEAD_DIM]
            cols.append(jnp.concatenate([qh, zeros] if kv == 0 else [zeros, qh], axis=0))
        return jnp.concatenate(cols, axis=1)

    def finish(accs):
        outs = []
        for acc in accs:
            o = acc[:HEAD_DIM] / acc[HEAD_DIM:HEAD_DIM + 1]
            for j in range(acc.shape[1] // tq):
                outs.append(o[:, j * tq:(j + 1) * tq])
        ot = jnp.concatenate(outs, axis=0)
        y_ref[0] = ((ot * _rms(ot, 0)).T * grow_ref[...]).astype(BF16)

    def scores_fn(qst, chunk):
        return lambda c: _dot(k_ref[0, c * chunk:(c + 1) * chunk, :], qst)

    qf = qt.astype(F32).reshape(N_Q_HEADS, HEAD_DIM, tq)
    qn = jnp.sqrt(jnp.sum(qf * qf, axis=1))
    groups = [range(j0, j0 + FAST_HEADS) for j0 in range(0, Q_PER_KV, FAST_HEADS)]
    n_chunks = s // FAST_CHUNK
    ms, scores = [], []
    for kv in range(N_KV_HEADS):
        bound = qn[kv * Q_PER_KV:(kv + 1) * Q_PER_KV] * (kn_ref[kv][:, :1] * BOUND_SLACK)
        ms.append([jnp.concatenate([bound[j:j + 1] for j in g], axis=1) for g in groups])
        scores.append([scores_fn(q_operand(kv, g), FAST_CHUNK) for g in groups])
    steps = [(kv, c) for kv in range(N_KV_HEADS) for c in range(n_chunks)]
    acc = [[None] * len(groups) for _ in range(N_KV_HEADS)]
    sc_next = [sf(0) for sf in scores[0]]
    for i, (kv, c) in enumerate(steps):
        for gi in range(len(groups)):
            sc = sc_next[gi]
            if i + 1 < len(steps):
                kv_n, c_n = steps[i + 1]
                sc_next[gi] = scores[kv_n][gi](c_n)
            p = jnp.exp2(sc - ms[kv][gi]).astype(BF16)
            pv = _dot(vta_ref[kv, :, c * FAST_CHUNK:(c + 1) * FAST_CHUNK], p)
            acc[kv][gi] = pv if acc[kv][gi] is None else acc[kv][gi] + pv
    accs = acc[0] + acc[1]
    finish(accs)
    den = accs[0][HEAD_DIM:HEAD_DIM + 1]
    for acc in accs[1:]:
        den = jnp.minimum(den, acc[HEAD_DIM:HEAD_DIM + 1])
    lmin = jnp.min(den)

    @pl.when(jnp.logical_not(lmin > L_MIN_OK))
    def _():
        accs = []
        for kv in range(N_KV_HEADS):
            scores = scores_fn(q_operand(kv), KEY_CHUNK)
            n_chunks = s // KEY_CHUNK
            m = None
            acc = None
            sc_next = scores(0)
            for c in range(n_chunks):
                sc = sc_next
                if c + 1 < n_chunks:
                    sc_next = scores(c + 1)
                mc = jnp.max(sc, axis=0, keepdims=True)
                m_new = mc if m is None else jnp.maximum(m, mc)
                p = jnp.exp2(sc - m_new).astype(BF16)
                pv = _dot(vta_ref[kv, :, c * KEY_CHUNK:(c + 1) * KEY_CHUNK], p)
                acc = pv if acc is None else acc * jnp.exp2(m - m_new) + pv
                m = m_new
            accs.append(acc)
        finish(accs)


def _attention(qt, k, vt, kn2, gcol, weights):
    b, _, s = qt.shape
    nq = s // Q_TILE
    n_steps = b * nq
    w_in_specs, w_out_specs, w_out_shapes = [], [], []
    for w in weights:
        _, rows, cols = w.shape
        step_rows = max(rows // n_steps, BF16_SUBLANES)
        last = rows // step_rows - 1
        w_in_specs.append(pl.BlockSpec(
            (1, step_rows, cols), lambda i, j, last=last: (0, jnp.minimum(i * nq + j, last), 0)))
        w_out_specs.append(pl.BlockSpec(
            (step_rows, cols), lambda i, j, last=last: (jnp.minimum(i * nq + j, last), 0)))
        w_out_shapes.append(jax.ShapeDtypeStruct((rows, cols), BF16))
    outs = pl.pallas_call(
        _attn_kernel,
        grid=(b, nq),
        in_specs=[
            pl.BlockSpec((1, D_ATTN, Q_TILE), lambda i, j: (i, 0, j)),
            pl.BlockSpec((1, s, D_KV), lambda i, j: (i, 0, 0)),
            pl.BlockSpec((1, D_KV, s), lambda i, j: (i, 0, 0)),
            pl.BlockSpec((1, N_KV_HEADS, s), lambda i, j: (i, 0, 0)),
            pl.BlockSpec((1, D_ATTN), lambda i, j: (0, 0)),
        ] + w_in_specs,
        out_specs=[pl.BlockSpec((1, Q_TILE, D_ATTN), lambda i, j: (i, j, 0))] + w_out_specs,
        out_shape=[jax.ShapeDtypeStruct((b, s, D_ATTN), BF16)] + w_out_shapes,
        scratch_shapes=[pltpu.VMEM((N_KV_HEADS, PV_ROWS, s), BF16),
                        pltpu.VMEM((N_KV_HEADS, 1, LANES), F32)],
        compiler_params=pltpu.CompilerParams(
            dimension_semantics=("arbitrary", "arbitrary"), vmem_limit_bytes=VMEM_LIMIT["attention"]),
        name="attention",
    )(qt, k, vt, kn2, gcol, *weights)
    return outs[0], outs[1:]


def _shift_seg_down(x):
    row = lax.broadcasted_iota(jnp.int32, x.shape, 0)
    return jnp.where(row == 0, 0.0, pltpu.roll(x, 1, 0))


def _shift_seg_up(x):
    row = lax.broadcasted_iota(jnp.int32, x.shape, 0)
    return jnp.where(row == SUBLANES - 1, 0.0, pltpu.roll(x, SUBLANES - 1, 0))


def _sigmoid(x):
    return 0.5 * jnp.tanh(0.5 * x) + 0.5


def _gelu_tanh(x):
    k = 0.7978845608028654
    hx = 0.5 * x
    t = jnp.tanh(x * (k + (k * 0.044715) * (x * x)))
    return hx * t + hx


def _lru_kernel(xr_ref, cw_ref, cb_ref, wa_ref, wx_ref, ba_ref, bx_ref, lam_ref, y_ref,
                xh_ref, h0_ref, p0_ref, h1_ref, p1_ref, ystage_ref):
    t_len = xr_ref.shape[1]
    lanes = xr_ref.shape[3]
    n_chunks = t_len // GATE_CHUNK

    cw = cw_ref[0] * 0.5
    cb = cb_ref[...] * 0.5

    halo_lo = [_shift_seg_down(xr_ref[0, t_len - 2])[None], _shift_seg_down(xr_ref[0, t_len - 1])[None]]
    halo_hi = [_shift_seg_up(xr_ref[0, 0])[None]]

    def shifted(t0, n, k):
        lo = t0 + k - CONV_PAD_LEFT
        pieces = halo_lo[lo + CONV_PAD_LEFT:] if lo < 0 else []
        pieces = pieces + [xr_ref[0, max(lo, 0):min(lo + n, t_len)]]
        if lo + n > t_len:
            pieces = pieces + halo_hi[:lo + n - t_len]
        return pieces[0] if len(pieces) == 1 else jnp.concatenate(pieces, axis=0)

    for t0 in range(0, t_len, GATE_CHUNK):
        xh = cb[None]
        for kk in range(CONV_W):
            xh = xh + shifted(t0, GATE_CHUNK, kk) * cw[kk:kk + 1][None]
        xh_ref[t0:t0 + GATE_CHUNK] = xh
    lam = lam_ref[0]
    hc = (0.5 * LRU_C * 1.4426950408889634) * (
        jnp.minimum(lam, 0.0) - jnp.log1p(jnp.exp(-jnp.abs(lam))))

    zblk = jnp.zeros((LRU_BLOCK_W, LRU_BLOCK_W), F32)
    brow = lax.broadcasted_iota(jnp.int32, (lanes, 2 * lanes), 0)
    wg = []
    for e in range(2):
        rows = []
        for i in range(lanes // LRU_BLOCK_W):
            blocks = []
            for w_ref in (wa_ref, wx_ref):
                blocks += [w_ref[0, e, i] if c == i else zblk for c in range(lanes // LRU_BLOCK_W)]
            rows.append(jnp.concatenate(blocks, axis=1))
        b_half = 0.5 * jnp.concatenate([ba_ref[0, e:e + 1], bx_ref[0, e:e + 1]], axis=1)
        b_hi = b_half.astype(BF16).astype(F32)
        b_lo = b_half - b_hi
        bias_rows = jnp.where(brow == 0, b_hi, jnp.where(brow == 1, b_lo, 0.0))
        wg.append(jnp.concatenate([jnp.concatenate(rows, axis=0), bias_rows], axis=0).astype(BF16))
    ones_cols = jnp.where(
        lax.broadcasted_iota(jnp.int32, (SCAN_PART * SUBLANES, lanes), 1) < 2, 1.0, 0.0).astype(BF16)

    def gates(t0, e):
        xh = xh_ref[t0:t0 + SCAN_PART].reshape(SCAN_PART * SUBLANES, lanes)
        th = jnp.tanh(_dot(jnp.concatenate([xh.astype(BF16), ones_cols], axis=1), wg[e]))
        tr = th[:, :lanes]
        ti = th[:, lanes:]
        a = jnp.exp2(tr * hc[e:e + 1] + hc[e:e + 1])
        y = 1.0 - a * a
        mult = y * lax.rsqrt(jnp.maximum(y, 1e-30))
        u = (ti * xh + xh) * mult
        return a, u

    zero = jnp.zeros((SUBLANES, lanes), F32)
    one = jnp.ones((SUBLANES, lanes), F32)
    e0, q0, e1, q1 = zero, one, zero, one
    for part in range(t_len // SCAN_PART):
        t0 = part * SCAN_PART
        a, u = gates(t0, 0)
        for t in range(SCAN_PART):
            rows = slice(t * SUBLANES, (t + 1) * SUBLANES)
            e0 = a[rows] * e0 + u[rows]
            q0 = a[rows] * q0
            h0_ref[t0 + t] = e0
            p0_ref[t0 + t] = q0
        t1 = t_len - (part + 1) * SCAN_PART
        a, u = gates(t1, 1)
        for t in reversed(range(SCAN_PART)):
            rows = slice(t * SUBLANES, (t + 1) * SUBLANES)
            e1 = a[rows] * e1 + u[rows]
            q1 = a[rows] * q1
            h1_ref[t1 + t] = e1
            p1_ref[t1 + t] = q1

    c0 = zero
    c1 = zero
    for _ in range(N_SEG - 1):
        c0 = _shift_seg_down(e0 + q0 * c0)
        c1 = _shift_seg_up(e1 + q1 * c1)

    def out_body(i, carry):
        t0 = pl.multiple_of(i * GATE_CHUNK, GATE_CHUNK)
        sl = pl.ds(t0, GATE_CHUNK)
        h = (h0_ref[sl] + p0_ref[sl] * c0[None]) + (h1_ref[sl] + p1_ref[sl] * c1[None])
        ystage_ref[...] = h.reshape(GATE_CHUNK * SUBLANES, lanes)
        for seg in range(N_SEG):
            y_ref[0, pl.ds(seg * t_len + t0, GATE_CHUNK)] = (
                ystage_ref[pl.ds(seg, GATE_CHUNK, stride=SUBLANES)])
        return carry

    lax.fori_loop(0, n_chunks, out_body, 0)


def _lru(xr_il, conv_w, conv_b, wa, wx, ba, bx, lam):
    b, t_len, n_seg, d = xr_il.shape
    n_groups = d // LANES
    blocks_per_group = LANES // LRU_BLOCK_W
    slab = pltpu.VMEM((t_len, n_seg, LANES), F32)
    w_spec = pl.BlockSpec((1, 2, blocks_per_group, LRU_BLOCK_W, LRU_BLOCK_W),
                          lambda i, j: (0, 0, j, 0, 0))
    lane_group = pl.BlockSpec((1, 2, LANES), lambda i, j: (0, 0, j))
    return pl.pallas_call(
        _lru_kernel,
        grid=(b, n_groups),
        in_specs=[
            pl.BlockSpec((1, t_len, n_seg, LANES), lambda i, j: (i, 0, 0, j)),
            pl.BlockSpec((1, CONV_W, LANES), lambda i, j: (0, 0, j)),
            pl.BlockSpec((1, LANES), lambda i, j: (0, j)),
            w_spec, w_spec, lane_group, lane_group, lane_group,
        ],
        out_specs=pl.BlockSpec((1, t_len * n_seg, LANES), lambda i, j: (i, 0, j)),
        out_shape=jax.ShapeDtypeStruct((b, t_len * n_seg, d), F32),
        scratch_shapes=[slab] * 5 + [
            pltpu.VMEM((GATE_CHUNK * n_seg, LANES), F32)],
        compiler_params=pltpu.CompilerParams(
            dimension_semantics=("arbitrary", "arbitrary"), vmem_limit_bytes=VMEM_LIMIT["rglru"]),
        name="rglru",
    )(xr_il, conv_w, conv_b, wa, wx, ba, bx, lam)


def _tail_kernel(x_ref, ya_ref, hl_ref, xg_ref, p_ref, glru_ref, gmlp_ref, gple_ref, gfin_ref,
                 wout_ref, wup_ref, wdown_ref, wgate_ref, wproj_ref, o_ref):
    x = x_ref[0]
    yl = hl_ref[0] * _gelu_tanh(xg_ref[0])
    yl_n = (yl * _rms(yl, -1) * glru_ref[...]).astype(BF16)
    h = x + _dot(ya_ref[0], wout_ref[:D_ATTN]) + _dot(yl_n, wout_ref[D_ATTN:])
    hn = (h * _rms(h, -1) * gmlp_ref[...]).astype(BF16)
    m = _dot(hn, wup_ref[...])
    act = jnp.square(jnp.maximum(m, 0.0)).astype(BF16)
    h = h + _dot(act, wdown_ref[...])
    hn = (h * _rms(h, -1) * gple_ref[...]).astype(BF16)
    gate = _sigmoid(_dot(hn, wgate_ref[...]))
    h = h + gate * _dot(p_ref[0, 0].astype(BF16), wproj_ref[...])
    o_ref[0] = h * _rms(h, -1) * gfin_ref[...]


def _tail(x, ya, hl, xg_il, p, glru, gmlp, gple, gfin, wout, wup, wdown, wgate, wproj):
    b, s, d = x.shape
    t = s // N_SEG
    const = lambda *_: (0, 0)
    weights = (wout, wup, wdown, wgate, wproj)

    def resident(arr):
        return pl.BlockSpec(arr.shape, const, pipeline_mode=pl.Buffered(1))

    return pl.pallas_call(
        _tail_kernel,
        grid=(b, N_SEG),
        in_specs=[
            pl.BlockSpec((1, t, d), lambda i, j: (i, j, 0)),
            pl.BlockSpec((1, t, D_ATTN), lambda i, j: (i, j, 0)),
            pl.BlockSpec((1, t, D_LRU), lambda i, j: (i, j, 0)),
            pl.BlockSpec((1, t, D_LRU), lambda i, j: (i, 0, j)),
            pl.BlockSpec((1, 1, t, D_PLE), lambda i, j: (0, i, j, 0)),
            resident(glru), resident(gmlp), resident(gple), resident(gfin),
        ] + [resident(w) for w in weights],
        out_specs=pl.BlockSpec((1, t, d), lambda i, j: (i, j, 0)),
        out_shape=jax.ShapeDtypeStruct((b, s, d), F32),
        compiler_params=pltpu.CompilerParams(
            dimension_semantics=("arbitrary", "arbitrary"), vmem_limit_bytes=VMEM_LIMIT["tail"]),
        name="tail",
    )(x, ya, hl, xg_il, p, glru, gmlp, gple, gfin, *weights)


def _rope_tables_t(seq_len):
    pos = np.arange(seq_len)
    row = (pos // GRID_W).astype(np.float32)
    col = (pos % GRID_W).astype(np.float32)
    inv_freq = np.float32(ROPE_THETA) ** (-np.arange(N_FREQ, dtype=np.float32) / np.float32(N_FREQ))
    ang_r = (row[None, :] * inv_freq[:, None]).astype(np.float32)
    ang_c = (col[None, :] * inv_freq[:, None]).astype(np.float32)
    cos_t = np.concatenate([np.cos(ang_r)] * 2 + [np.cos(ang_c)] * 2, axis=0)
    sin_t = np.concatenate([-np.sin(ang_r), np.sin(ang_r), -np.sin(ang_c), np.sin(ang_c)], axis=0)
    return jnp.asarray(cos_t, F32), jnp.asarray(sin_t, F32)


def kernel(x, p, mix_norm, w_in, q_norm, k_norm, conv_w, conv_b, lru_wa, lru_ba, lru_wx, lru_bx,
           lru_lambda, attn_out_norm, lru_out_norm, w_out, mlp_norm, w_up, w_down, ple_norm,
           w_ple_gate, w_ple_proj, final_norm):
    b, s, d = x.shape
    assert w_in.shape[0] == 1, "single-layer trunk: the final norm is fused into the layer tail"
    cos_t, sin_t = _rope_tables_t(s)
    wqkv_t = w_in[0, :, :D_ATTN + 2 * D_KV].T.astype(BF16)
    wrg = w_in[0, :, D_ATTN + 2 * D_KV:].astype(BF16)
    qt, k, vt, xr, xg, kn2 = _in_proj(x, mix_norm, wqkv_t, wrg, q_norm, k_norm, cos_t, sin_t)
    ya, tail_weights = _attention(qt, k, vt, kn2, attn_out_norm,
                                  (w_out, w_up, w_down, w_ple_gate, w_ple_proj))
    hl = _lru(xr.reshape(b, s // N_SEG, N_SEG, D_LRU), conv_w, conv_b, lru_wa, lru_wx, lru_ba, lru_bx,
              lru_lambda)
    return _tail(x, ya, hl, xg, p, lru_out_norm, mlp_norm, ple_norm, final_norm.reshape(1, -1),
                 *tail_weights)
```

```python
import jax
import jax.numpy as jnp
import numpy as np
from jax import lax
from jax.experimental import pallas as pl
from jax.experimental.pallas import tpu as pltpu

D_MODEL = 1024
GRID_W = 64
HEAD_DIM = 64
D_ATTN = 512
N_Q_HEADS = 8
N_KV_HEADS = 2
Q_PER_KV = 4
D_KV = 128
ROPE_THETA = 10000.0
N_FREQ = 16
D_LRU = 512
LRU_BLOCK_W = 64
LRU_C = 8.0
CONV_W = 4
CONV_PAD_LEFT = 2
D_FF = 4096
D_PLE = 256
NORM_EPS = 1e-6

SUBLANES = 8
BF16_SUBLANES = 16
LANES = 128

N_SEG = SUBLANES
Q_TILE = 256
KEY_CHUNK = 256
FAST_CHUNK = 256
FAST_HEADS = 1
PV_ROWS = 80
BOUND_SLACK = 1.001
L_MIN_OK = 2.0 ** -80
GATE_CHUNK = 64
SCAN_PART = 32
IN_SEGS = 4
IN_SPLIT = 2
VMEM_LIMIT = 56 * 1024 * 1024
Q_SCALE = HEAD_DIM ** -0.5 * 1.4426950408889634

F32 = jnp.float32
BF16 = jnp.bfloat16


def _rms(x, axis):
    return lax.rsqrt(jnp.mean(x * x, axis=axis, keepdims=True) + NORM_EPS)


def _dot(a, b):
    return jnp.dot(a, b, preferred_element_type=F32)


def _as_column(row):
    n = row.shape[1]
    ri = lax.broadcasted_iota(jnp.int32, (n, n), 0)
    ci = lax.broadcasted_iota(jnp.int32, (n, n), 1)
    return jnp.sum(jnp.where(ri == ci, row, 0.0), axis=1, keepdims=True)


def _dot_nt(a, b):
    return lax.dot_general(a, b, (((1,), (1,)), ((), ())), preferred_element_type=F32)


def _norm_rope_t(xt, gcol, cos_t, sin_t, n_heads):
    t = xt.shape[1]
    x3 = xt.reshape(n_heads, HEAD_DIM, t)
    xn = x3 * _rms(x3, 1) * gcol[None]
    x5 = xn.reshape(n_heads * 2, 2, N_FREQ, t)
    xs = jnp.concatenate([x5[:, 1:2], x5[:, 0:1]], axis=1).reshape(n_heads, HEAD_DIM, t)
    out = xn * cos_t[None] + xs * sin_t[None]
    return out.reshape(n_heads * HEAD_DIM, t)


def _in_proj_kernel(x_ref, gmix_ref, wqkv_t_ref, wrg_ref, gq_ref, gk_ref, cos_ref, sin_ref,
                    qt_ref, k_ref, vt_ref, xr_ref, xg_ref, kn2_ref):
    gq = _as_column(gq_ref[...]) * Q_SCALE
    gk = _as_column(gk_ref[...])
    t_len = xr_ref.shape[1]
    rows = t_len // IN_SPLIT
    groups = [slice(i * rows, (i + 1) * rows) for i in range(x_ref.shape[1] // rows)]
    hns = []
    for r in groups:
        x = x_ref[0, r]
        hns.append((x * _rms(x, -1) * gmix_ref[...]).astype(BF16))
    lru_proj = lambda hn: _dot(hn, wrg_ref[...])
    qkv_proj = lambda hn: _dot_nt(wqkv_t_ref[...], hn)
    zrs = [lru_proj(hn) for hn in hns[:-1]]
    zts = [qkv_proj(hn) for hn in hns]
    zrs.append(lru_proj(hns[-1]))
    for r, zt, zr in zip(groups, zts, zrs):
        cos_t = cos_ref[:, r]
        sin_t = sin_ref[:, r]
        qt = _norm_rope_t(zt[:D_ATTN], gq, cos_t, sin_t, N_Q_HEADS)
        kt = _norm_rope_t(zt[D_ATTN:D_ATTN + D_KV], gk, cos_t, sin_t, N_KV_HEADS)
        qt_ref[0, :, r] = qt.astype(BF16)
        k_ref[0, r] = kt.T.astype(BF16)
        kf = kt.astype(BF16).astype(F32).reshape(N_KV_HEADS, HEAD_DIM, kt.shape[1])
        kn2_ref[0, :, r] = jnp.sum(kf * kf, axis=1)
        vt_ref[0, :, r] = zt[D_ATTN + D_KV:].astype(BF16)
        seg, t0 = divmod(r.start, t_len)
        xr_ref[0, t0:t0 + rows, seg * D_LRU:(seg + 1) * D_LRU] = zr[:, :D_LRU]
        xg_ref[0, t0:t0 + rows, seg * D_LRU:(seg + 1) * D_LRU] = zr[:, D_LRU:]


def _in_proj(x, gmix, wqkv_t, wrg, gq, gk, cos_t, sin_t):
    b, s, d = x.shape
    t = s // N_SEG
    tt = IN_SEGS * t
    const = lambda *_: (0, 0)
    return pl.pallas_call(
        _in_proj_kernel,
        grid=(b, N_SEG // IN_SEGS),
        in_specs=[
            pl.BlockSpec((1, tt, d), lambda i, j: (i, j, 0)),
            pl.BlockSpec((1, d), const),
            pl.BlockSpec(wqkv_t.shape, const),
            pl.BlockSpec(wrg.shape, const),
            pl.BlockSpec((1, HEAD_DIM), const),
            pl.BlockSpec((1, HEAD_DIM), const),
            pl.BlockSpec((HEAD_DIM, tt), lambda i, j: (0, j)),
            pl.BlockSpec((HEAD_DIM, tt), lambda i, j: (0, j)),
        ],
        out_specs=[
            pl.BlockSpec((1, D_ATTN, tt), lambda i, j: (i, 0, j)),
            pl.BlockSpec((1, tt, D_KV), lambda i, j: (i, j, 0)),
            pl.BlockSpec((1, D_KV, tt), lambda i, j: (i, 0, j)),
            pl.BlockSpec((1, t, IN_SEGS * D_LRU), lambda i, j: (i, 0, j)),
            pl.BlockSpec((1, t, IN_SEGS * D_LRU), lambda i, j: (i, 0, j)),
            pl.BlockSpec((1, N_KV_HEADS, tt), lambda i, j: (i, 0, j)),
        ],
        out_shape=[
            jax.ShapeDtypeStruct((b, D_ATTN, s), BF16),
            jax.ShapeDtypeStruct((b, s, D_KV), BF16),
            jax.ShapeDtypeStruct((b, D_KV, s), BF16),
            jax.ShapeDtypeStruct((b, t, N_SEG * D_LRU), F32),
            jax.ShapeDtypeStruct((b, t, N_SEG * D_LRU), F32),
            jax.ShapeDtypeStruct((b, N_KV_HEADS, s), F32),
        ],
        compiler_params=pltpu.CompilerParams(
            dimension_semantics=("arbitrary", "arbitrary"), vmem_limit_bytes=VMEM_LIMIT),
        name="in_proj",
    )(x, gmix, wqkv_t, wrg, gq, gk, cos_t, sin_t)


def _attn_kernel(qt_ref, k_ref, vt_ref, kn2_ref, grow_ref, *refs):
    n_w = (len(refs) - 3) // 2
    w_f32_refs, y_ref, w_bf16_refs = refs[:n_w], refs[n_w], refs[n_w + 1:2 * n_w + 1]
    vta_ref, kn_ref = refs[2 * n_w + 1:]
    s = k_ref.shape[1]
    tq = qt_ref.shape[2]

    for w_in_ref, w_out_ref in zip(w_f32_refs, w_bf16_refs):
        w_out_ref[...] = w_in_ref[0].astype(BF16)

    @pl.when(pl.program_id(1) == 0)
    def _():
        vt = vt_ref[0]
        row = lax.broadcasted_iota(jnp.int32, (PV_ROWS - HEAD_DIM, s), 0)
        ones_pad = jnp.where(row == 0, 1.0, 0.0).astype(BF16)
        kn_max = jnp.sqrt(jnp.max(kn2_ref[0], axis=1, keepdims=True))
        for kv in range(N_KV_HEADS):
            vta_ref[kv] = jnp.concatenate([vt[kv * HEAD_DIM:(kv + 1) * HEAD_DIM], ones_pad], axis=0)
            kn_ref[kv] = jnp.broadcast_to(kn_max[kv:kv + 1], (1, LANES))

    qt = qt_ref[0]
    zeros = jnp.zeros((HEAD_DIM, tq), BF16)

    def q_operand(kv, heads=range(Q_PER_KV)):
        cols = []
        for j in heads:
            h = kv * Q_PER_KV + j
            qh = qt[h * HEAD_DIM:(h + 1) * HEAD_DIM]
            cols.append(jnp.concatenate([qh, zeros] if kv == 0 else [zeros, qh], axis=0))
        return jnp.concatenate(cols, axis=1)

    def finish(accs):
        outs = []
        for acc in accs:
            o = acc[:HEAD_DIM] / acc[HEAD_DIM:HEAD_DIM + 1]
            for j in range(acc.shape[1] // tq):
                outs.append(o[:, j * tq:(j + 1) * tq])
        ot = jnp.concatenate(outs, axis=0)
        y_ref[0] = ((ot * _rms(ot, 0)).T * grow_ref[...]).astype(BF16)

    def scores_fn(qst, chunk):
        return lambda c: _dot(k_ref[0, c * chunk:(c + 1) * chunk, :], qst)

    qf = qt.astype(F32).reshape(N_Q_HEADS, HEAD_DIM, tq)
    qn = jnp.sqrt(jnp.sum(qf * qf, axis=1))
    groups = [range(j0, j0 + FAST_HEADS) for j0 in range(0, Q_PER_KV, FAST_HEADS)]
    n_chunks = s // FAST_CHUNK
    ms, scores = [], []
    for kv in range(N_KV_HEADS):
        bound = qn[kv * Q_PER_KV:(kv + 1) * Q_PER_KV] * (kn_ref[kv][:, :1] * BOUND_SLACK)
        ms.append([jnp.concatenate([bound[j:j + 1] for j in g], axis=1) for g in groups])
        scores.append([scores_fn(q_operand(kv, g), FAST_CHUNK) for g in groups])
    steps = [(kv, c) for kv in range(N_KV_HEADS) for c in range(n_chunks)]
    acc = [[None] * len(groups) for _ in range(N_KV_HEADS)]
    sc_next = [sf(0) for sf in scores[0]]
    for i, (kv, c) in enumerate(steps):
        for gi in range(len(groups)):
            sc = sc_next[gi]
            if i + 1 < len(steps):
                kv_n, c_n = steps[i + 1]
                sc_next[gi] = scores[kv_n][gi](c_n)
            p = jnp.exp2(sc - ms[kv][gi]).astype(BF16)
            pv = _dot(vta_ref[kv, :, c * FAST_CHUNK:(c + 1) * FAST_CHUNK], p)
            acc[kv][gi] = pv if acc[kv][gi] is None else acc[kv][gi] + pv
    accs = acc[0] + acc[1]
    finish(accs)
    den = accs[0][HEAD_DIM:HEAD_DIM + 1]
    for acc in accs[1:]:
        den = jnp.minimum(den, acc[HEAD_DIM:HEAD_DIM + 1])
    lmin = jnp.min(den)

    @pl.when(jnp.logical_not(lmin > L_MIN_OK))
    def _():
        accs = []
        for kv in range(N_KV_HEADS):
            scores = scores_fn(q_operand(kv), KEY_CHUNK)
            n_chunks = s // KEY_CHUNK
            m = None
            acc = None
            sc_next = scores(0)
            for c in range(n_chunks):
                sc = sc_next
                if c + 1 < n_chunks:
                    sc_next = scores(c + 1)
                mc = jnp.max(sc, axis=0, keepdims=True)
                m_new = mc if m is None else jnp.maximum(m, mc)
                p = jnp.exp2(sc - m_new).astype(BF16)
                pv = _dot(vta_ref[kv, :, c * KEY_CHUNK:(c + 1) * KEY_CHUNK], p)
                acc = pv if acc is None else acc * jnp.exp2(m - m_new) + pv
                m = m_new
            accs.append(acc)
        finish(accs)


def _attention(qt, k, vt, kn2, gcol, weights):
    b, _, s = qt.shape
    nq = s // Q_TILE
    n_steps = b * nq
    w_in_specs, w_out_specs, w_out_shapes = [], [], []
    for w in weights:
        _, rows, cols = w.shape
        step_rows = max(rows // n_steps, BF16_SUBLANES)
        last = rows // step_rows - 1
        w_in_specs.append(pl.BlockSpec(
            (1, step_rows, cols), lambda i, j, last=last: (0, jnp.minimum(i * nq + j, last), 0)))
        w_out_specs.append(pl.BlockSpec(
            (step_rows, cols), lambda i, j, last=last: (jnp.minimum(i * nq + j, last), 0)))
        w_out_shapes.append(jax.ShapeDtypeStruct((rows, cols), BF16))
    outs = pl.pallas_call(
        _attn_kernel,
        grid=(b, nq),
        in_specs=[
            pl.BlockSpec((1, D_ATTN, Q_TILE), lambda i, j: (i, 0, j)),
            pl.BlockSpec((1, s, D_KV), lambda i, j: (i, 0, 0)),
            pl.BlockSpec((1, D_KV, s), lambda i, j: (i, 0, 0)),
            pl.BlockSpec((1, N_KV_HEADS, s), lambda i, j: (i, 0, 0)),
            pl.BlockSpec((1, D_ATTN), lambda i, j: (0, 0)),
        ] + w_in_specs,
        out_specs=[pl.BlockSpec((1, Q_TILE, D_ATTN), lambda i, j: (i, j, 0))] + w_out_specs,
        out_shape=[jax.ShapeDtypeStruct((b, s, D_ATTN), BF16)] + w_out_shapes,
        scratch_shapes=[pltpu.VMEM((N_KV_HEADS, PV_ROWS, s), BF16),
                        pltpu.VMEM((N_KV_HEADS, 1, LANES), F32)],
        compiler_params=pltpu.CompilerParams(
            dimension_semantics=("arbitrary", "arbitrary"), vmem_limit_bytes=VMEM_LIMIT),
        name="attention",
    )(qt, k, vt, kn2, gcol, *weights)
    return outs[0], outs[1:]


def _shift_seg_down(x):
    row = lax.broadcasted_iota(jnp.int32, x.shape, 0)
    return jnp.where(row == 0, 0.0, pltpu.roll(x, 1, 0))


def _shift_seg_up(x):
    row = lax.broadcasted_iota(jnp.int32, x.shape, 0)
    return jnp.where(row == SUBLANES - 1, 0.0, pltpu.roll(x, SUBLANES - 1, 0))


def _sigmoid(x):
    return 0.5 * jnp.tanh(0.5 * x) + 0.5


def _gelu_tanh(x):
    k = 0.7978845608028654
    hx = 0.5 * x
    t = jnp.tanh(x * (k + (k * 0.044715) * (x * x)))
    return hx * t + hx


def _lru_kernel(xr_ref, cw_ref, cb_ref, wa_ref, wx_ref, ba_ref, bx_ref, lam_ref, y_ref,
                xh_ref, h0_ref, p0_ref, h1_ref, p1_ref, ystage_ref):
    t_len = xr_ref.shape[1]
    lanes = xr_ref.shape[3]
    n_chunks = t_len // GATE_CHUNK

    cw = cw_ref[0] * 0.5
    cb = cb_ref[...] * 0.5

    halo_lo = [_shift_seg_down(xr_ref[0, t_len - 2])[None], _shift_seg_down(xr_ref[0, t_len - 1])[None]]
    halo_hi = [_shift_seg_up(xr_ref[0, 0])[None]]

    def shifted(t0, n, k):
        lo = t0 + k - CONV_PAD_LEFT
        pieces = halo_lo[lo + CONV_PAD_LEFT:] if lo < 0 else []
        pieces = pieces + [xr_ref[0, max(lo, 0):min(lo + n, t_len)]]
        if lo + n > t_len:
            pieces = pieces + halo_hi[:lo + n - t_len]
        return pieces[0] if len(pieces) == 1 else jnp.concatenate(pieces, axis=0)

    for t0 in range(0, t_len, GATE_CHUNK):
        xh = cb[None]
        for kk in range(CONV_W):
            xh = xh + shifted(t0, GATE_CHUNK, kk) * cw[kk:kk + 1][None]
        xh_ref[t0:t0 + GATE_CHUNK] = xh
    lam = lam_ref[0]
    hc = (0.5 * LRU_C * 1.4426950408889634) * (
        jnp.minimum(lam, 0.0) - jnp.log1p(jnp.exp(-jnp.abs(lam))))

    zblk = jnp.zeros((LRU_BLOCK_W, LRU_BLOCK_W), F32)
    brow = lax.broadcasted_iota(jnp.int32, (lanes, 2 * lanes), 0)
    wg = []
    for e in range(2):
        rows = []
        for i in range(lanes // LRU_BLOCK_W):
            blocks = []
            for w_ref in (wa_ref, wx_ref):
                blocks += [w_ref[0, e, i] if c == i else zblk for c in range(lanes // LRU_BLOCK_W)]
            rows.append(jnp.concatenate(blocks, axis=1))
        b_half = 0.5 * jnp.concatenate([ba_ref[0, e:e + 1], bx_ref[0, e:e + 1]], axis=1)
        b_hi = b_half.astype(BF16).astype(F32)
        b_lo = b_half - b_hi
        bias_rows = jnp.where(brow == 0, b_hi, jnp.where(brow == 1, b_lo, 0.0))
        wg.append(jnp.concatenate([jnp.concatenate(rows, axis=0), bias_rows], axis=0).astype(BF16))
    ones_cols = jnp.where(
        lax.broadcasted_iota(jnp.int32, (SCAN_PART * SUBLANES, lanes), 1) < 2, 1.0, 0.0).astype(BF16)

    def gates(t0, e):
        xh = xh_ref[t0:t0 + SCAN_PART].reshape(SCAN_PART * SUBLANES, lanes)
        th = jnp.tanh(_dot(jnp.concatenate([xh.astype(BF16), ones_cols], axis=1), wg[e]))
        tr = th[:, :lanes]
        ti = th[:, lanes:]
        a = jnp.exp2(tr * hc[e:e + 1] + hc[e:e + 1])
        y = 1.0 - a * a
        mult = y * lax.rsqrt(jnp.maximum(y, 1e-30))
        u = (ti * xh + xh) * mult
        return a, u

    zero = jnp.zeros((SUBLANES, lanes), F32)
    one = jnp.ones((SUBLANES, lanes), F32)
    e0, q0, e1, q1 = zero, one, zero, one
    for part in range(t_len // SCAN_PART):
        t0 = part * SCAN_PART
        a, u = gates(t0, 0)
        for t in range(SCAN_PART):
            rows = slice(t * SUBLANES, (t + 1) * SUBLANES)
            e0 = a[rows] * e0 + u[rows]
            q0 = a[rows] * q0
            h0_ref[t0 + t] = e0
            p0_ref[t0 + t] = q0
        t1 = t_len - (part + 1) * SCAN_PART
        a, u = gates(t1, 1)
        for t in reversed(range(SCAN_PART)):
            rows = slice(t * SUBLANES, (t + 1) * SUBLANES)
            e1 = a[rows] * e1 + u[rows]
            q1 = a[rows] * q1
            h1_ref[t1 + t] = e1
            p1_ref[t1 + t] = q1

    c0 = zero
    c1 = zero
    for _ in range(N_SEG - 1):
        c0 = _shift_seg_down(e0 + q0 * c0)
        c1 = _shift_seg_up(e1 + q1 * c1)

    def out_body(i, carry):
        t0 = pl.multiple_of(i * GATE_CHUNK, GATE_CHUNK)
        sl = pl.ds(t0, GATE_CHUNK)
        h = (h0_ref[sl] + p0_ref[sl] * c0[None]) + (h1_ref[sl] + p1_ref[sl] * c1[None])
        ystage_ref[...] = h.reshape(GATE_CHUNK * SUBLANES, lanes)
        for seg in range(N_SEG):
            y_ref[0, pl.ds(seg * t_len + t0, GATE_CHUNK)] = (
                ystage_ref[pl.ds(seg, GATE_CHUNK, stride=SUBLANES)])
        return carry

    lax.fori_loop(0, n_chunks, out_body, 0)


def _lru(xr_il, conv_w, conv_b, wa, wx, ba, bx, lam):
    b, t_len, n_seg, d = xr_il.shape
    n_groups = d // LANES
    blocks_per_group = LANES // LRU_BLOCK_W
    slab = pltpu.VMEM((t_len, n_seg, LANES), F32)
    w_spec = pl.BlockSpec((1, 2, blocks_per_group, LRU_BLOCK_W, LRU_BLOCK_W),
                          lambda i, j: (0, 0, j, 0, 0))
    lane_group = pl.BlockSpec((1, 2, LANES), lambda i, j: (0, 0, j))
    return pl.pallas_call(
        _lru_kernel,
        grid=(b, n_groups),
        in_specs=[
            pl.BlockSpec((1, t_len, n_seg, LANES), lambda i, j: (i, 0, 0, j)),
            pl.BlockSpec((1, CONV_W, LANES), lambda i, j: (0, 0, j)),
            pl.BlockSpec((1, LANES), lambda i, j: (0, j)),
            w_spec, w_spec, lane_group, lane_group, lane_group,
        ],
        out_specs=pl.BlockSpec((1, t_len * n_seg, LANES), lambda i, j: (i, 0, j)),
        out_shape=jax.ShapeDtypeStruct((b, t_len * n_seg, d), F32),
        scratch_shapes=[slab] * 5 + [
            pltpu.VMEM((GATE_CHUNK * n_seg, LANES), F32)],
        compiler_params=pltpu.CompilerParams(
            dimension_semantics=("arbitrary", "arbitrary"), vmem_limit_bytes=VMEM_LIMIT),
        name="rglru",
    )(xr_il, conv_w, conv_b, wa, wx, ba, bx, lam)


def _tail_kernel(x_ref, ya_ref, hl_ref, xg_ref, p_ref, glru_ref, gmlp_ref, gple_ref, gfin_ref,
                 wout_ref, wup_ref, wdown_ref, wgate_ref, wproj_ref, o_ref):
    x = x_ref[0]
    yl = hl_ref[0] * _gelu_tanh(xg_ref[0])
    yl_n = (yl * _rms(yl, -1) * glru_ref[...]).astype(BF16)
    h = x + _dot(ya_ref[0], wout_ref[:D_ATTN]) + _dot(yl_n, wout_ref[D_ATTN:])
    hn = (h * _rms(h, -1) * gmlp_ref[...]).astype(BF16)
    m = _dot(hn, wup_ref[...])
    act = jnp.square(jnp.maximum(m, 0.0)).astype(BF16)
    h = h + _dot(act, wdown_ref[...])
    hn = (h * _rms(h, -1) * gple_ref[...]).astype(BF16)
    gate = _sigmoid(_dot(hn, wgate_ref[...]))
    h = h + gate * _dot(p_ref[0, 0].astype(BF16), wproj_ref[...])
    o_ref[0] = h * _rms(h, -1) * gfin_ref[...]


def _tail(x, ya, hl, xg_il, p, glru, gmlp, gple, gfin, wout, wup, wdown, wgate, wproj):
    b, s, d = x.shape
    t = s // N_SEG
    const = lambda *_: (0, 0)
    weights = (wout, wup, wdown, wgate, wproj)

    def resident(arr):
        return pl.BlockSpec(arr.shape, const, pipeline_mode=pl.Buffered(1))

    return pl.pallas_call(
        _tail_kernel,
        grid=(b, N_SEG),
        in_specs=[
            pl.BlockSpec((1, t, d), lambda i, j: (i, j, 0)),
            pl.BlockSpec((1, t, D_ATTN), lambda i, j: (i, j, 0)),
            pl.BlockSpec((1, t, D_LRU), lambda i, j: (i, j, 0)),
            pl.BlockSpec((1, t, D_LRU), lambda i, j: (i, 0, j)),
            pl.BlockSpec((1, 1, t, D_PLE), lambda i, j: (0, i, j, 0)),
            resident(glru), resident(gmlp), resident(gple), resident(gfin),
        ] + [resident(w) for w in weights],
        out_specs=pl.BlockSpec((1, t, d), lambda i, j: (i, j, 0)),
        out_shape=jax.ShapeDtypeStruct((b, s, d), F32),
        compiler_params=pltpu.CompilerParams(
            dimension_semantics=("arbitrary", "arbitrary"), vmem_limit_bytes=VMEM_LIMIT),
        name="tail",
    )(x, ya, hl, xg_il, p, glru, gmlp, gple, gfin, *weights)


def _rope_tables_t(seq_len):
    pos = np.arange(seq_len)
    row = (pos // GRID_W).astype(np.float32)
    col = (pos % GRID_W).astype(np.float32)
    inv_freq = np.float32(ROPE_THETA) ** (-np.arange(N_FREQ, dtype=np.float32) / np.float32(N_FREQ))
    ang_r = (row[None, :] * inv_freq[:, None]).astype(np.float32)
    ang_c = (col[None, :] * inv_freq[:, None]).astype(np.float32)
    cos_t = np.concatenate([np.cos(ang_r)] * 2 + [np.cos(ang_c)] * 2, axis=0)
    sin_t = np.concatenate([-np.sin(ang_r), np.sin(ang_r), -np.sin(ang_c), np.sin(ang_c)], axis=0)
    return jnp.asarray(cos_t, F32), jnp.asarray(sin_t, F32)


def kernel(x, p, mix_norm, w_in, q_norm, k_norm, conv_w, conv_b, lru_wa, lru_ba, lru_wx, lru_bx,
           lru_lambda, attn_out_norm, lru_out_norm, w_out, mlp_norm, w_up, w_down, ple_norm,
           w_ple_gate, w_ple_proj, final_norm):
    b, s, d = x.shape
    assert w_in.shape[0] == 1, "single-layer trunk: the final norm is fused into the layer tail"
    cos_t, sin_t = _rope_tables_t(s)
    wqkv_t = w_in[0, :, :D_ATTN + 2 * D_KV].T.astype(BF16)
    wrg = w_in[0, :, D_ATTN + 2 * D_KV:].astype(BF16)
    qt, k, vt, xr, xg, kn2 = _in_proj(x, mix_norm, wqkv_t, wrg, q_norm, k_norm, cos_t, sin_t)
    ya, tail_weights = _attention(qt, k, vt, kn2, attn_out_norm,
                                  (w_out, w_up, w_down, w_ple_gate, w_ple_proj))
    hl = _lru(xr.reshape(b, s // N_SEG, N_SEG, D_LRU), conv_w, conv_b, lru_wa, lru_wx, lru_ba, lru_bx,
              lru_lambda)
    return _tail(x, ya, hl, xg, p, lru_out_norm, mlp_norm, ple_norm, final_norm.reshape(1, -1),
                 *tail_weights)
```

```python
import jax
import jax.numpy as jnp
import numpy as np
from jax import lax
from jax.experimental import pallas as pl
from jax.experimental.pallas import tpu as pltpu

D_MODEL = 1024
GRID_W = 64
HEAD_DIM = 64
D_ATTN = 512
N_Q_HEADS = 8
N_KV_HEADS = 2
Q_PER_KV = 4
D_KV = 128
ROPE_THETA = 10000.0
N_FREQ = 16
D_LRU = 512
LRU_BLOCK_W = 64
LRU_C = 8.0
CONV_W = 4
CONV_PAD_LEFT = 2
D_FF = 4096
D_PLE = 256
NORM_EPS = 1e-6

SUBLANES = 8
BF16_SUBLANES = 16
LANES = 128

N_SEG = SUBLANES
Q_TILE = 256
KEY_CHUNK = 256
FAST_CHUNK = 256
FAST_HEADS = 1
PV_ROWS = 80
BOUND_SLACK = 1.001
L_MIN_OK = 2.0 ** -80
GATE_CHUNK = 64
SCAN_PART = 32
IN_SEGS = 2
IN_SPLIT = 2
VMEM_LIMIT = 56 * 1024 * 1024
Q_SCALE = HEAD_DIM ** -0.5 * 1.4426950408889634

F32 = jnp.float32
BF16 = jnp.bfloat16


def _rms(x, axis):
    return lax.rsqrt(jnp.mean(x * x, axis=axis, keepdims=True) + NORM_EPS)


def _dot(a, b):
    return jnp.dot(a, b, preferred_element_type=F32)


def _as_column(row):
    n = row.shape[1]
    ri = lax.broadcasted_iota(jnp.int32, (n, n), 0)
    ci = lax.broadcasted_iota(jnp.int32, (n, n), 1)
    return jnp.sum(jnp.where(ri == ci, row, 0.0), axis=1, keepdims=True)


def _dot_nt(a, b):
    return lax.dot_general(a, b, (((1,), (1,)), ((), ())), preferred_element_type=F32)


def _norm_rope_t(xt, gcol, cos_t, sin_t, n_heads):
    t = xt.shape[1]
    x3 = xt.reshape(n_heads, HEAD_DIM, t)
    xn = x3 * _rms(x3, 1) * gcol[None]
    x5 = xn.reshape(n_heads * 2, 2, N_FREQ, t)
    xs = jnp.concatenate([x5[:, 1:2], x5[:, 0:1]], axis=1).reshape(n_heads, HEAD_DIM, t)
    out = xn * cos_t[None] + xs * sin_t[None]
    return out.reshape(n_heads * HEAD_DIM, t)


def _in_proj_kernel(x_ref, gmix_ref, win_ref, gq_ref, gk_ref, cos_ref, sin_ref,
                    qt_ref, k_ref, vt_ref, xr_ref, xg_ref, kn2_ref, wqkv_t_ref, wrg_ref):
    @pl.when((pl.program_id(0) == 0) & (pl.program_id(1) == 0))
    def _():
        n_qkv = wqkv_t_ref.shape[0]
        wqkv_t_ref[...] = win_ref[0, :, :n_qkv].T.astype(BF16)
        wrg_ref[...] = win_ref[0, :, n_qkv:].astype(BF16)

    gq = _as_column(gq_ref[...]) * Q_SCALE
    gk = _as_column(gk_ref[...])
    t_len = xr_ref.shape[1]
    rows = t_len // IN_SPLIT
    groups = [slice(i * rows, (i + 1) * rows) for i in range(x_ref.shape[1] // rows)]
    hns = []
    for r in groups:
        x = x_ref[0, r]
        hns.append((x * _rms(x, -1) * gmix_ref[...]).astype(BF16))
    lru_proj = lambda hn: _dot(hn, wrg_ref[...])
    qkv_proj = lambda hn: _dot_nt(wqkv_t_ref[...], hn)
    zrs = [lru_proj(hn) for hn in hns[:-1]]
    zts = [qkv_proj(hn) for hn in hns]
    zrs.append(lru_proj(hns[-1]))
    for r, zt, zr in zip(groups, zts, zrs):
        cos_t = cos_ref[:, r]
        sin_t = sin_ref[:, r]
        qt = _norm_rope_t(zt[:D_ATTN], gq, cos_t, sin_t, N_Q_HEADS)
        kt = _norm_rope_t(zt[D_ATTN:D_ATTN + D_KV], gk, cos_t, sin_t, N_KV_HEADS)
        qt_ref[0, :, r] = qt.astype(BF16)
        k_ref[0, r] = kt.T.astype(BF16)
        kf = kt.astype(BF16).astype(F32).reshape(N_KV_HEADS, HEAD_DIM, kt.shape[1])
        kn2_ref[0, :, r] = jnp.sum(kf * kf, axis=1)
        vt_ref[0, :, r] = zt[D_ATTN + D_KV:].astype(BF16)
        seg, t0 = divmod(r.start, t_len)
        xr_ref[0, t0:t0 + rows, seg * D_LRU:(seg + 1) * D_LRU] = zr[:, :D_LRU]
        xg_ref[0, t0:t0 + rows, seg * D_LRU:(seg + 1) * D_LRU] = zr[:, D_LRU:]


def _in_proj(x, gmix, w_in, gq, gk, cos_t, sin_t):
    b, s, d = x.shape
    t = s // N_SEG
    tt = IN_SEGS * t
    const = lambda *_: (0, 0)
    return pl.pallas_call(
        _in_proj_kernel,
        grid=(b, N_SEG // IN_SEGS),
        in_specs=[
            pl.BlockSpec((1, tt, d), lambda i, j: (i, j, 0)),
            pl.BlockSpec((1, d), const),
            pl.BlockSpec(w_in.shape, lambda *_: (0, 0, 0), pipeline_mode=pl.Buffered(1)),
            pl.BlockSpec((1, HEAD_DIM), const),
            pl.BlockSpec((1, HEAD_DIM), const),
            pl.BlockSpec((HEAD_DIM, tt), lambda i, j: (0, j)),
            pl.BlockSpec((HEAD_DIM, tt), lambda i, j: (0, j)),
        ],
        out_specs=[
            pl.BlockSpec((1, D_ATTN, tt), lambda i, j: (i, 0, j)),
            pl.BlockSpec((1, tt, D_KV), lambda i, j: (i, j, 0)),
            pl.BlockSpec((1, D_KV, tt), lambda i, j: (i, 0, j)),
            pl.BlockSpec((1, t, IN_SEGS * D_LRU), lambda i, j: (i, 0, j)),
            pl.BlockSpec((1, t, IN_SEGS * D_LRU), lambda i, j: (i, 0, j)),
            pl.BlockSpec((1, N_KV_HEADS, tt), lambda i, j: (i, 0, j)),
        ],
        out_shape=[
            jax.ShapeDtypeStruct((b, D_ATTN, s), BF16),
            jax.ShapeDtypeStruct((b, s, D_KV), BF16),
            jax.ShapeDtypeStruct((b, D_KV, s), BF16),
            jax.ShapeDtypeStruct((b, t, N_SEG * D_LRU), F32),
            jax.ShapeDtypeStruct((b, t, N_SEG * D_LRU), F32),
            jax.ShapeDtypeStruct((b, N_KV_HEADS, s), F32),
        ],
        scratch_shapes=[pltpu.VMEM((D_ATTN + 2 * D_KV, d), BF16),
                        pltpu.VMEM((d, 2 * D_LRU), BF16)],
        compiler_params=pltpu.CompilerParams(
            dimension_semantics=("arbitrary", "arbitrary"), vmem_limit_bytes=VMEM_LIMIT),
        name="in_proj",
    )(x, gmix, w_in, gq, gk, cos_t, sin_t)


def _attn_kernel(qt_ref, k_ref, vt_ref, kn2_ref, grow_ref, *refs):
    n_w = (len(refs) - 3) // 2
    w_f32_refs, y_ref, w_bf16_refs = refs[:n_w], refs[n_w], refs[n_w + 1:2 * n_w + 1]
    vta_ref, kn_ref = refs[2 * n_w + 1:]
    s = k_ref.shape[1]
    tq = qt_ref.shape[2]

    for w_in_ref, w_out_ref in zip(w_f32_refs, w_bf16_refs):
        w_out_ref[...] = w_in_ref[0].astype(BF16)

    @pl.when(pl.program_id(1) == 0)
    def _():
        vt = vt_ref[0]
        row = lax.broadcasted_iota(jnp.int32, (PV_ROWS - HEAD_DIM, s), 0)
        ones_pad = jnp.where(row == 0, 1.0, 0.0).astype(BF16)
        kn_max = jnp.sqrt(jnp.max(kn2_ref[0], axis=1, keepdims=True))
        for kv in range(N_KV_HEADS):
            vta_ref[kv] = jnp.concatenate([vt[kv * HEAD_DIM:(kv + 1) * HEAD_DIM], ones_pad], axis=0)
            kn_ref[kv] = jnp.broadcast_to(kn_max[kv:kv + 1], (1, LANES))

    qt = qt_ref[0]
    zeros = jnp.zeros((HEAD_DIM, tq), BF16)

    def q_operand(kv, heads=range(Q_PER_KV)):
        cols = []
        for j in heads:
            h = kv * Q_PER_KV + j
            qh = qt[h * HEAD_DIM:(h + 1) * HEAD_DIM]
            cols.append(jnp.concatenate([qh, zeros] if kv == 0 else [zeros, qh], axis=0))
        return jnp.concatenate(cols, axis=1)

    def finish(accs):
        outs = []
        for acc in accs:
            o = acc[:HEAD_DIM] / acc[HEAD_DIM:HEAD_DIM + 1]
            for j in range(acc.shape[1] // tq):
                outs.append(o[:, j * tq:(j + 1) * tq])
        ot = jnp.concatenate(outs, axis=0)
        y_ref[0] = ((ot * _rms(ot, 0)).T * grow_ref[...]).astype(BF16)

    def scores_fn(qst, chunk):
        return lambda c: _dot(k_ref[0, c * chunk:(c + 1) * chunk, :], qst)

    qf = qt.astype(F32).reshape(N_Q_HEADS, HEAD_DIM, tq)
    qn = jnp.sqrt(jnp.sum(qf * qf, axis=1))
    groups = [range(j0, j0 + FAST_HEADS) for j0 in range(0, Q_PER_KV, FAST_HEADS)]
    n_chunks = s // FAST_CHUNK
    ms, scores = [], []
    for kv in range(N_KV_HEADS):
        bound = qn[kv * Q_PER_KV:(kv + 1) * Q_PER_KV] * (kn_ref[kv][:, :1] * BOUND_SLACK)
        ms.append([jnp.concatenate([bound[j:j + 1] for j in g], axis=1) for g in groups])
        scores.append([scores_fn(q_operand(kv, g), FAST_CHUNK) for g in groups])
    steps = [(kv, c) for kv in range(N_KV_HEADS) for c in range(n_chunks)]
    acc = [[None] * len(groups) for _ in range(N_KV_HEADS)]
    sc_next = [sf(0) for sf in scores[0]]
    for i, (kv, c) in enumerate(steps):
        for gi in range(len(groups)):
            sc = sc_next[gi]
            if i + 1 < len(steps):
                kv_n, c_n = steps[i + 1]
                sc_next[gi] = scores[kv_n][gi](c_n)
            p = jnp.exp2(sc - ms[kv][gi]).astype(BF16)
            pv = _dot(vta_ref[kv, :, c * FAST_CHUNK:(c + 1) * FAST_CHUNK], p)
            acc[kv][gi] = pv if acc[kv][gi] is None else acc[kv][gi] + pv
    accs = acc[0] + acc[1]
    finish(accs)
    den = accs[0][HEAD_DIM:HEAD_DIM + 1]
    for acc in accs[1:]:
        den = jnp.minimum(den, acc[HEAD_DIM:HEAD_DIM + 1])
    lmin = jnp.min(den)

    @pl.when(jnp.logical_not(lmin > L_MIN_OK))
    def _():
        accs = []
        for kv in range(N_KV_HEADS):
            scores = scores_fn(q_operand(kv), KEY_CHUNK)
            n_chunks = s // KEY_CHUNK
            m = None
            acc = None
            sc_next = scores(0)
            for c in range(n_chunks):
                sc = sc_next
                if c + 1 < n_chunks:
                    sc_next = scores(c + 1)
                mc = jnp.max(sc, axis=0, keepdims=True)
                m_new = mc if m is None else jnp.maximum(m, mc)
                p = jnp.exp2(sc - m_new).astype(BF16)
                pv = _dot(vta_ref[kv, :, c * KEY_CHUNK:(c + 1) * KEY_CHUNK], p)
                acc = pv if acc is None else acc * jnp.exp2(m - m_new) + pv
                m = m_new
            accs.append(acc)
        finish(accs)


def _attention(qt, k, vt, kn2, gcol, weights):
    b, _, s = qt.shape
    nq = s // Q_TILE
    n_steps = b * nq
    w_in_specs, w_out_specs, w_out_shapes = [], [], []
    for w in weights:
        _, rows, cols = w.shape
        step_rows = max(rows // n_steps, BF16_SUBLANES)
        last = rows // step_rows - 1
        w_in_specs.append(pl.BlockSpec(
            (1, step_rows, cols), lambda i, j, last=last: (0, jnp.minimum(i * nq + j, last), 0)))
        w_out_specs.append(pl.BlockSpec(
            (step_rows, cols), lambda i, j, last=last: (jnp.minimum(i * nq + j, last), 0)))
        w_out_shapes.append(jax.ShapeDtypeStruct((rows, cols), BF16))
    outs = pl.pallas_call(
        _attn_kernel,
        grid=(b, nq),
        in_specs=[
            pl.BlockSpec((1, D_ATTN, Q_TILE), lambda i, j: (i, 0, j)),
            pl.BlockSpec((1, s, D_KV), lambda i, j: (i, 0, 0)),
            pl.BlockSpec((1, D_KV, s), lambda i, j: (i, 0, 0)),
            pl.BlockSpec((1, N_KV_HEADS, s), lambda i, j: (i, 0, 0)),
            pl.BlockSpec((1, D_ATTN), lambda i, j: (0, 0)),
        ] + w_in_specs,
        out_specs=[pl.BlockSpec((1, Q_TILE, D_ATTN), lambda i, j: (i, j, 0))] + w_out_specs,
        out_shape=[jax.ShapeDtypeStruct((b, s, D_ATTN), BF16)] + w_out_shapes,
        scratch_shapes=[pltpu.VMEM((N_KV_HEADS, PV_ROWS, s), BF16),
                        pltpu.VMEM((N_KV_HEADS, 1, LANES), F32)],
        compiler_params=pltpu.CompilerParams(
            dimension_semantics=("arbitrary", "arbitrary"), vmem_limit_bytes=VMEM_LIMIT),
        name="attention",
    )(qt, k, vt, kn2, gcol, *weights)
    return outs[0], outs[1:]


def _shift_seg_down(x):
    row = lax.broadcasted_iota(jnp.int32, x.shape, 0)
    return jnp.where(row == 0, 0.0, pltpu.roll(x, 1, 0))


def _shift_seg_up(x):
    row = lax.broadcasted_iota(jnp.int32, x.shape, 0)
    return jnp.where(row == SUBLANES - 1, 0.0, pltpu.roll(x, SUBLANES - 1, 0))


def _sigmoid(x):
    return 0.5 * jnp.tanh(0.5 * x) + 0.5


def _gelu_tanh(x):
    k = 0.7978845608028654
    hx = 0.5 * x
    t = jnp.tanh(x * (k + (k * 0.044715) * (x * x)))
    return hx * t + hx


def _lru_kernel(xr_ref, cw_ref, cb_ref, wa_ref, wx_ref, ba_ref, bx_ref, lam_ref, y_ref,
                xh_ref, h0_ref, p0_ref, h1_ref, p1_ref, ystage_ref):
    t_len = xr_ref.shape[1]
    lanes = xr_ref.shape[3]
    n_chunks = t_len // GATE_CHUNK

    cw = cw_ref[0] * 0.5
    cb = cb_ref[...] * 0.5

    halo_lo = [_shift_seg_down(xr_ref[0, t_len - 2])[None], _shift_seg_down(xr_ref[0, t_len - 1])[None]]
    halo_hi = [_shift_seg_up(xr_ref[0, 0])[None]]

    def shifted(t0, n, k):
        lo = t0 + k - CONV_PAD_LEFT
        pieces = halo_lo[lo + CONV_PAD_LEFT:] if lo < 0 else []
        pieces = pieces + [xr_ref[0, max(lo, 0):min(lo + n, t_len)]]
        if lo + n > t_len:
            pieces = pieces + halo_hi[:lo + n - t_len]
        return pieces[0] if len(pieces) == 1 else jnp.concatenate(pieces, axis=0)

    for t0 in range(0, t_len, GATE_CHUNK):
        xh = cb[None]
        for kk in range(CONV_W):
            xh = xh + shifted(t0, GATE_CHUNK, kk) * cw[kk:kk + 1][None]
        xh_ref[t0:t0 + GATE_CHUNK] = xh
    lam = lam_ref[0]
    hc = (0.5 * LRU_C * 1.4426950408889634) * (
        jnp.minimum(lam, 0.0) - jnp.log1p(jnp.exp(-jnp.abs(lam))))

    zblk = jnp.zeros((LRU_BLOCK_W, LRU_BLOCK_W), F32)
    brow = lax.broadcasted_iota(jnp.int32, (lanes, 2 * lanes), 0)
    wg = []
    for e in range(2):
        rows = []
        for i in range(lanes // LRU_BLOCK_W):
            blocks = []
            for w_ref in (wa_ref, wx_ref):
                blocks += [w_ref[0, e, i] if c == i else zblk for c in range(lanes // LRU_BLOCK_W)]
            rows.append(jnp.concatenate(blocks, axis=1))
        b_half = 0.5 * jnp.concatenate([ba_ref[0, e:e + 1], bx_ref[0, e:e + 1]], axis=1)
        b_hi = b_half.astype(BF16).astype(F32)
        b_lo = b_half - b_hi
        bias_rows = jnp.where(brow == 0, b_hi, jnp.where(brow == 1, b_lo, 0.0))
        wg.append(jnp.concatenate([jnp.concatenate(rows, axis=0), bias_rows], axis=0).astype(BF16))
    ones_cols = jnp.where(
        lax.broadcasted_iota(jnp.int32, (SCAN_PART * SUBLANES, lanes), 1) < 2, 1.0, 0.0).astype(BF16)

    def gates(t0, e):
        xh = xh_ref[t0:t0 + SCAN_PART].reshape(SCAN_PART * SUBLANES, lanes)
        th = jnp.tanh(_dot(jnp.concatenate([xh.astype(BF16), ones_cols], axis=1), wg[e]))
        tr = th[:, :lanes]
        ti = th[:, lanes:]
        a = jnp.exp2(tr * hc[e:e + 1] + hc[e:e + 1])
        y = 1.0 - a * a
        mult = y * lax.rsqrt(jnp.maximum(y, 1e-30))
        u = (ti * xh + xh) * mult
        return a, u

    zero = jnp.zeros((SUBLANES, lanes), F32)
    one = jnp.ones((SUBLANES, lanes), F32)
    e0, q0, e1, q1 = zero, one, zero, one
    for part in range(t_len // SCAN_PART):
        t0 = part * SCAN_PART
        a, u = gates(t0, 0)
        for t in range(SCAN_PART):
            rows = slice(t * SUBLANES, (t + 1) * SUBLANES)
            e0 = a[rows] * e0 + u[rows]
            q0 = a[rows] * q0
            h0_ref[t0 + t] = e0
            p0_ref[t0 + t] = q0
        t1 = t_len - (part + 1) * SCAN_PART
        a, u = gates(t1, 1)
        for t in reversed(range(SCAN_PART)):
            rows = slice(t * SUBLANES, (t + 1) * SUBLANES)
            e1 = a[rows] * e1 + u[rows]
            q1 = a[rows] * q1
            h1_ref[t1 + t] = e1
            p1_ref[t1 + t] = q1

    c0 = zero
    c1 = zero
    for _ in range(N_SEG - 1):
        c0 = _shift_seg_down(e0 + q0 * c0)
        c1 = _shift_seg_up(e1 + q1 * c1)

    def out_body(i, carry):
        t0 = pl.multiple_of(i * GATE_CHUNK, GATE_CHUNK)
        sl = pl.ds(t0, GATE_CHUNK)
        h = (h0_ref[sl] + p0_ref[sl] * c0[None]) + (h1_ref[sl] + p1_ref[sl] * c1[None])
        ystage_ref[...] = h.reshape(GATE_CHUNK * SUBLANES, lanes)
        for seg in range(N_SEG):
            y_ref[0, pl.ds(seg * t_len + t0, GATE_CHUNK)] = (
                ystage_ref[pl.ds(seg, GATE_CHUNK, stride=SUBLANES)])
        return carry

    lax.fori_loop(0, n_chunks, out_body, 0)


def _lru(xr_il, conv_w, conv_b, wa, wx, ba, bx, lam):
    b, t_len, n_seg, d = xr_il.shape
    n_groups = d // LANES
    blocks_per_group = LANES // LRU_BLOCK_W
    slab = pltpu.VMEM((t_len, n_seg, LANES), F32)
    w_spec = pl.BlockSpec((1, 2, blocks_per_group, LRU_BLOCK_W, LRU_BLOCK_W),
                          lambda i, j: (0, 0, j, 0, 0))
    lane_group = pl.BlockSpec((1, 2, LANES), lambda i, j: (0, 0, j))
    return pl.pallas_call(
        _lru_kernel,
        grid=(b, n_groups),
        in_specs=[
            pl.BlockSpec((1, t_len, n_seg, LANES), lambda i, j: (i, 0, 0, j)),
            pl.BlockSpec((1, CONV_W, LANES), lambda i, j: (0, 0, j)),
            pl.BlockSpec((1, LANES), lambda i, j: (0, j)),
            w_spec, w_spec, lane_group, lane_group, lane_group,
        ],
        out_specs=pl.BlockSpec((1, t_len * n_seg, LANES), lambda i, j: (i, 0, j)),
        out_shape=jax.ShapeDtypeStruct((b, t_len * n_seg, d), F32),
        scratch_shapes=[slab] * 5 + [
            pltpu.VMEM((GATE_CHUNK * n_seg, LANES), F32)],
        compiler_params=pltpu.CompilerParams(
            dimension_semantics=("arbitrary", "arbitrary"), vmem_limit_bytes=VMEM_LIMIT),
        name="rglru",
    )(xr_il, conv_w, conv_b, wa, wx, ba, bx, lam)


def _tail_kernel(x_ref, ya_ref, hl_ref, xg_ref, p_ref, glru_ref, gmlp_ref, gple_ref, gfin_ref,
                 wout_ref, wup_ref, wdown_ref, wgate_ref, wproj_ref, o_ref):
    x = x_ref[0]
    yl = hl_ref[0] * _gelu_tanh(xg_ref[0])
    yl_n = (yl * _rms(yl, -1) * glru_ref[...]).astype(BF16)
    h = x + _dot(ya_ref[0], wout_ref[:D_ATTN]) + _dot(yl_n, wout_ref[D_ATTN:])
    hn = (h * _rms(h, -1) * gmlp_ref[...]).astype(BF16)
    m = _dot(hn, wup_ref[...])
    act = jnp.square(jnp.maximum(m, 0.0)).astype(BF16)
    h = h + _dot(act, wdown_ref[...])
    hn = (h * _rms(h, -1) * gple_ref[...]).astype(BF16)
    gate = _sigmoid(_dot(hn, wgate_ref[...]))
    h = h + gate * _dot(p_ref[0, 0].astype(BF16), wproj_ref[...])
    o_ref[0] = h * _rms(h, -1) * gfin_ref[...]


def _tail(x, ya, hl, xg_il, p, glru, gmlp, gple, gfin, wout, wup, wdown, wgate, wproj):
    b, s, d = x.shape
    t = s // N_SEG
    const = lambda *_: (0, 0)
    weights = (wout, wup, wdown, wgate, wproj)

    def resident(arr):
        return pl.BlockSpec(arr.shape, const, pipeline_mode=pl.Buffered(1))

    return pl.pallas_call(
        _tail_kernel,
        grid=(b, N_SEG),
        in_specs=[
            pl.BlockSpec((1, t, d), lambda i, j: (i, j, 0)),
            pl.BlockSpec((1, t, D_ATTN), lambda i, j: (i, j, 0)),
            pl.BlockSpec((1, t, D_LRU), lambda i, j: (i, j, 0)),
            pl.BlockSpec((1, t, D_LRU), lambda i, j: (i, 0, j)),
            pl.BlockSpec((1, 1, t, D_PLE), lambda i, j: (0, i, j, 0)),
            resident(glru), resident(gmlp), resident(gple), resident(gfin),
        ] + [resident(w) for w in weights],
        out_specs=pl.BlockSpec((1, t, d), lambda i, j: (i, j, 0)),
        out_shape=jax.ShapeDtypeStruct((b, s, d), F32),
        compiler_params=pltpu.CompilerParams(
            dimension_semantics=("arbitrary", "arbitrary"), vmem_limit_bytes=VMEM_LIMIT),
        name="tail",
    )(x, ya, hl, xg_il, p, glru, gmlp, gple, gfin, *weights)


def _rope_tables_t(seq_len):
    pos = np.arange(seq_len)
    row = (pos // GRID_W).astype(np.float32)
    col = (pos % GRID_W).astype(np.float32)
    inv_freq = np.float32(ROPE_THETA) ** (-np.arange(N_FREQ, dtype=np.float32) / np.float32(N_FREQ))
    ang_r = (row[None, :] * inv_freq[:, None]).astype(np.float32)
    ang_c = (col[None, :] * inv_freq[:, None]).astype(np.float32)
    cos_t = np.concatenate([np.cos(ang_r)] * 2 + [np.cos(ang_c)] * 2, axis=0)
    sin_t = np.concatenate([-np.sin(ang_r), np.sin(ang_r), -np.sin(ang_c), np.sin(ang_c)], axis=0)
    return jnp.asarray(cos_t, F32), jnp.asarray(sin_t, F32)


def kernel(x, p, mix_norm, w_in, q_norm, k_norm, conv_w, conv_b, lru_wa, lru_ba, lru_wx, lru_bx,
           lru_lambda, attn_out_norm, lru_out_norm, w_out, mlp_norm, w_up, w_down, ple_norm,
           w_ple_gate, w_ple_proj, final_norm):
    b, s, d = x.shape
    assert w_in.shape[0] == 1, "single-layer trunk: the final norm is fused into the layer tail"
    cos_t, sin_t = _rope_tables_t(s)
    qt, k, vt, xr, xg, kn2 = _in_proj(x, mix_norm, w_in, q_norm, k_norm, cos_t, sin_t)
    ya, tail_weights = _attention(qt, k, vt, kn2, attn_out_norm,
                                  (w_out, w_up, w_down, w_ple_gate, w_ple_proj))
    hl = _lru(xr.reshape(b, s // N_SEG, N_SEG, D_LRU), conv_w, conv_b, lru_wa, lru_wx, lru_ba, lru_bx,
              lru_lambda)
    return _tail(x, ya, hl, xg, p, lru_out_norm, mlp_norm, ple_norm, final_norm.reshape(1, -1),
                 *tail_weights)
```
